```python
import math
import jax
import jax.numpy as jnp
from jax import lax
import numpy as np

D_MODEL = 1024
BATCH = 16
SEQ = 4096
DEPTH = 1

CHUNK = 64
GLA_HEADS = 4
GLA_DK = 64
GLA_DV = 128
GLA_GATE_RANK = 16
GLA_TAU = 16.0
ATT_HEADS = 8
ATT_DH = 64
IDX_HEADS = 8
IDX_DH = 32
TOPK_MAX = 256
Q_BLOCK = CHUNK
REL_BUCKETS = 32
REL_MAX_DIST = 128
N_GROUPS = 4
EXPERTS_PER_GROUP = 8
N_EXPERTS = N_GROUPS * EXPERTS_PER_GROUP
EXPERT_TOPK = 2
D_EXPERT = 128
EPS = 1e-6

GLA_QK_W = GLA_HEADS * GLA_DK
GLA_V_W = GLA_HEADS * GLA_DV
ATT_W = ATT_HEADS * ATT_DH
IDX_Q_W = IDX_HEADS * IDX_DH
SPLITS = (GLA_QK_W, GLA_QK_W, GLA_V_W, GLA_V_W, GLA_GATE_RANK, ATT_W, ATT_W, ATT_W, IDX_Q_W, IDX_DH, IDX_HEADS, D_MODEL, D_MODEL)
D_IN = 2 * GLA_QK_W + 2 * GLA_V_W + GLA_GATE_RANK + 3 * ATT_W + IDX_Q_W + IDX_DH + IDX_HEADS + 2 * D_MODEL

kernel_name = 'hybrid_gla_dsa_hmoe_block'


def rms_norm(x, g):
    xf = x.astype(jnp.float32)
    y = xf * lax.rsqrt(jnp.mean(xf * xf, axis=-1, keepdims=True) + EPS)
    return (y * g.astype(jnp.float32)).astype(x.dtype)


def gla_mixer(q, k, v, r, a_lr, w_alpha2, b_alpha, g_out):
    b, t, _ = q.shape
    nc = t // CHUNK
    f32 = jnp.float32
    q = q.reshape(b, nc, CHUNK, GLA_HEADS, GLA_DK).astype(f32) * (GLA_DK ** -0.5)
    k = k.reshape(b, nc, CHUNK, GLA_HEADS, GLA_DK)
    v = v.reshape(b, nc, CHUNK, GLA_HEADS, GLA_DV)
    log_a = jax.nn.log_sigmoid((a_lr @ w_alpha2 + b_alpha).astype(f32)) / GLA_TAU
    log_a = log_a.reshape(b, nc, CHUNK, GLA_HEADS, GLA_DK)
    cum = jnp.cumsum(log_a, axis=2)
    total = cum[:, :, -1]
    k_dec = (k.astype(f32) * jnp.exp(total[:, :, None] - cum)).astype(k.dtype)
    u = jnp.einsum('bnchd,bnche->nbhde', k_dec, v).astype(f32)

    def step(s, inp):
        dec, u_c = inp
        s = s * dec[..., None] + u_c
        return s, s

    s0 = jnp.zeros((b, GLA_HEADS, GLA_DK, GLA_DV), f32)
    _, states = lax.scan(step, s0, (jnp.exp(total).transpose(1, 0, 2, 3), u))
    o = jnp.einsum('bnchd,nbhde->bnche', q, states)
    o = rms_norm(o, g_out)
    o = o.reshape(b, t, GLA_V_W) * jax.nn.silu(r.astype(f32))
    return o.astype(r.dtype)


def t5_bucket(rel):
    half = REL_BUCKETS // 2
    exact = half // 2
    n = jnp.abs(rel)
    nf = jnp.maximum(n, 1).astype(jnp.float32)
    large = exact + (jnp.log(nf / exact) / math.log(REL_MAX_DIST / exact) * (half - exact)).astype(jnp.int32)
    large = jnp.minimum(large, half - 1)
    return jnp.where(rel > 0, half, 0) + jnp.where(n < exact, n, large)


def dsa_mixer(q, k, v, q_idx, k_idx, w_idx, rel_bias):
    b, t = q.shape[:2]
    k_top = min(TOPK_MAX, t // 4)
    nb = t // Q_BLOCK
    key_pos = jnp.arange(t)
    k_idx_f = k_idx.astype(jnp.float32)

    def to_blocks(a):
        return a.reshape(b, nb, Q_BLOCK, *a.shape[2:]).swapaxes(0, 1)

    def block(args):
        qb, qib, wb, i = args
        s = jnp.einsum('bqhd,bsd->bqhs', qib.astype(jnp.float32), k_idx_f)
        score = jnp.einsum('bqhs,bqh->bqs', jax.nn.relu(s), wb.astype(jnp.float32))
        admissible = key_pos < (i + 1) * Q_BLOCK
        score = jnp.where(admissible[None, None, :], score, -jnp.inf)
        top_val, top_idx = lax.top_k(score, k_top)
        valid = jnp.isfinite(top_val)
        k_sel = jax.vmap(lambda kk, ii: kk[ii])(k, top_idx)
        v_sel = jax.vmap(lambda vv, ii: vv[ii])(v, top_idx)
        logits = jnp.einsum('bqhd,bqkhd->bqhk', qb, k_sel).astype(jnp.float32) * (ATT_DH ** -0.5)
        q_pos = i * Q_BLOCK + jnp.arange(Q_BLOCK)
        bias = rel_bias[t5_bucket(top_idx - q_pos[None, :, None])]
        logits = logits + bias.astype(jnp.float32).transpose(0, 1, 3, 2)
        logits = jnp.where(valid[:, :, None, :], logits, -jnp.inf)
        p = jax.nn.softmax(logits, axis=-1)
        return jnp.einsum('bqhk,bqkhd->bqhd', p.astype(v.dtype), v_sel)

    out = lax.map(block, (to_blocks(q), to_blocks(q_idx), to_blocks(w_idx), jnp.arange(nb)))
    return out.swapaxes(0, 1).reshape(b, t, ATT_W)


def hier_moe(h, w_rg, b_rg, w_re, b_re, w_gate, w_up, w_down):
    b, t, d = h.shape
    f32 = jnp.float32
    xf = h.reshape(b * t, d)
    g_logits = (xf @ w_rg + b_rg).astype(f32)
    g_sel = jnp.argmax(g_logits, axis=-1)
    g_w = jnp.take_along_axis(jax.nn.softmax(g_logits, axis=-1), g_sel[:, None], axis=-1)
    e_logits = (xf @ w_re + b_re).astype(f32).reshape(-1, N_GROUPS, EXPERTS_PER_GROUP)
    e_in_group = jnp.take_along_axis(e_logits, g_sel[:, None, None], axis=1)[:, 0]
    top_v, top_i = lax.top_k(e_in_group, EXPERT_TOPK)
    top_w = jax.nn.softmax(top_v, axis=-1) * g_w
    within = jnp.sum(jax.nn.one_hot(top_i, EXPERTS_PER_GROUP, dtype=f32) * top_w[..., None], axis=1)
    combine = (jax.nn.one_hot(g_sel, N_GROUPS, dtype=f32)[:, :, None] * within[:, None, :]).reshape(-1, N_EXPERTS)
    y = jnp.zeros((b * t, d), f32)
    for e in range(N_EXPERTS):
        hid = jax.nn.silu(xf @ w_gate[e]) * (xf @ w_up[e])
        y = y + combine[:, e:e + 1] * (hid @ w_down[e]).astype(f32)
    return y.astype(h.dtype).reshape(b, t, d)


def setup_inputs(seed: int = 0) -> dict:
    key = jax.random.key(seed)
    ks = jax.random.split(key, 20)
    f32 = jnp.float32

    def nrm(k, shape, scale):
        return jax.random.normal(k, shape, f32) * scale

    def gain(k, shape):
        return 1.0 + 0.02 * jax.random.normal(k, shape, f32)

    L = DEPTH
    return {
        'x': nrm(ks[0], (BATCH, SEQ, D_MODEL), 1.0),
        'g_mix': gain(ks[1], (L, D_MODEL)),
        'w_in': nrm(ks[2], (L, D_MODEL, D_IN), D_MODEL ** -0.5),
        'w_alpha2': nrm(ks[3], (L, GLA_GATE_RANK, GLA_QK_W), GLA_GATE_RANK ** -0.5),
        'b_alpha': nrm(ks[4], (L, GLA_QK_W), 0.1),
        'g_gla': gain(ks[5], (L, GLA_DV)),
        'w_br_gla': nrm(ks[6], (L, GLA_V_W, D_MODEL), GLA_V_W ** -0.5),
        'g_q': gain(ks[7], (L, ATT_DH)),
        'g_k': gain(ks[8], (L, ATT_DH)),
        'rel_bias': nrm(ks[9], (REL_BUCKETS, ATT_HEADS), 0.3),
        'w_br_att': nrm(ks[10], (L, ATT_W, D_MODEL), ATT_W ** -0.5),
        'w_out': nrm(ks[11], (L, D_MODEL, D_MODEL), D_MODEL ** -0.5),
        'g_ffn': gain(ks[12], (L, D_MODEL)),
        'w_rg': nrm(ks[13], (L, D_MODEL, N_GROUPS), D_MODEL ** -0.5),
        'b_rg': nrm(ks[14], (L, N_GROUPS), 0.01),
        'w_re': nrm(ks[15], (L, D_MODEL, N_EXPERTS), D_MODEL ** -0.5),
        'b_re': nrm(ks[16], (L, N_EXPERTS), 0.01),
        'w_gate': nrm(ks[17], (L, N_EXPERTS, D_MODEL, D_EXPERT), D_MODEL ** -0.5),
        'w_up': nrm(ks[18], (L, N_EXPERTS, D_MODEL, D_EXPERT), D_MODEL ** -0.5),
        'w_down': nrm(ks[19], (L, N_EXPERTS, D_EXPERT, D_MODEL), D_EXPERT ** -0.5),
    }


def reference(x, g_mix, w_in, w_alpha2, b_alpha, g_gla, w_br_gla, g_q, g_k, rel_bias, w_br_att, w_out, g_ffn, w_rg, b_rg, w_re, b_re, w_gate, w_up, w_down):
    b, t, _ = x.shape
    split_points = [int(p) for p in np.cumsum(SPLITS)[:-1]]
    for l in range(DEPTH):
        hn = rms_norm(x, g_mix[l])
        proj = hn @ w_in[l]
        (qa, ka, va, ra, alr, qb, kb, vb, qi, ki, wi, gate_a, gate_b) = jnp.split(proj, split_points, axis=-1)
        ya = gla_mixer(qa, ka, va, ra, alr, w_alpha2[l], b_alpha[l], g_gla[l]) @ w_br_gla[l]
        qb = rms_norm(qb.reshape(b, t, ATT_HEADS, ATT_DH), g_q[l])
        kb = rms_norm(kb.reshape(b, t, ATT_HEADS, ATT_DH), g_k[l])
        vb = vb.reshape(b, t, ATT_HEADS, ATT_DH)
        qi = qi.reshape(b, t, IDX_HEADS, IDX_DH)
        wi = wi * (IDX_HEADS ** -0.5 * IDX_DH ** -0.5)
        yb = dsa_mixer(qb, kb, vb, qi, ki, wi, rel_bias) @ w_br_att[l]
        mixed = jax.nn.sigmoid(gate_a) * ya + jax.nn.sigmoid(gate_b) * yb
        x = x + mixed @ w_out[l]
        x = x + hier_moe(rms_norm(x, g_ffn[l]), w_rg[l], b_rg[l], w_re[l], b_re[l], w_gate[l], w_up[l], w_down[l])
    return x
```

```python
import functools
import math

import jax
import jax.numpy as jnp
from jax import lax
from jax.experimental import pallas as pl
from jax.experimental.pallas import tpu as pltpu

F32 = jnp.float32
BF16 = jnp.bfloat16
I32 = jnp.int32

CHUNK = 64
GLA_HEADS = 4
GLA_DK = 64
GLA_DV = 128
GLA_GATE_RANK = 16
GLA_TAU = 16.0
ATT_HEADS = 8
ATT_DH = 64
IDX_HEADS = 8
IDX_DH = 32
TOPK_MAX = 256
REL_BUCKETS = 32
REL_MAX_DIST = 128
N_GROUPS = 4
EXPERTS_PER_GROUP = 8
N_EXPERTS = N_GROUPS * EXPERTS_PER_GROUP
D_EXPERT = 128
EPS = 1e-6

GLA_QK_W = GLA_HEADS * GLA_DK
GLA_V_W = GLA_HEADS * GLA_DV
ATT_W = ATT_HEADS * ATT_DH
IDX_Q_W = IDX_HEADS * IDX_DH

LANES = 128
VMEM_LIMIT = 56 * 1024 * 1024

SM_ALR = 0
SM_KI = GLA_GATE_RANK
SM_WI = GLA_GATE_RANK + IDX_DH

NEG = -1e30
INT_MIN = -2 ** 31
KEY_NEG_INF = -2139095041

ROW_TILE = 512
Q_TILE = 128
K_TILE = 512
S_TILE = 256


def _dot(a, b):
    return jnp.dot(a, b, preferred_element_type=F32)


def _dot_nt(a, b):
    return lax.dot_general(a, b, (((1,), (1,)), ((), ())), preferred_element_type=F32)


def _dot_tn(a, b):
    return lax.dot_general(a, b, (((0,), (0,)), ((), ())), preferred_element_type=F32)


def _split_bf16(a):
    hi = a.astype(BF16)
    lo = (a - hi.astype(F32)).astype(BF16)
    return hi, lo


_PROJ = (("qa", GLA_QK_W), ("ka", GLA_QK_W), ("va", GLA_V_W), ("ra", GLA_V_W),
         ("qb", ATT_W), ("kb", ATT_W), ("vb", ATT_W), ("qi", IDX_Q_W),
         ("ga", None), ("gb", None), ("sm", LANES))


def _inproj_kernel(x_ref, g_ref, w_ref, gq_ref, gk_ref, bd_ref, *out_refs, offs):
    x = x_ref[...]
    ms = jnp.mean(x * x, axis=-1, keepdims=True)
    hn = (x * lax.rsqrt(ms + EPS) * g_ref[...]).astype(BF16)

    def proj(name):
        c0, c1 = offs[name]
        return _dot(hn, w_ref[:, c0:c1])

    def head_norm(y, gain):
        ss = _dot((y * y).astype(BF16), bd_ref[...])
        return y * lax.rsqrt(ss * (1.0 / ATT_DH) + EPS) * gain

    qa_ref, ka_ref, va_ref, ra_ref, qb_ref, kb_ref, vb_ref, qi_ref, ga_ref, gb_ref, sm_ref = out_refs
    qa_ref[...] = (proj("qa") * (GLA_DK ** -0.5)).astype(BF16)
    ka_ref[...] = proj("ka").astype(BF16)
    va_ref[...] = proj("va").astype(BF16)
    ra_ref[...] = proj("ra").astype(BF16)
    qb_ref[...] = (head_norm(proj("qb"), gq_ref[...]) * (ATT_DH ** -0.5)).astype(BF16)
    kb_ref[...] = head_norm(proj("kb"), gk_ref[...]).astype(BF16)
    vb_ref[...] = proj("vb").astype(BF16)
    qi_ref[...] = proj("qi").astype(BF16)
    ga_ref[...] = proj("ga").astype(BF16)
    gb_ref[...] = proj("gb").astype(BF16)
    sm_ref[...] = proj("sm")


def _in_projection(x2, g_mix, w_in, g_q, g_k):
    n, d = x2.shape
    widths = {name: (d if w is None else w) for name, w in _PROJ}
    ref_order = (("qa", GLA_QK_W), ("ka", GLA_QK_W), ("va", GLA_V_W), ("ra", GLA_V_W),
                 ("alr", GLA_GATE_RANK), ("qb", ATT_W), ("kb", ATT_W), ("vb", ATT_W),
                 ("qi", IDX_Q_W), ("ki", IDX_DH), ("wi", IDX_HEADS), ("ga", d), ("gb", d))
    cols, c = {}, 0
    for name, w in ref_order:
        cols[name] = w_in[:, c:c + w]
        c += w
    small = jnp.concatenate(
        [cols["alr"], cols["ki"], cols["wi"],
         jnp.zeros((d, LANES - GLA_GATE_RANK - IDX_DH - IDX_HEADS), w_in.dtype)], axis=1)
    cols["sm"] = small
    w_cat = jnp.concatenate([cols[name] for name, _ in _PROJ], axis=1).astype(BF16)
    offs, c = {}, 0
    for name, _ in _PROJ:
        offs[name] = (c, c + widths[name])
        c += widths[name]
    d_cat = c

    head = jnp.arange(ATT_W) // ATT_DH
    blockdiag = (head[:, None] == head[None, :]).astype(BF16)
    gq = jnp.tile(g_q, ATT_HEADS)[None, :].astype(F32)
    gk = jnp.tile(g_k, ATT_HEADS)[None, :].astype(F32)

    tm = ROW_TILE
    const = lambda i: (0, 0)
    out_shape = [jax.ShapeDtypeStruct((n, widths[name]), F32 if name == "sm" else BF16)
                 for name, _ in _PROJ]
    out_specs = [pl.BlockSpec((tm, widths[name]), lambda i: (i, 0)) for name, _ in _PROJ]
    outs = pl.pallas_call(
        functools.partial(_inproj_kernel, offs=offs),
        grid=(n // tm,),
        in_specs=[pl.BlockSpec((tm, d), lambda i: (i, 0)),
                  pl.BlockSpec((1, d), const),
                  pl.BlockSpec((d, d_cat), const),
                  pl.BlockSpec((1, ATT_W), const),
                  pl.BlockSpec((1, ATT_W), const),
                  pl.BlockSpec((ATT_W, ATT_W), const)],
        out_specs=out_specs,
        out_shape=out_shape,
        compiler_params=pltpu.CompilerParams(dimension_semantics=("arbitrary",),
                                             vmem_limit_bytes=VMEM_LIMIT),
        name="in_projection",
    )(x2, g_mix[None, :].astype(F32), w_cat, gq, gk, blockdiag)
    return dict(zip([name for name, _ in _PROJ], outs))


def _gla_kernel(qa_ref, ka_ref, va_ref, ra_ref, sm_ref, ga_ref, wa2_ref, ba_ref, ltri_ref,
                lall_ref, gg_ref, wbr_ref, out_ref, st_ref, o_ref):
    @pl.when(pl.program_id(1) == 0)
    def _():
        st_ref[...] = jnp.zeros_like(st_ref)

    tb = qa_ref.shape[0]
    z = _dot(sm_ref[...].astype(BF16), wa2_ref[...]) + ba_ref[...]
    log_a = (jnp.minimum(z, 0.0) - jnp.log1p(jnp.exp(-jnp.abs(z)))) * (1.0 / GLA_TAU)
    la_hi, la_lo = _split_bf16(log_a)
    ltri = ltri_ref[...]
    lall = lall_ref[...]
    cum = _dot(ltri, la_hi) + _dot(ltri, la_lo)
    tot = _dot(lall, la_hi) + _dot(lall, la_lo)
    k_dec = (ka_ref[...].astype(F32) * jnp.exp(tot - cum)).astype(BF16)
    dec = jnp.exp(tot)

    states = [st_ref[h] for h in range(GLA_HEADS)]
    for c in range(tb // CHUNK):
        rows = slice(c * CHUNK, (c + 1) * CHUNK)
        for h in range(GLA_HEADS):
            kl = slice(h * GLA_DK, (h + 1) * GLA_DK)
            vl = slice(h * GLA_DV, (h + 1) * GLA_DV)
            u_t = _dot_tn(va_ref[rows, vl], k_dec[rows, kl])
            s = states[h] * dec[c * CHUNK:c * CHUNK + 1, kl] + u_t
            states[h] = s
            o_ref[rows, vl] = _dot_nt(qa_ref[rows, kl], s.astype(BF16))
    for h in range(GLA_HEADS):
        st_ref[h] = states[h]

    r = ra_ref[...].astype(F32)
    gated = []
    for h in range(GLA_HEADS):
        vl = slice(h * GLA_DV, (h + 1) * GLA_DV)
        oh = o_ref[:, vl]
        ms = jnp.mean(oh * oh, axis=-1, keepdims=True)
        oh = oh * lax.rsqrt(ms + EPS) * gg_ref[...]
        rh = r[:, vl]
        gated.append((oh * (rh * jax.nn.sigmoid(rh))).astype(BF16))
    og = jnp.concatenate(gated, axis=1)
    ya = _dot(og, wbr_ref[...])
    out_ref[...] = (jax.nn.sigmoid(ga_ref[...].astype(F32)) * ya).astype(BF16)


def _gla_mixer(p, w_alpha2, b_alpha, g_gla, w_br_gla, batch, seq):
    n = batch * seq
    d = w_br_gla.shape[1]
    tb = ROW_TILE
    nt = seq // tb
    wa2 = jnp.zeros((LANES, GLA_QK_W), F32).at[SM_ALR:SM_ALR + GLA_GATE_RANK].set(w_alpha2).astype(BF16)
    r = jnp.arange(tb)
    same = (r[:, None] // CHUNK) == (r[None, :] // CHUNK)
    ltri = (same & (r[None, :] <= r[:, None])).astype(BF16)
    lall = same.astype(BF16)
    row = lambda b, i: (b * nt + i, 0)
    const = lambda b, i: (0, 0)
    return pl.pallas_call(
        _gla_kernel,
        grid=(batch, nt),
        in_specs=[pl.BlockSpec((tb, GLA_QK_W), row), pl.BlockSpec((tb, GLA_QK_W), row),
                  pl.BlockSpec((tb, GLA_V_W), row), pl.BlockSpec((tb, GLA_V_W), row),
                  pl.BlockSpec((tb, LANES), row), pl.BlockSpec((tb, d), row),
                  pl.BlockSpec((LANES, GLA_QK_W), const), pl.BlockSpec((1, GLA_QK_W), const),
                  pl.BlockSpec((tb, tb), const), pl.BlockSpec((tb, tb), const),
                  pl.BlockSpec((1, GLA_DV), const), pl.BlockSpec((GLA_V_W, d), const)],
        out_specs=pl.BlockSpec((tb, d), row),
        out_shape=jax.ShapeDtypeStruct((n, d), BF16),
        scratch_shapes=[pltpu.VMEM((GLA_HEADS, GLA_DV, GLA_DK), F32),
                        pltpu.VMEM((tb, GLA_V_W), F32)],
        compiler_params=pltpu.CompilerParams(dimension_semantics=("arbitrary", "arbitrary"),
                                             vmem_limit_bytes=VMEM_LIMIT),
        name="gla_mixer",
    )(p["qa"], p["ka"], p["va"], p["ra"], p["sm"], p["ga"], wa2, b_alpha[None, :].astype(F32),
      ltri, lall, g_gla[None, :].astype(F32), w_br_gla.astype(BF16))


def _t5_bucket_int(rel):
    half = REL_BUCKETS // 2
    exact = half // 2
    n = jnp.abs(rel)
    large = jnp.full(rel.shape, exact, I32)
    for j in range(1, half - exact):
        thr = math.ceil(exact * (REL_MAX_DIST / exact) ** (j / (half - exact)) - 1e-9)
        large = large + jnp.where(n >= thr, 1, 0)
    return jnp.where(rel > 0, half, 0) + jnp.where(n < exact, n, large)


def _dsa_kernel(relb_ref, qb_ref, qi_ref, sm_ref, gb_ref, ya_ref, kb_ref, vb_ref, kit_ref,
                wbr_ref, out_ref, skey_ref, mb_ref, mbf_ref, vaug_ref, biasw_ref, q2_ref,
                qst_ref, acc_ref, m_ref, *, k_top, seq):
    b = pl.program_id(0)
    j = pl.program_id(1)
    tq = Q_TILE
    t0 = j * tq
    n_pairs = ATT_HEADS // 2

    @pl.when((b == 0) & (j == 0))
    def _():
        rq = lax.broadcasted_iota(I32, (tq, 2 * tq), 0)
        rc = lax.broadcasted_iota(I32, (tq, 2 * tq), 1)
        bucket = _t5_bucket_int(rc - tq - rq)
        far = REL_BUCKETS // 2 - 1
        for h in range(ATT_HEADS):
            a = jnp.zeros((tq, 2 * tq), F32)
            for bk in range(REL_BUCKETS):
                a = jnp.where(bucket == bk, relb_ref[bk * ATT_HEADS + h], a)
            biasw_ref[h] = a - relb_ref[far * ATT_HEADS + h]

    @pl.when(j == 0)
    def _():
        def body(i, _):
            rows = pl.ds(pl.multiple_of(i * K_TILE, K_TILE), K_TILE)
            for p in range(n_pairs):
                vaug_ref[rows, 2 * p * LANES:(2 * p + 1) * LANES] = vb_ref[0, rows, p * LANES:(p + 1) * LANES]
                vaug_ref[rows, (2 * p + 1) * LANES:(2 * p + 2) * LANES] = jnp.ones((K_TILE, LANES), BF16)
            return 0
        lax.fori_loop(0, seq // K_TILE, body, 0)

    lane_q = lax.broadcasted_iota(I32, (tq, LANES), 1)
    for h in range(ATT_HEADS):
        p = h // 2
        qp = qb_ref[:, p * LANES:(p + 1) * LANES]
        keep = (lane_q < ATT_DH) if h % 2 == 0 else (lane_q >= ATT_DH)
        q2_ref[h] = jnp.where(keep, qp, jnp.zeros_like(qp))
    for h in range(IDX_HEADS):
        qst_ref[h * tq:(h + 1) * tq, :] = qi_ref[:, h * IDX_DH:(h + 1) * IDX_DH]
    wi = sm_ref[:, SM_WI:SM_WI + IDX_HEADS] * (IDX_HEADS ** -0.5 * IDX_DH ** -0.5)
    wb = [jnp.broadcast_to(wi[:, h:h + 1], (tq, S_TILE)) for h in range(IDX_HEADS)]

    n_sel = (j + 2) // 2
    row_s = lax.broadcasted_iota(I32, (tq, S_TILE), 0)
    lane_s = lax.broadcasted_iota(I32, (tq, S_TILE), 1)
    limit = t0 + ((row_s >> 6) + 1) * CHUNK

    def score_body(kt, _):
        ks = pl.multiple_of(kt * S_TILE, S_TILE)
        s_all = _dot(qst_ref[...], kit_ref[0, :, pl.ds(ks, S_TILE)])
        score = jnp.zeros((tq, S_TILE), F32)
        for h in range(IDX_HEADS):
            score = score + jnp.maximum(s_all[h * tq:(h + 1) * tq], 0.0) * wb[h]
        score = jnp.where(ks + lane_s < limit, score, -jnp.inf)
        bits = lax.bitcast_convert_type(score, I32)
        skey_ref[:, pl.ds(ks, S_TILE)] = jnp.where(bits < 0, bits ^ 0x7FFFFFFF, bits)
        return 0
    lax.fori_loop(0, n_sel, score_body, 0)

    kf = float(k_top)

    def count_ge(cand):
        cb = jnp.broadcast_to(cand, (tq, S_TILE))

        def body(kt, c):
            sk = skey_ref[:, pl.ds(pl.multiple_of(kt * S_TILE, S_TILE), S_TILE)]
            return c + jnp.where(sk >= cb, 1.0, 0.0)
        c = lax.fori_loop(0, n_sel, body, jnp.zeros((tq, S_TILE), F32))
        return jnp.sum(c, axis=-1, keepdims=True)

    cnt0 = count_ge(jnp.zeros((tq, 1), I32))
    take0 = cnt0 >= kf
    ans0 = jnp.where(take0, 0, INT_MIN).astype(I32)
    cnt_ans0 = jnp.where(take0, cnt0, (n_sel * S_TILE).astype(F32))

    def bisect(it, carry):
        ans, cnt_ans = carry
        cand = ans + lax.shift_left(jnp.int32(1), 30 - it)
        cnt = count_ge(cand)
        take = cnt >= kf
        return jnp.where(take, cand, ans), jnp.where(take, cnt, cnt_ans)
    thr, cnt_thr = lax.fori_loop(0, 31, bisect, (ans0, cnt_ans0))

    has_thr = thr > KEY_NEG_INF
    tie = has_thr & (cnt_thr > kf)
    any_tie = jnp.max(jnp.where(tie, 1.0, 0.0)) > 0.0
    far_end = t0 - tq

    def write_mask(ks, width, sel):
        pos = ks + lax.broadcasted_iota(I32, (tq, width), 1)
        mb_ref[:, pl.ds(ks, width)] = jnp.where(sel, 0.0, NEG)
        mbf_ref[:, pl.ds(ks, width)] = jnp.where(sel & (pos < far_end), 0.0, NEG)

    @pl.when(jnp.logical_not(any_tie))
    def _():
        lo = jnp.broadcast_to(jnp.maximum(thr, KEY_NEG_INF + 1), (tq, S_TILE))

        def body(kt, _):
            ks = pl.multiple_of(kt * S_TILE, S_TILE)
            write_mask(ks, S_TILE, skey_ref[:, pl.ds(ks, S_TILE)] >= lo)
            return 0
        lax.fori_loop(0, n_sel, body, 0)

    @pl.when(any_tie)
    def _():
        need = kf - count_ge(thr + 1)
        thr_b = jnp.broadcast_to(thr, (tq, LANES))
        lo = jnp.broadcast_to(jnp.maximum(thr, KEY_NEG_INF), (tq, LANES))
        eq_ok = jnp.broadcast_to(has_thr, (tq, LANES))
        ri = lax.broadcasted_iota(I32, (LANES, LANES), 0)
        ci = lax.broadcasted_iota(I32, (LANES, LANES), 1)
        tri = jnp.where(ri <= ci, 1.0, 0.0).astype(BF16)

        def body(kt, seen):
            ks = pl.multiple_of(kt * LANES, LANES)
            sk = skey_ref[:, pl.ds(ks, LANES)]
            eq = (sk == thr_b) & eq_ok
            prefix = seen + _dot(jnp.where(eq, 1.0, 0.0).astype(BF16), tri)
            write_mask(ks, LANES, (sk > lo) | (eq & (prefix <= need)))
            return prefix[:, LANES - 1:LANES]
        lax.fori_loop(0, n_sel * (S_TILE // LANES), body, jnp.zeros((tq, 1), F32))

    acc_ref[...] = jnp.zeros_like(acc_ref)
    m_ref[...] = jnp.full(m_ref.shape, NEG, F32)

    def attend(ks, width, bias_ref, bias_c0, mask_ref):
        rows = pl.ds(ks, width)
        mbt = mask_ref[:, rows]
        for h in range(ATT_HEADS):
            p = h // 2
            s = _dot_nt(q2_ref[h], kb_ref[0, rows, p * LANES:(p + 1) * LANES]) + mbt
            if bias_ref is not None:
                s = s + bias_ref[h, :, bias_c0:bias_c0 + width]
            m_old = m_ref[h]
            m_new = jnp.maximum(m_old, jnp.max(s, axis=-1, keepdims=True))
            pexp = jnp.exp(s - m_new).astype(BF16)
            pv = _dot(pexp, vaug_ref[rows, 2 * p * LANES:(2 * p + 2) * LANES])
            acc_ref[h] = acc_ref[h] * jnp.exp(m_old - m_new) + pv
            m_ref[h] = m_new

    def far_body(kt, _):
        attend(pl.multiple_of(kt * K_TILE, K_TILE), K_TILE, None, 0, mbf_ref)
        return 0
    lax.fori_loop(0, (j + 2) // 4, far_body, 0)

    @pl.when(j > 0)
    def _():
        attend(pl.multiple_of(t0 - tq, tq), tq, biasw_ref, 0, mb_ref)
    attend(pl.multiple_of(t0, tq), tq, biasw_ref, tq, mb_ref)

    lane_o = lax.broadcasted_iota(I32, (tq, LANES), 1)
    outs = []
    for p in range(n_pairs):
        a0 = acc_ref[2 * p]
        a1 = acc_ref[2 * p + 1]
        o0 = a0[:, :LANES] / a0[:, LANES:LANES + 1]
        o1 = a1[:, :LANES] / a1[:, LANES:LANES + 1]
        outs.append(jnp.where(lane_o < ATT_DH, o0, o1).astype(BF16))
    yb = _dot(jnp.concatenate(outs, axis=1), wbr_ref[...])
    mixed = jax.nn.sigmoid(gb_ref[...].astype(F32)) * yb + ya_ref[...].astype(F32)
    out_ref[...] = mixed.astype(BF16)


def _dsa_mixer(p, ya, rel_bias, w_br_att, batch, seq):
    n = batch * seq
    d = w_br_att.shape[1]
    tq = Q_TILE
    nq = seq // tq
    k_top = min(TOPK_MAX, seq // 4)
    kb3 = p["kb"].reshape(batch, seq, ATT_W)
    vb3 = p["vb"].reshape(batch, seq, ATT_W)
    kit = p["sm"][:, SM_KI:SM_KI + IDX_DH].astype(BF16).reshape(batch, seq, IDX_DH).swapaxes(1, 2)
    row = lambda b, j: (b * nq + j, 0)
    per_b = lambda b, j: (b, 0, 0)
    const = lambda b, j: (0, 0)
    return pl.pallas_call(
        functools.partial(_dsa_kernel, k_top=k_top, seq=seq),
        grid=(batch, nq),
        in_specs=[pl.BlockSpec(memory_space=pltpu.SMEM),
                  pl.BlockSpec((tq, ATT_W), row), pl.BlockSpec((tq, IDX_Q_W), row),
                  pl.BlockSpec((tq, LANES), row), pl.BlockSpec((tq, d), row),
                  pl.BlockSpec((tq, d), row),
                  pl.BlockSpec((1, seq, ATT_W), per_b), pl.BlockSpec((1, seq, ATT_W), per_b),
                  pl.BlockSpec((1, IDX_DH, seq), per_b),
                  pl.BlockSpec((ATT_W, d), const)],
        out_specs=pl.BlockSpec((tq, d), row),
        out_shape=jax.ShapeDtypeStruct((n, d), BF16),
        scratch_shapes=[pltpu.VMEM((tq, seq), I32),
                        pltpu.VMEM((tq, seq), F32),
                        pltpu.VMEM((tq, seq), F32),
                        pltpu.VMEM((seq, 2 * ATT_W), BF16),
                        pltpu.VMEM((ATT_HEADS, tq, 2 * tq), F32),
                        pltpu.VMEM((ATT_HEADS, tq, LANES), BF16),
                        pltpu.VMEM((IDX_HEADS * tq, IDX_DH), BF16),
                        pltpu.VMEM((ATT_HEADS, tq, 2 * LANES), F32),
                        pltpu.VMEM((ATT_HEADS, tq, 1), F32)],
        compiler_params=pltpu.CompilerParams(dimension_semantics=("arbitrary", "arbitrary"),
                                             vmem_limit_bytes=VMEM_LIMIT),
        name="dsa_mixer",
    )(rel_bias.astype(F32).reshape(-1), p["qb"], p["qi"], p["sm"], p["gb"], ya, kb3, vb3, kit,
      w_br_att.astype(BF16))


RT_G0 = 0
RT_E0 = N_GROUPS


def _moe_kernel(x_ref, mx_ref, wo_ref, gf_ref, wrh_ref, wrl_ref, br_ref, ex_ref, wg_ref, wu_ref,
                wd_ref, out_ref, x1_ref, h2_ref, comb_ref, y_ref):
    g = pl.program_id(1)
    tm = x_ref.shape[0]

    @pl.when(g == 0)
    def _():
        x1 = x_ref[...] + _dot(mx_ref[...], wo_ref[...])
        x1_ref[...] = x1
        ms = jnp.mean(x1 * x1, axis=-1, keepdims=True)
        h = x1 * lax.rsqrt(ms + EPS) * gf_ref[...]
        h_hi, h_lo = _split_bf16(h)
        h2_ref[...] = h_hi
        logits = (_dot(h_hi, wrh_ref[...]) + _dot(h_lo, wrh_ref[...]) + _dot(h_hi, wrl_ref[...])
                  + br_ref[...])
        lane = lax.broadcasted_iota(I32, (tm, LANES), 1)
        big = jnp.int32(LANES)
        is_g = lane < N_GROUPS
        gl = jnp.where(is_g, logits, -jnp.inf)
        gmax = jnp.max(gl, axis=-1, keepdims=True)
        gidx = jnp.min(jnp.where(gl == gmax, lane, big), axis=-1, keepdims=True)
        gsum = jnp.sum(jnp.where(is_g, jnp.exp(gl - gmax), 0.0), axis=-1, keepdims=True)
        g_w = 1.0 / gsum
        in_grp = (lane >= RT_E0) & (lane < RT_E0 + N_EXPERTS) & (((lane - RT_E0) >> 3) == gidx)
        ev = jnp.where(in_grp, logits, -jnp.inf)
        t1 = jnp.max(ev, axis=-1, keepdims=True)
        i1 = jnp.min(jnp.where(ev == t1, lane, big), axis=-1, keepdims=True)
        ev2 = jnp.where(lane == i1, -jnp.inf, ev)
        t2 = jnp.max(ev2, axis=-1, keepdims=True)
        i2 = jnp.min(jnp.where(ev2 == t2, lane, big), axis=-1, keepdims=True)
        e2 = jnp.exp(t2 - t1)
        w1 = 1.0 / (1.0 + e2)
        w2 = e2 * w1
        comb_ref[...] = g_w * (jnp.where(lane == i1, w1, 0.0) + jnp.where(lane == i2, w2, 0.0))
        y_ref[...] = jnp.zeros_like(y_ref)

    c_hi, c_lo = _split_bf16(comb_ref[...])
    cexp = _dot(c_hi, ex_ref[0]) + _dot(c_lo, ex_ref[0])
    h2 = h2_ref[...]
    hg = _dot(h2, wg_ref[0])
    hu = _dot(h2, wu_ref[0])
    hid = (hg * jax.nn.sigmoid(hg) * hu * cexp).astype(BF16)
    y_ref[...] += _dot(hid, wd_ref[0])

    @pl.when(g == N_GROUPS - 1)
    def _():
        out_ref[...] = x1_ref[...] + y_ref[...]


def _out_proj_moe(x2, mixed, w_out, g_ffn, w_rg, b_rg, w_re, b_re, w_gate, w_up, w_down):
    n, d = x2.shape
    tm = ROW_TILE
    gw = EXPERTS_PER_GROUP * D_EXPERT
    w_r = jnp.zeros((d, LANES), F32).at[:, RT_G0:RT_G0 + N_GROUPS].set(w_rg)
    w_r = w_r.at[:, RT_E0:RT_E0 + N_EXPERTS].set(w_re)
    wr_hi = w_r.astype(BF16)
    wr_lo = (w_r - wr_hi.astype(F32)).astype(BF16)
    b_r = jnp.zeros((1, LANES), F32).at[0, RT_G0:RT_G0 + N_GROUPS].set(b_rg)
    b_r = b_r.at[0, RT_E0:RT_E0 + N_EXPERTS].set(b_re)
    lane = jnp.arange(LANES)[None, :, None]
    grp = jnp.arange(N_GROUPS)[:, None, None]
    col = jnp.arange(gw)[None, None, :]
    expand = (lane == RT_E0 + grp * EXPERTS_PER_GROUP + col // D_EXPERT).astype(BF16)

    def by_group(w):
        return (w.reshape(N_GROUPS, EXPERTS_PER_GROUP, d, D_EXPERT).transpose(0, 2, 1, 3)
                .reshape(N_GROUPS, d, gw).astype(BF16))
    wg = by_group(w_gate)
    wu = by_group(w_up)
    wd = w_down.reshape(N_GROUPS, gw, d).astype(BF16)

    row = lambda i, g: (i, 0)
    const = lambda i, g: (0, 0)
    grp3 = lambda i, g: (g, 0, 0)
    return pl.pallas_call(
        _moe_kernel,
        grid=(n // tm, N_GROUPS),
        in_specs=[pl.BlockSpec((tm, d), row), pl.BlockSpec((tm, d), row),
                  pl.BlockSpec((d, d), const), pl.BlockSpec((1, d), const),
                  pl.BlockSpec((d, LANES), const), pl.BlockSpec((d, LANES), const),
                  pl.BlockSpec((1, LANES), const),
                  pl.BlockSpec((1, LANES, gw), grp3),
                  pl.BlockSpec((1, d, gw), grp3), pl.BlockSpec((1, d, gw), grp3),
                  pl.BlockSpec((1, gw, d), grp3)],
        out_specs=pl.BlockSpec((tm, d), row),
        out_shape=jax.ShapeDtypeStruct((n, d), F32),
        scratch_shapes=[pltpu.VMEM((tm, d), F32), pltpu.VMEM((tm, d), BF16),
                        pltpu.VMEM((tm, LANES), F32), pltpu.VMEM((tm, d), F32)],
        compiler_params=pltpu.CompilerParams(dimension_semantics=("arbitrary", "arbitrary"),
                                             vmem_limit_bytes=VMEM_LIMIT),
        name="out_proj_moe",
    )(x2, mixed, w_out.astype(BF16), g_ffn[None, :].astype(F32), wr_hi, wr_lo, b_r, expand,
      wg, wu, wd)


def kernel(x, g_mix, w_in, w_alpha2, b_alpha, g_gla, w_br_gla, g_q, g_k, rel_bias, w_br_att, w_out, g_ffn, w_rg, b_rg, w_re, b_re, w_gate, w_up, w_down):
    batch, seq, d = x.shape
    assert seq % ROW_TILE == 0 and seq % K_TILE == 0 and (batch * seq) % ROW_TILE == 0
    x2 = x.reshape(batch * seq, d)
    for l in range(g_mix.shape[0]):
        p = _in_projection(x2, g_mix[l], w_in[l], g_q[l], g_k[l])
        ya = _gla_mixer(p, w_alpha2[l], b_alpha[l], g_gla[l], w_br_gla[l], batch, seq)
        mixed = _dsa_mixer(p, ya, rel_bias, w_br_att[l], batch, seq)
        x2 = _out_proj_moe(x2, mixed, w_out[l], g_ffn[l], w_rg[l], b_rg[l], w_re[l], b_re[l],
                           w_gate[l], w_up[l], w_down[l])
    return x2.reshape(batch, seq, d)
```

```python
import functools
import math

import jax
import jax.numpy as jnp
from jax import lax
from jax.experimental import pallas as pl
from jax.experimental.pallas import tpu as pltpu

F32 = jnp.float32
BF16 = jnp.bfloat16
I32 = jnp.int32

CHUNK = 64
GLA_HEADS = 4
GLA_DK = 64
GLA_DV = 128
GLA_GATE_RANK = 16
GLA_TAU = 16.0
ATT_HEADS = 8
ATT_DH = 64
IDX_HEADS = 8
IDX_DH = 32
TOPK_MAX = 256
REL_BUCKETS = 32
REL_MAX_DIST = 128
N_GROUPS = 4
EXPERTS_PER_GROUP = 8
N_EXPERTS = N_GROUPS * EXPERTS_PER_GROUP
D_EXPERT = 128
EPS = 1e-6

GLA_QK_W = GLA_HEADS * GLA_DK
GLA_V_W = GLA_HEADS * GLA_DV
ATT_W = ATT_HEADS * ATT_DH
IDX_Q_W = IDX_HEADS * IDX_DH

LANES = 128
VMEM_LIMIT = 56 * 1024 * 1024

SM_ALR = 0
SM_KI = GLA_GATE_RANK
SM_WI = GLA_GATE_RANK + IDX_DH

NEG = -1e30

ROW_TILE = 512
Q_TILE = 128
K_TILE = 512
S_TILE = 256


def _dot(a, b):
    return jnp.dot(a, b, preferred_element_type=F32)


def _dot_nt(a, b):
    return lax.dot_general(a, b, (((1,), (1,)), ((), ())), preferred_element_type=F32)


def _dot_tn(a, b):
    return lax.dot_general(a, b, (((0,), (0,)), ((), ())), preferred_element_type=F32)


def _split_bf16(a):
    hi = a.astype(BF16)
    lo = (a - hi.astype(F32)).astype(BF16)
    return hi, lo


_PROJ = (("qa", GLA_QK_W), ("ka", GLA_QK_W), ("va", GLA_V_W), ("ra", GLA_V_W),
         ("qb", ATT_W), ("kb", ATT_W), ("vb", ATT_W), ("qi", IDX_Q_W),
         ("ga", None), ("gb", None), ("sm", LANES))


def _inproj_kernel(x_ref, g_ref, w_ref, gq_ref, gk_ref, bd_ref, *out_refs, offs):
    x = x_ref[...]
    ms = jnp.mean(x * x, axis=-1, keepdims=True)
    hn = (x * lax.rsqrt(ms + EPS) * g_ref[...]).astype(BF16)

    def proj(name):
        c0, c1 = offs[name]
        return _dot(hn, w_ref[:, c0:c1])

    def head_norm(y, gain):
        ss = _dot((y * y).astype(BF16), bd_ref[...])
        return y * lax.rsqrt(ss * (1.0 / ATT_DH) + EPS) * gain

    qa_ref, ka_ref, va_ref, ra_ref, qb_ref, kb_ref, vb_ref, qi_ref, ga_ref, gb_ref, sm_ref = out_refs
    qa_ref[...] = (proj("qa") * (GLA_DK ** -0.5)).astype(BF16)
    ka_ref[...] = proj("ka").astype(BF16)
    va_ref[...] = proj("va").astype(BF16)
    ra_ref[...] = proj("ra").astype(BF16)
    qb_ref[...] = (head_norm(proj("qb"), gq_ref[...]) * (ATT_DH ** -0.5)).astype(BF16)
    kb_ref[...] = head_norm(proj("kb"), gk_ref[...]).astype(BF16)
    vb_ref[...] = proj("vb").astype(BF16)
    qi_ref[...] = proj("qi").astype(BF16)
    ga_ref[...] = proj("ga").astype(BF16)
    gb_ref[...] = proj("gb").astype(BF16)
    sm_ref[...] = proj("sm")


def _in_projection(x2, g_mix, w_in, g_q, g_k):
    n, d = x2.shape
    widths = {name: (d if w is None else w) for name, w in _PROJ}
    ref_order = (("qa", GLA_QK_W), ("ka", GLA_QK_W), ("va", GLA_V_W), ("ra", GLA_V_W),
                 ("alr", GLA_GATE_RANK), ("qb", ATT_W), ("kb", ATT_W), ("vb", ATT_W),
                 ("qi", IDX_Q_W), ("ki", IDX_DH), ("wi", IDX_HEADS), ("ga", d), ("gb", d))
    cols, c = {}, 0
    for name, w in ref_order:
        cols[name] = w_in[:, c:c + w]
        c += w
    small = jnp.concatenate(
        [cols["alr"], cols["ki"], cols["wi"],
         jnp.zeros((d, LANES - GLA_GATE_RANK - IDX_DH - IDX_HEADS), w_in.dtype)], axis=1)
    cols["sm"] = small
    w_cat = jnp.concatenate([cols[name] for name, _ in _PROJ], axis=1).astype(BF16)
    offs, c = {}, 0
    for name, _ in _PROJ:
        offs[name] = (c, c + widths[name])
        c += widths[name]
    d_cat = c

    head = jnp.arange(ATT_W) // ATT_DH
    blockdiag = (head[:, None] == head[None, :]).astype(BF16)
    gq = jnp.tile(g_q, ATT_HEADS)[None, :].astype(F32)
    gk = jnp.tile(g_k, ATT_HEADS)[None, :].astype(F32)

    tm = ROW_TILE
    const = lambda i: (0, 0)
    out_shape = [jax.ShapeDtypeStruct((n, widths[name]), F32 if name == "sm" else BF16)
                 for name, _ in _PROJ]
    out_specs = [pl.BlockSpec((tm, widths[name]), lambda i: (i, 0)) for name, _ in _PROJ]
    outs = pl.pallas_call(
        functools.partial(_inproj_kernel, offs=offs),
        grid=(n // tm,),
        in_specs=[pl.BlockSpec((tm, d), lambda i: (i, 0)),
                  pl.BlockSpec((1, d), const),
                  pl.BlockSpec((d, d_cat), const),
                  pl.BlockSpec((1, ATT_W), const),
                  pl.BlockSpec((1, ATT_W), const),
                  pl.BlockSpec((ATT_W, ATT_W), const)],
        out_specs=out_specs,
        out_shape=out_shape,
        compiler_params=pltpu.CompilerParams(dimension_semantics=("arbitrary",),
                                             vmem_limit_bytes=VMEM_LIMIT),
        name="in_projection",
    )(x2, g_mix[None, :].astype(F32), w_cat, gq, gk, blockdiag)
    return dict(zip([name for name, _ in _PROJ], outs))


def _gla_kernel(qa_ref, ka_ref, va_ref, ra_ref, sm_ref, ga_ref, wa2_ref, ba_ref, ltri_ref,
                lall_ref, gg_ref, wbr_ref, out_ref, st_ref, o_ref):
    @pl.when(pl.program_id(1) == 0)
    def _():
        st_ref[...] = jnp.zeros_like(st_ref)

    tb = qa_ref.shape[0]
    z = _dot(sm_ref[...].astype(BF16), wa2_ref[...]) + ba_ref[...]
    log_a = (jnp.minimum(z, 0.0) - jnp.log1p(jnp.exp(-jnp.abs(z)))) * (1.0 / GLA_TAU)
    la_hi, la_lo = _split_bf16(log_a)
    ltri = ltri_ref[...]
    lall = lall_ref[...]
    cum = _dot(ltri, la_hi) + _dot(ltri, la_lo)
    tot = _dot(lall, la_hi) + _dot(lall, la_lo)
    k_dec = (ka_ref[...].astype(F32) * jnp.exp(tot - cum)).astype(BF16)
    dec = jnp.exp(tot)

    states = [st_ref[h] for h in range(GLA_HEADS)]
    for c in range(tb // CHUNK):
        rows = slice(c * CHUNK, (c + 1) * CHUNK)
        for h in range(GLA_HEADS):
            kl = slice(h * GLA_DK, (h + 1) * GLA_DK)
            vl = slice(h * GLA_DV, (h + 1) * GLA_DV)
            u_t = _dot_tn(va_ref[rows, vl], k_dec[rows, kl])
            s = states[h] * dec[c * CHUNK:c * CHUNK + 1, kl] + u_t
            states[h] = s
            o_ref[rows, vl] = _dot_nt(qa_ref[rows, kl], s.astype(BF16))
    for h in range(GLA_HEADS):
        st_ref[h] = states[h]

    r = ra_ref[...].astype(F32)
    gated = []
    for h in range(GLA_HEADS):
        vl = slice(h * GLA_DV, (h + 1) * GLA_DV)
        oh = o_ref[:, vl]
        ms = jnp.mean(oh * oh, axis=-1, keepdims=True)
        oh = oh * lax.rsqrt(ms + EPS) * gg_ref[...]
        rh = r[:, vl]
        gated.append((oh * (rh * jax.nn.sigmoid(rh))).astype(BF16))
    og = jnp.concatenate(gated, axis=1)
    ya = _dot(og, wbr_ref[...])
    out_ref[...] = (jax.nn.sigmoid(ga_ref[...].astype(F32)) * ya).astype(BF16)


def _gla_mixer(p, w_alpha2, b_alpha, g_gla, w_br_gla, batch, seq):
    n = batch * seq
    d = w_br_gla.shape[1]
    tb = ROW_TILE
    nt = seq // tb
    wa2 = jnp.zeros((LANES, GLA_QK_W), F32).at[SM_ALR:SM_ALR + GLA_GATE_RANK].set(w_alpha2).astype(BF16)
    r = jnp.arange(tb)
    same = (r[:, None] // CHUNK) == (r[None, :] // CHUNK)
    ltri = (same & (r[None, :] <= r[:, None])).astype(BF16)
    lall = same.astype(BF16)
    row = lambda b, i: (b * nt + i, 0)
    const = lambda b, i: (0, 0)
    return pl.pallas_call(
        _gla_kernel,
        grid=(batch, nt),
        in_specs=[pl.BlockSpec((tb, GLA_QK_W), row), pl.BlockSpec((tb, GLA_QK_W), row),
                  pl.BlockSpec((tb, GLA_V_W), row), pl.BlockSpec((tb, GLA_V_W), row),
                  pl.BlockSpec((tb, LANES), row), pl.BlockSpec((tb, d), row),
                  pl.BlockSpec((LANES, GLA_QK_W), const), pl.BlockSpec((1, GLA_QK_W), const),
                  pl.BlockSpec((tb, tb), const), pl.BlockSpec((tb, tb), const),
                  pl.BlockSpec((1, GLA_DV), const), pl.BlockSpec((GLA_V_W, d), const)],
        out_specs=pl.BlockSpec((tb, d), row),
        out_shape=jax.ShapeDtypeStruct((n, d), BF16),
        scratch_shapes=[pltpu.VMEM((GLA_HEADS, GLA_DV, GLA_DK), F32),
                        pltpu.VMEM((tb, GLA_V_W), F32)],
        compiler_params=pltpu.CompilerParams(dimension_semantics=("arbitrary", "arbitrary"),
                                             vmem_limit_bytes=VMEM_LIMIT),
        name="gla_mixer",
    )(p["qa"], p["ka"], p["va"], p["ra"], p["sm"], p["ga"], wa2, b_alpha[None, :].astype(F32),
      ltri, lall, g_gla[None, :].astype(F32), w_br_gla.astype(BF16))


def _t5_bucket_int(rel):
    half = REL_BUCKETS // 2
    exact = half // 2
    n = jnp.abs(rel)
    large = jnp.full(rel.shape, exact, I32)
    for j in range(1, half - exact):
        thr = math.ceil(exact * (REL_MAX_DIST / exact) ** (j / (half - exact)) - 1e-9)
        large = large + jnp.where(n >= thr, 1, 0)
    return jnp.where(rel > 0, half, 0) + jnp.where(n < exact, n, large)


def _dsa_kernel(relb_ref, qb_ref, qi_ref, sm_ref, gb_ref, ya_ref, kb_ref, vb_ref, kit_ref,
                wbr_ref, out_ref, sc_ref, mb_ref, mbf_ref, vaug_ref, biasw_ref, qp_ref,
                qst_ref, s_ref, acc_ref, o_ref, *, k_top, seq):
    b = pl.program_id(0)
    j = pl.program_id(1)
    tq = Q_TILE
    t0 = j * tq
    n_pairs = ATT_HEADS // 2
    wwin = 2 * tq

    @pl.when((b == 0) & (j == 0))
    def _():
        rq = lax.broadcasted_iota(I32, (tq, wwin), 0)
        rc = lax.broadcasted_iota(I32, (tq, wwin), 1)
        bucket = _t5_bucket_int(rc - tq - rq)
        far = REL_BUCKETS // 2 - 1
        for h in range(ATT_HEADS):
            a = jnp.zeros((tq, wwin), F32)
            for bk in range(REL_BUCKETS):
                a = jnp.where(bucket == bk, relb_ref[bk * ATT_HEADS + h], a)
            a = a - relb_ref[far * ATT_HEADS + h]
            rows = slice((h % 2) * tq, (h % 2 + 1) * tq)
            biasw_ref[0, h // 2, rows, :] = a
            biasw_ref[1, h // 2, rows, :] = jnp.concatenate(
                [a[:, tq:], jnp.zeros((tq, tq), F32)], axis=1)

    @pl.when(j == 0)
    def _():
        def body(i, _):
            rows = pl.ds(pl.multiple_of(i * K_TILE, K_TILE), K_TILE)
            for p in range(n_pairs):
                vaug_ref[rows, 2 * p * LANES:(2 * p + 1) * LANES] = vb_ref[0, rows, p * LANES:(p + 1) * LANES]
                vaug_ref[rows, (2 * p + 1) * LANES:(2 * p + 2) * LANES] = jnp.ones((K_TILE, LANES), BF16)
            return 0
        lax.fori_loop(0, seq // K_TILE, body, 0)

    lane_q = lax.broadcasted_iota(I32, (tq, LANES), 1)
    for p in range(n_pairs):
        qpair = qb_ref[:, p * LANES:(p + 1) * LANES]
        zero = jnp.zeros_like(qpair)
        qp_ref[p, 0:tq, :] = jnp.where(lane_q < ATT_DH, qpair, zero)
        qp_ref[p, tq:2 * tq, :] = jnp.where(lane_q >= ATT_DH, qpair, zero)
    for h in range(IDX_HEADS):
        qst_ref[h * tq:(h + 1) * tq, :] = qi_ref[:, h * IDX_DH:(h + 1) * IDX_DH]
    wi = sm_ref[:, SM_WI:SM_WI + IDX_HEADS] * (IDX_HEADS ** -0.5 * IDX_DH ** -0.5)
    wb = [jnp.broadcast_to(wi[:, h:h + 1], (tq, S_TILE)) for h in range(IDX_HEADS)]

    n_sel = (j + 2) // 2
    row_s = lax.broadcasted_iota(I32, (tq, S_TILE), 0)
    lane_s = lax.broadcasted_iota(I32, (tq, S_TILE), 1)
    limit = t0 + ((row_s >> 6) + 1) * CHUNK

    def score_body(kt, _):
        ks = pl.multiple_of(kt * S_TILE, S_TILE)
        s_all = _dot(qst_ref[...], kit_ref[0, :, pl.ds(ks, S_TILE)])
        score = jnp.zeros((tq, S_TILE), F32)
        for h in range(IDX_HEADS):
            score = score + jnp.maximum(s_all[h * tq:(h + 1) * tq], 0.0) * wb[h]
        sc_ref[:, pl.ds(ks, S_TILE)] = jnp.where(ks + lane_s < limit, score, -jnp.inf)
        return 0
    lax.fori_loop(0, n_sel, score_body, 0)

    kf = float(k_top)

    def count(cand, strict):
        cb = jnp.broadcast_to(cand, (tq, LANES))

        def body(kt, c):
            for half in range(S_TILE // LANES):
                ks = pl.multiple_of(kt * S_TILE + half * LANES, LANES)
                sc = sc_ref[:, pl.ds(ks, LANES)]
                c = c + jnp.where((sc > cb) if strict else (sc >= cb), 1.0, 0.0)
            return c
        c = lax.fori_loop(0, n_sel, body, jnp.zeros((tq, LANES), F32))
        return jnp.sum(c, axis=-1, keepdims=True)

    def minmax_body(kt, carry):
        lo, hi = carry
        for half in range(S_TILE // LANES):
            ks = pl.multiple_of(kt * S_TILE + half * LANES, LANES)
            sc = sc_ref[:, pl.ds(ks, LANES)]
            hi = jnp.maximum(hi, sc)
            lo = jnp.minimum(lo, jnp.where(sc > -jnp.inf, sc, jnp.inf))
        return lo, hi
    lo_l, hi_l = lax.fori_loop(0, n_sel, minmax_body,
                               (jnp.full((tq, LANES), jnp.inf, F32),
                                jnp.full((tq, LANES), -jnp.inf, F32)))
    row_min = jnp.min(lo_l, axis=-1, keepdims=True)
    row_max = jnp.max(hi_l, axis=-1, keepdims=True)
    n_adm = count(jnp.full((tq, 1), -jnp.inf, F32), True)
    has_thr = n_adm >= kf
    c_max = count(row_max, False)
    at_max = c_max >= kf
    lo0 = jnp.where(at_max, row_max, row_min)
    cnt0 = jnp.where(at_max, c_max, n_adm)
    done0 = jnp.logical_not(has_thr) | at_max | (cnt0 == kf)

    def as_flag(done):
        return jnp.where(done, 1.0, 0.0)

    def not_done(done):
        return jnp.min(as_flag(done))

    def bisect_cond(carry):
        return carry[4] < 0.5

    def bisect_body(carry):
        lo, hi, cnt_lo, done_f, _ = carry
        done = done_f > 0.5
        mid = 0.5 * lo + 0.5 * hi
        closed = (mid <= lo) | (mid >= hi)
        c = count(mid, False)
        ge = (c >= kf) & jnp.logical_not(done)
        lt = (c < kf) & jnp.logical_not(done)
        lo = jnp.where(ge, mid, lo)
        cnt_lo = jnp.where(ge, c, cnt_lo)
        hi = jnp.where(lt, mid, hi)
        done = done | closed | (cnt_lo == kf)
        return lo, hi, cnt_lo, as_flag(done), not_done(done)
    thr, _, cnt_thr, _, _ = lax.while_loop(
        bisect_cond, bisect_body, (lo0, row_max, cnt0, as_flag(done0), not_done(done0)))

    tie = has_thr & (cnt_thr > kf)
    any_tie = jnp.max(jnp.where(tie, 1.0, 0.0)) > 0.0
    far_end = t0 - tq
    f32_min = float(jnp.finfo(F32).min)

    def write_mask(ks, sel):
        pos = ks + lax.broadcasted_iota(I32, (tq, LANES), 1)
        mb_ref[:, pl.ds(ks, LANES)] = jnp.where(sel, 0.0, NEG)
        mbf_ref[:, pl.ds(ks, LANES)] = jnp.where(sel & (pos < far_end), 0.0, NEG)

    @pl.when(jnp.logical_not(any_tie))
    def _():
        lo = jnp.broadcast_to(jnp.where(has_thr, thr, f32_min), (tq, LANES))

        def body(kt, _):
            ks = pl.multiple_of(kt * LANES, LANES)
            write_mask(ks, sc_ref[:, pl.ds(ks, LANES)] >= lo)
            return 0
        lax.fori_loop(0, n_sel * (S_TILE // LANES), body, 0)

    @pl.when(any_tie)
    def _():
        need = kf - count(thr, True)
        thr_b = jnp.broadcast_to(thr, (tq, LANES))
        lo = jnp.broadcast_to(jnp.where(has_thr, thr, -jnp.inf), (tq, LANES))
        eq_ok = jnp.broadcast_to(has_thr, (tq, LANES))
        ri = lax.broadcasted_iota(I32, (LANES, LANES), 0)
        ci = lax.broadcasted_iota(I32, (LANES, LANES), 1)
        tri = jnp.where(ri <= ci, 1.0, 0.0).astype(BF16)

        def body(kt, seen):
            ks = pl.multiple_of(kt * LANES, LANES)
            sc = sc_ref[:, pl.ds(ks, LANES)]
            eq = (sc == thr_b) & eq_ok
            prefix = seen + _dot(jnp.where(eq, 1.0, 0.0).astype(BF16), tri)
            write_mask(ks, (sc > lo) | (eq & (prefix <= need)))
            return prefix[:, LANES - 1:LANES]
        lax.fori_loop(0, n_sel * (S_TILE // LANES), body, jnp.zeros((tq, 1), F32))

    n_far = (j + 2) // 4
    ws = pl.multiple_of(jnp.maximum(t0 - tq, 0), tq)
    first = (j == 0).astype(I32)
    lane_o = lax.broadcasted_iota(I32, (tq, LANES), 1)

    def lane_group_max(s):
        m = s[:, :LANES]
        for g in range(1, s.shape[1] // LANES):
            m = jnp.maximum(m, s[:, g * LANES:(g + 1) * LANES])
        return m

    for p in range(n_pairs):
        pl_lo, pl_hi = p * LANES, (p + 1) * LANES
        q_pair = qp_ref[p]

        def logits_body(kt, m_acc, pl_lo=pl_lo, pl_hi=pl_hi, q_pair=q_pair):
            ks = pl.multiple_of(kt * K_TILE, K_TILE)
            mbt = mbf_ref[:, pl.ds(ks, K_TILE)]
            s = _dot_nt(q_pair, kb_ref[0, pl.ds(ks, K_TILE), pl_lo:pl_hi]) + jnp.concatenate([mbt, mbt], axis=0)
            s_ref[:, pl.ds(ks, K_TILE)] = s
            return jnp.maximum(m_acc, lane_group_max(s))
        m_acc = lax.fori_loop(0, n_far, logits_body, jnp.full((2 * tq, LANES), NEG, F32))

        mbw = mb_ref[:, pl.ds(ws, wwin)]
        sw = (_dot_nt(q_pair, kb_ref[0, pl.ds(ws, wwin), pl_lo:pl_hi]) + jnp.concatenate([mbw, mbw], axis=0)
              + biasw_ref[first, p])
        m_acc = jnp.maximum(m_acc, lane_group_max(sw))
        m_row = jnp.broadcast_to(jnp.max(m_acc, axis=-1, keepdims=True), (2 * tq, LANES))

        pw = jnp.exp(sw - jnp.concatenate([m_row] * (wwin // LANES), axis=1)).astype(BF16)
        acc_ref[...] = _dot(pw, vaug_ref[pl.ds(ws, wwin), 2 * pl_lo:2 * pl_hi])
        m_far = jnp.concatenate([m_row] * (K_TILE // LANES), axis=1)

        def pv_body(kt, _, pl_lo=pl_lo, pl_hi=pl_hi, m_far=m_far):
            ks = pl.multiple_of(kt * K_TILE, K_TILE)
            pexp = jnp.exp(s_ref[:, pl.ds(ks, K_TILE)] - m_far).astype(BF16)
            acc_ref[...] += _dot(pexp, vaug_ref[pl.ds(ks, K_TILE), 2 * pl_lo:2 * pl_hi])
            return 0
        lax.fori_loop(0, n_far, pv_body, 0)

        a = acc_ref[...]
        o_even = a[:tq, :LANES] / a[:tq, LANES:LANES + 1]
        o_odd = a[tq:, :LANES] / a[tq:, LANES:LANES + 1]
        o_ref[:, pl_lo:pl_hi] = jnp.where(lane_o < ATT_DH, o_even, o_odd).astype(BF16)

    yb = _dot(o_ref[...], wbr_ref[...])
    mixed = jax.nn.sigmoid(gb_ref[...].astype(F32)) * yb + ya_ref[...].astype(F32)
    out_ref[...] = mixed.astype(BF16)


def _dsa_mixer(p, ya, rel_bias, w_br_att, batch, seq):
    n = batch * seq
    d = w_br_att.shape[1]
    tq = Q_TILE
    nq = seq // tq
    k_top = min(TOPK_MAX, seq // 4)
    kb3 = p["kb"].reshape(batch, seq, ATT_W)
    vb3 = p["vb"].reshape(batch, seq, ATT_W)
    kit = p["sm"][:, SM_KI:SM_KI + IDX_DH].astype(BF16).reshape(batch, seq, IDX_DH).swapaxes(1, 2)
    row = lambda b, j: (b * nq + j, 0)
    per_b = lambda b, j: (b, 0, 0)
    const = lambda b, j: (0, 0)
    return pl.pallas_call(
        functools.partial(_dsa_kernel, k_top=k_top, seq=seq),
        grid=(batch, nq),
        in_specs=[pl.BlockSpec(memory_space=pltpu.SMEM),
                  pl.BlockSpec((tq, ATT_W), row), pl.BlockSpec((tq, IDX_Q_W), row),
                  pl.BlockSpec((tq, LANES), row), pl.BlockSpec((tq, d), row),
                  pl.BlockSpec((tq, d), row),
                  pl.BlockSpec((1, seq, ATT_W), per_b), pl.BlockSpec((1, seq, ATT_W), per_b),
                  pl.BlockSpec((1, IDX_DH, seq), per_b),
                  pl.BlockSpec((ATT_W, d), const)],
        out_specs=pl.BlockSpec((tq, d), row),
        out_shape=jax.ShapeDtypeStruct((n, d), BF16),
        scratch_shapes=[pltpu.VMEM((tq, seq), F32),
                        pltpu.VMEM((tq, seq), F32),
                        pltpu.VMEM((tq, seq), F32),
                        pltpu.VMEM((seq, 2 * ATT_W), BF16),
                        pltpu.VMEM((2, ATT_HEADS // 2, 2 * tq, 2 * tq), F32),
                        pltpu.VMEM((ATT_HEADS // 2, 2 * tq, LANES), BF16),
                        pltpu.VMEM((IDX_HEADS * tq, IDX_DH), BF16),
                        pltpu.VMEM((2 * tq, seq), F32),
                        pltpu.VMEM((2 * tq, 2 * LANES), F32),
                        pltpu.VMEM((tq, ATT_W), BF16)],
        compiler_params=pltpu.CompilerParams(dimension_semantics=("arbitrary", "arbitrary"),
                                             vmem_limit_bytes=VMEM_LIMIT),
        name="dsa_mixer",
    )(rel_bias.astype(F32).reshape(-1), p["qb"], p["qi"], p["sm"], p["gb"], ya, kb3, vb3, kit,
      w_br_att.astype(BF16))


RT_G0 = 0
RT_E0 = N_GROUPS


def _moe_kernel(x_ref, mx_ref, wo_ref, gf_ref, wrh_ref, wrl_ref, br_ref, ex_ref, wg_ref, wu_ref,
                wd_ref, out_ref, x1_ref, h2_ref, comb_ref, y_ref):
    g = pl.program_id(1)
    tm = x_ref.shape[0]

    @pl.when(g == 0)
    def _():
        x1 = x_ref[...] + _dot(mx_ref[...], wo_ref[...])
        x1_ref[...] = x1
        ms = jnp.mean(x1 * x1, axis=-1, keepdims=True)
        h = x1 * lax.rsqrt(ms + EPS) * gf_ref[...]
        h_hi, h_lo = _split_bf16(h)
        h2_ref[...] = h_hi
        logits = (_dot(h_hi, wrh_ref[...]) + _dot(h_lo, wrh_ref[...]) + _dot(h_hi, wrl_ref[...])
                  + br_ref[...])
        lane = lax.broadcasted_iota(I32, (tm, LANES), 1)
        big = jnp.int32(LANES)
        is_g = lane < N_GROUPS
        gl = jnp.where(is_g, logits, -jnp.inf)
        gmax = jnp.max(gl, axis=-1, keepdims=True)
        gidx = jnp.min(jnp.where(gl == gmax, lane, big), axis=-1, keepdims=True)
        gsum = jnp.sum(jnp.where(is_g, jnp.exp(gl - gmax), 0.0), axis=-1, keepdims=True)
        g_w = 1.0 / gsum
        in_grp = (lane >= RT_E0) & (lane < RT_E0 + N_EXPERTS) & (((lane - RT_E0) >> 3) == gidx)
        ev = jnp.where(in_grp, logits, -jnp.inf)
        t1 = jnp.max(ev, axis=-1, keepdims=True)
        i1 = jnp.min(jnp.where(ev == t1, lane, big), axis=-1, keepdims=True)
        ev2 = jnp.where(lane == i1, -jnp.inf, ev)
        t2 = jnp.max(ev2, axis=-1, keepdims=True)
        i2 = jnp.min(jnp.where(ev2 == t2, lane, big), axis=-1, keepdims=True)
        e2 = jnp.exp(t2 - t1)
        w1 = 1.0 / (1.0 + e2)
        w2 = e2 * w1
        comb_ref[...] = g_w * (jnp.where(lane == i1, w1, 0.0) + jnp.where(lane == i2, w2, 0.0))
        y_ref[...] = jnp.zeros_like(y_ref)

    c_hi, c_lo = _split_bf16(comb_ref[...])
    cexp = _dot(c_hi, ex_ref[0]) + _dot(c_lo, ex_ref[0])
    h2 = h2_ref[...]
    hg = _dot(h2, wg_ref[0])
    hu = _dot(h2, wu_ref[0])
    hid = (hg * jax.nn.sigmoid(hg) * hu * cexp).astype(BF16)
    y_ref[...] += _dot(hid, wd_ref[0])

    @pl.when(g == N_GROUPS - 1)
    def _():
        out_ref[...] = x1_ref[...] + y_ref[...]


def _out_proj_moe(x2, mixed, w_out, g_ffn, w_rg, b_rg, w_re, b_re, w_gate, w_up, w_down):
    n, d = x2.shape
    tm = ROW_TILE
    gw = EXPERTS_PER_GROUP * D_EXPERT
    w_r = jnp.zeros((d, LANES), F32).at[:, RT_G0:RT_G0 + N_GROUPS].set(w_rg)
    w_r = w_r.at[:, RT_E0:RT_E0 + N_EXPERTS].set(w_re)
    wr_hi = w_r.astype(BF16)
    wr_lo = (w_r - wr_hi.astype(F32)).astype(BF16)
    b_r = jnp.zeros((1, LANES), F32).at[0, RT_G0:RT_G0 + N_GROUPS].set(b_rg)
    b_r = b_r.at[0, RT_E0:RT_E0 + N_EXPERTS].set(b_re)
    lane = jnp.arange(LANES)[None, :, None]
    grp = jnp.arange(N_GROUPS)[:, None, None]
    col = jnp.arange(gw)[None, None, :]
    expand = (lane == RT_E0 + grp * EXPERTS_PER_GROUP + col // D_EXPERT).astype(BF16)

    def by_group(w):
        return (w.reshape(N_GROUPS, EXPERTS_PER_GROUP, d, D_EXPERT).transpose(0, 2, 1, 3)
                .reshape(N_GROUPS, d, gw).astype(BF16))
    wg = by_group(w_gate)
    wu = by_group(w_up)
    wd = w_down.reshape(N_GROUPS, gw, d).astype(BF16)

    row = lambda i, g: (i, 0)
    const = lambda i, g: (0, 0)
    grp3 = lambda i, g: (g, 0, 0)
    return pl.pallas_call(
        _moe_kernel,
        grid=(n // tm, N_GROUPS),
        in_specs=[pl.BlockSpec((tm, d), row), pl.BlockSpec((tm, d), row),
                  pl.BlockSpec((d, d), const), pl.BlockSpec((1, d), const),
                  pl.BlockSpec((d, LANES), const), pl.BlockSpec((d, LANES), const),
                  pl.BlockSpec((1, LANES), const),
                  pl.BlockSpec((1, LANES, gw), grp3),
                  pl.BlockSpec((1, d, gw), grp3), pl.BlockSpec((1, d, gw), grp3),
                  pl.BlockSpec((1, gw, d), grp3)],
        out_specs=pl.BlockSpec((tm, d), row),
        out_shape=jax.ShapeDtypeStruct((n, d), F32),
        scratch_shapes=[pltpu.VMEM((tm, d), F32), pltpu.VMEM((tm, d), BF16),
                        pltpu.VMEM((tm, LANES), F32), pltpu.VMEM((tm, d), F32)],
        compiler_params=pltpu.CompilerParams(dimension_semantics=("arbitrary", "arbitrary"),
                                             vmem_limit_bytes=VMEM_LIMIT),
        name="out_proj_moe",
    )(x2, mixed, w_out.astype(BF16), g_ffn[None, :].astype(F32), wr_hi, wr_lo, b_r, expand,
      wg, wu, wd)


def kernel(x, g_mix, w_in, w_alpha2, b_alpha, g_gla, w_br_gla, g_q, g_k, rel_bias, w_br_att, w_out, g_ffn, w_rg, b_rg, w_re, b_re, w_gate, w_up, w_down):
    batch, seq, d = x.shape
    assert seq % ROW_TILE == 0 and seq % K_TILE == 0 and (batch * seq) % ROW_TILE == 0
    x2 = x.reshape(batch * seq, d)
    for l in range(g_mix.shape[0]):
        p = _in_projection(x2, g_mix[l], w_in[l], g_q[l], g_k[l])
        ya = _gla_mixer(p, w_alpha2[l], b_alpha[l], g_gla[l], w_br_gla[l], batch, seq)
        mixed = _dsa_mixer(p, ya, rel_bias, w_br_att[l], batch, seq)
        x2 = _out_proj_moe(x2, mixed, w_out[l], g_ffn[l], w_rg[l], b_rg[l], w_re[l], b_re[l],
                           w_gate[l], w_up[l], w_down[l])
    return x2.reshape(batch, seq, d)
```

```python
import functools
import math

import jax
import jax.numpy as jnp
from jax import lax
from jax.experimental import pallas as pl
from jax.experimental.pallas import tpu as pltpu

F32 = jnp.float32
BF16 = jnp.bfloat16
I32 = jnp.int32

CHUNK = 64
GLA_HEADS = 4
GLA_DK = 64
GLA_DV = 128
GLA_GATE_RANK = 16
GLA_TAU = 16.0
ATT_HEADS = 8
ATT_DH = 64
IDX_HEADS = 8
IDX_DH = 32
TOPK_MAX = 256
REL_BUCKETS = 32
REL_MAX_DIST = 128
N_GROUPS = 4
EXPERTS_PER_GROUP = 8
N_EXPERTS = N_GROUPS * EXPERTS_PER_GROUP
D_EXPERT = 128
EPS = 1e-6

GLA_QK_W = GLA_HEADS * GLA_DK
GLA_V_W = GLA_HEADS * GLA_DV
ATT_W = ATT_HEADS * ATT_DH
IDX_Q_W = IDX_HEADS * IDX_DH

LANES = 128
VMEM_LIMIT = 56 * 1024 * 1024

SM_ALR = 0
SM_KI = GLA_GATE_RANK
SM_WI = GLA_GATE_RANK + IDX_DH

NEG = -1e30

ROW_TILE = 512
Q_TILE = 128
K_TILE = 512
S_TILE = 256
C_TILE = 2 * S_TILE


def _dot(a, b):
    return jnp.dot(a, b, preferred_element_type=F32)


def _dot_nt(a, b):
    return lax.dot_general(a, b, (((1,), (1,)), ((), ())), preferred_element_type=F32)


def _dot_tn(a, b):
    return lax.dot_general(a, b, (((0,), (0,)), ((), ())), preferred_element_type=F32)


def _split_bf16(a):
    hi = a.astype(BF16)
    lo = (a - hi.astype(F32)).astype(BF16)
    return hi, lo


_PROJ = (("qa", GLA_QK_W), ("ka", GLA_QK_W), ("va", GLA_V_W), ("ra", GLA_V_W),
         ("qb", ATT_W), ("kb", ATT_W), ("vb", ATT_W), ("qi", IDX_Q_W),
         ("ga", None), ("gb", None), ("sm", LANES))


def _inproj_kernel(x_ref, g_ref, w_ref, gq_ref, gk_ref, bd_ref, *out_refs, offs):
    x = x_ref[...]
    ms = jnp.mean(x * x, axis=-1, keepdims=True)
    hn = (x * lax.rsqrt(ms + EPS) * g_ref[...]).astype(BF16)

    def proj(name):
        c0, c1 = offs[name]
        return _dot(hn, w_ref[:, c0:c1])

    def head_norm(y, gain):
        ss = _dot((y * y).astype(BF16), bd_ref[...])
        return y * lax.rsqrt(ss * (1.0 / ATT_DH) + EPS) * gain

    qa_ref, ka_ref, va_ref, ra_ref, qb_ref, kb_ref, vb_ref, qi_ref, ga_ref, gb_ref, sm_ref = out_refs
    qa_ref[...] = (proj("qa") * (GLA_DK ** -0.5)).astype(BF16)
    ka_ref[...] = proj("ka").astype(BF16)
    va_ref[...] = proj("va").astype(BF16)
    ra_ref[...] = proj("ra").astype(BF16)
    qb_ref[...] = (head_norm(proj("qb"), gq_ref[...]) * (ATT_DH ** -0.5)).astype(BF16)
    kb_ref[...] = head_norm(proj("kb"), gk_ref[...]).astype(BF16)
    vb_ref[...] = proj("vb").astype(BF16)
    qi_ref[...] = proj("qi").astype(BF16)
    ga_ref[...] = proj("ga").astype(BF16)
    gb_ref[...] = proj("gb").astype(BF16)
    sm_ref[...] = proj("sm")


def _in_projection(x2, g_mix, w_in, g_q, g_k):
    n, d = x2.shape
    widths = {name: (d if w is None else w) for name, w in _PROJ}
    ref_order = (("qa", GLA_QK_W), ("ka", GLA_QK_W), ("va", GLA_V_W), ("ra", GLA_V_W),
                 ("alr", GLA_GATE_RANK), ("qb", ATT_W), ("kb", ATT_W), ("vb", ATT_W),
                 ("qi", IDX_Q_W), ("ki", IDX_DH), ("wi", IDX_HEADS), ("ga", d), ("gb", d))
    cols, c = {}, 0
    for name, w in ref_order:
        cols[name] = w_in[:, c:c + w]
        c += w
    small = jnp.concatenate(
        [cols["alr"], cols["ki"], cols["wi"],
         jnp.zeros((d, LANES - GLA_GATE_RANK - IDX_DH - IDX_HEADS), w_in.dtype)], axis=1)
    cols["sm"] = small
    w_cat = jnp.concatenate([cols[name] for name, _ in _PROJ], axis=1).astype(BF16)
    offs, c = {}, 0
    for name, _ in _PROJ:
        offs[name] = (c, c + widths[name])
        c += widths[name]
    d_cat = c

    head = jnp.arange(ATT_W) // ATT_DH
    blockdiag = (head[:, None] == head[None, :]).astype(BF16)
    gq = jnp.tile(g_q, ATT_HEADS)[None, :].astype(F32)
    gk = jnp.tile(g_k, ATT_HEADS)[None, :].astype(F32)

    tm = ROW_TILE
    const = lambda i: (0, 0)
    out_shape = [jax.ShapeDtypeStruct((n, widths[name]), F32 if name == "sm" else BF16)
                 for name, _ in _PROJ]
    out_specs = [pl.BlockSpec((tm, widths[name]), lambda i: (i, 0)) for name, _ in _PROJ]
    outs = pl.pallas_call(
        functools.partial(_inproj_kernel, offs=offs),
        grid=(n // tm,),
        in_specs=[pl.BlockSpec((tm, d), lambda i: (i, 0)),
                  pl.BlockSpec((1, d), const),
                  pl.BlockSpec((d, d_cat), const),
                  pl.BlockSpec((1, ATT_W), const),
                  pl.BlockSpec((1, ATT_W), const),
                  pl.BlockSpec((ATT_W, ATT_W), const)],
        out_specs=out_specs,
        out_shape=out_shape,
        compiler_params=pltpu.CompilerParams(dimension_semantics=("arbitrary",),
                                             vmem_limit_bytes=VMEM_LIMIT),
        name="in_projection",
    )(x2, g_mix[None, :].astype(F32), w_cat, gq, gk, blockdiag)
    return dict(zip([name for name, _ in _PROJ], outs))


def _gla_kernel(qa_ref, ka_ref, va_ref, ra_ref, sm_ref, ga_ref, wa2_ref, ba_ref, ltri_ref,
                lall_ref, gg_ref, wbr_ref, out_ref, st_ref, o_ref):
    @pl.when(pl.program_id(1) == 0)
    def _():
        st_ref[...] = jnp.zeros_like(st_ref)

    tb = qa_ref.shape[0]
    z = _dot(sm_ref[...].astype(BF16), wa2_ref[...]) + ba_ref[...]
    log_a = (jnp.minimum(z, 0.0) - jnp.log1p(jnp.exp(-jnp.abs(z)))) * (1.0 / GLA_TAU)
    la_hi, la_lo = _split_bf16(log_a)
    ltri = ltri_ref[...]
    lall = lall_ref[...]
    cum = _dot(ltri, la_hi) + _dot(ltri, la_lo)
    tot = _dot(lall, la_hi) + _dot(lall, la_lo)
    k_dec = (ka_ref[...].astype(F32) * jnp.exp(tot - cum)).astype(BF16)
    dec = jnp.exp(tot)

    states = [st_ref[h] for h in range(GLA_HEADS)]
    for c in range(tb // CHUNK):
        rows = slice(c * CHUNK, (c + 1) * CHUNK)
        for h in range(GLA_HEADS):
            kl = slice(h * GLA_DK, (h + 1) * GLA_DK)
            vl = slice(h * GLA_DV, (h + 1) * GLA_DV)
            u_t = _dot_tn(va_ref[rows, vl], k_dec[rows, kl])
            s = states[h] * dec[c * CHUNK:c * CHUNK + 1, kl] + u_t
            states[h] = s
            o_ref[rows, vl] = _dot_nt(qa_ref[rows, kl], s.astype(BF16))
    for h in range(GLA_HEADS):
        st_ref[h] = states[h]

    r = ra_ref[...].astype(F32)
    gated = []
    for h in range(GLA_HEADS):
        vl = slice(h * GLA_DV, (h + 1) * GLA_DV)
        oh = o_ref[:, vl]
        ms = jnp.mean(oh * oh, axis=-1, keepdims=True)
        oh = oh * lax.rsqrt(ms + EPS) * gg_ref[...]
        rh = r[:, vl]
        gated.append((oh * (rh * jax.nn.sigmoid(rh))).astype(BF16))
    og = jnp.concatenate(gated, axis=1)
    ya = _dot(og, wbr_ref[...])
    out_ref[...] = (jax.nn.sigmoid(ga_ref[...].astype(F32)) * ya).astype(BF16)


def _gla_mixer(p, w_alpha2, b_alpha, g_gla, w_br_gla, batch, seq):
    n = batch * seq
    d = w_br_gla.shape[1]
    tb = ROW_TILE
    nt = seq // tb
    wa2 = jnp.zeros((LANES, GLA_QK_W), F32).at[SM_ALR:SM_ALR + GLA_GATE_RANK].set(w_alpha2).astype(BF16)
    r = jnp.arange(tb)
    same = (r[:, None] // CHUNK) == (r[None, :] // CHUNK)
    ltri = (same & (r[None, :] <= r[:, None])).astype(BF16)
    lall = same.astype(BF16)
    row = lambda b, i: (b * nt + i, 0)
    const = lambda b, i: (0, 0)
    return pl.pallas_call(
        _gla_kernel,
        grid=(batch, nt),
        in_specs=[pl.BlockSpec((tb, GLA_QK_W), row), pl.BlockSpec((tb, GLA_QK_W), row),
                  pl.BlockSpec((tb, GLA_V_W), row), pl.BlockSpec((tb, GLA_V_W), row),
                  pl.BlockSpec((tb, LANES), row), pl.BlockSpec((tb, d), row),
                  pl.BlockSpec((LANES, GLA_QK_W), const), pl.BlockSpec((1, GLA_QK_W), const),
                  pl.BlockSpec((tb, tb), const), pl.BlockSpec((tb, tb), const),
                  pl.BlockSpec((1, GLA_DV), const), pl.BlockSpec((GLA_V_W, d), const)],
        out_specs=pl.BlockSpec((tb, d), row),
        out_shape=jax.ShapeDtypeStruct((n, d), BF16),
        scratch_shapes=[pltpu.VMEM((GLA_HEADS, GLA_DV, GLA_DK), F32),
                        pltpu.VMEM((tb, GLA_V_W), F32)],
        compiler_params=pltpu.CompilerParams(dimension_semantics=("arbitrary", "arbitrary"),
                                             vmem_limit_bytes=VMEM_LIMIT),
        name="gla_mixer",
    )(p["qa"], p["ka"], p["va"], p["ra"], p["sm"], p["ga"], wa2, b_alpha[None, :].astype(F32),
      ltri, lall, g_gla[None, :].astype(F32), w_br_gla.astype(BF16))


def _t5_bucket_int(rel):
    half = REL_BUCKETS // 2
    exact = half // 2
    n = jnp.abs(rel)
    large = jnp.full(rel.shape, exact, I32)
    for j in range(1, half - exact):
        thr = math.ceil(exact * (REL_MAX_DIST / exact) ** (j / (half - exact)) - 1e-9)
        large = large + jnp.where(n >= thr, 1, 0)
    return jnp.where(rel > 0, half, 0) + jnp.where(n < exact, n, large)


def _dsa_kernel(relb_ref, qb_ref, qi_ref, sm_ref, gb_ref, ya_ref, kb_ref, vb_ref, kit_ref,
                wbr_ref, out_ref, sc_ref, mb_ref, mbf_ref, vaug_ref, biasw_ref, qp_ref,
                qst_ref, s_ref, mrow_ref, acc_ref, o_ref, *, k_top, seq):
    b = pl.program_id(0)
    j = pl.program_id(1)
    tq = Q_TILE
    t0 = j * tq
    n_pairs = ATT_HEADS // 2
    wwin = 2 * tq

    @pl.when((b == 0) & (j == 0))
    def _():
        rq = lax.broadcasted_iota(I32, (tq, wwin), 0)
        rc = lax.broadcasted_iota(I32, (tq, wwin), 1)
        bucket = _t5_bucket_int(rc - tq - rq)
        far = REL_BUCKETS // 2 - 1
        for h in range(ATT_HEADS):
            a = jnp.zeros((tq, wwin), F32)
            for bk in range(REL_BUCKETS):
                a = jnp.where(bucket == bk, relb_ref[bk * ATT_HEADS + h], a)
            a = a - relb_ref[far * ATT_HEADS + h]
            rows = slice((h % 2) * tq, (h % 2 + 1) * tq)
            biasw_ref[0, h // 2, rows, :] = a
            biasw_ref[1, h // 2, rows, :] = jnp.concatenate(
                [a[:, tq:], jnp.zeros((tq, tq), F32)], axis=1)

    @pl.when(j == 0)
    def _():
        def body(i, _):
            rows = pl.ds(pl.multiple_of(i * K_TILE, K_TILE), K_TILE)
            for p in range(n_pairs):
                vaug_ref[rows, 2 * p * LANES:(2 * p + 1) * LANES] = vb_ref[0, rows, p * LANES:(p + 1) * LANES]
                vaug_ref[rows, (2 * p + 1) * LANES:(2 * p + 2) * LANES] = jnp.ones((K_TILE, LANES), BF16)
            return 0
        lax.fori_loop(0, seq // K_TILE, body, 0)

    lane_q = lax.broadcasted_iota(I32, (tq, LANES), 1)
    for p in range(n_pairs):
        qpair = qb_ref[:, p * LANES:(p + 1) * LANES]
        zero = jnp.zeros_like(qpair)
        qp_ref[p, 0:tq, :] = jnp.where(lane_q < ATT_DH, qpair, zero)
        qp_ref[p, tq:2 * tq, :] = jnp.where(lane_q >= ATT_DH, qpair, zero)
    for h in range(IDX_HEADS):
        qst_ref[h * tq:(h + 1) * tq, :] = qi_ref[:, h * IDX_DH:(h + 1) * IDX_DH]
    wi = sm_ref[:, SM_WI:SM_WI + IDX_HEADS] * (IDX_HEADS ** -0.5 * IDX_DH ** -0.5)
    wb = [jnp.broadcast_to(wi[:, h:h + 1], (tq, S_TILE)) for h in range(IDX_HEADS)]

    n_sel = (j + 2) // 2
    row_s = lax.broadcasted_iota(I32, (tq, S_TILE), 0)
    lane_s = lax.broadcasted_iota(I32, (tq, S_TILE), 1)
    limit = t0 + ((row_s >> 6) + 1) * CHUNK

    def score_body(kt, _):
        ks = pl.multiple_of(kt * S_TILE, S_TILE)
        s_all = _dot(qst_ref[...], kit_ref[0, :, pl.ds(ks, S_TILE)])
        score = jnp.zeros((tq, S_TILE), F32)
        for h in range(IDX_HEADS):
            score = score + jnp.maximum(s_all[h * tq:(h + 1) * tq], 0.0) * wb[h]
        sc_ref[:, pl.ds(ks, S_TILE)] = jnp.where(ks + lane_s < limit, score, -jnp.inf)
        return 0
    lax.fori_loop(0, n_sel, score_body, 0)

    n_cnt = (n_sel + 1) // 2

    @pl.when(n_sel % 2 == 1)
    def _():
        sc_ref[:, pl.ds(pl.multiple_of(n_sel * S_TILE, S_TILE), S_TILE)] = jnp.full((tq, S_TILE), -jnp.inf, F32)

    kf = float(k_top)
    c_groups = C_TILE // LANES

    def score_group(kt, g):
        return sc_ref[:, pl.ds(pl.multiple_of(kt * C_TILE + g * LANES, LANES), LANES)]

    def lane_sum(c):
        return jnp.sum(c, axis=-1, keepdims=True)

    def count(cand, strict):
        cb = jnp.broadcast_to(cand, (tq, LANES))

        def body(kt, c):
            for g in range(c_groups):
                sc = score_group(kt, g)
                c = c + jnp.where((sc > cb) if strict else (sc >= cb), 1.0, 0.0)
            return c
        return lane_sum(lax.fori_loop(0, n_cnt, body, jnp.zeros((tq, LANES), F32)))

    def stats_body(kt, carry):
        lo, hi, n_fin, n_ge0, n_gt0 = carry
        for g in range(c_groups):
            sc = score_group(kt, g)
            fin = sc > -jnp.inf
            hi = jnp.maximum(hi, sc)
            lo = jnp.minimum(lo, jnp.where(fin, sc, jnp.inf))
            n_fin = n_fin + jnp.where(fin, 1.0, 0.0)
            n_ge0 = n_ge0 + jnp.where(sc >= 0.0, 1.0, 0.0)
            n_gt0 = n_gt0 + jnp.where(sc > 0.0, 1.0, 0.0)
        return lo, hi, n_fin, n_ge0, n_gt0
    zeros_l = jnp.zeros((tq, LANES), F32)
    lo_l, hi_l, fin_l, ge0_l, gt0_l = lax.fori_loop(
        0, n_cnt, stats_body,
        (jnp.full((tq, LANES), jnp.inf, F32), jnp.full((tq, LANES), -jnp.inf, F32),
         zeros_l, zeros_l, zeros_l))
    row_min = jnp.min(lo_l, axis=-1, keepdims=True)
    row_max = jnp.max(hi_l, axis=-1, keepdims=True)
    n_adm = lane_sum(fin_l)
    c_ge0 = lane_sum(ge0_l)
    c_gt0 = lane_sum(gt0_l)
    has_thr = n_adm >= kf
    c_max = count(row_max, False)
    at_max = c_max >= kf
    at_zero = (c_gt0 < kf) & (c_ge0 >= kf)
    above_zero = c_gt0 >= kf
    lo0 = jnp.where(at_max, row_max, jnp.where(at_zero | above_zero, 0.0, row_min))
    cnt0 = jnp.where(at_max, c_max, jnp.where(at_zero | above_zero, c_ge0, n_adm))
    hi0 = jnp.where(at_zero | above_zero, row_max, 0.0)
    done0 = jnp.logical_not(has_thr) | at_max | at_zero | (cnt0 == kf)

    def as_flag(done):
        return jnp.where(done, 1.0, 0.0)

    def all_done(done):
        return jnp.min(as_flag(done))

    def bisect_cond(carry):
        return carry[4] < 0.5

    def bisect_body(carry):
        lo, hi, cnt_lo, done_f, _ = carry
        done = done_f > 0.5
        mid = 0.5 * lo + 0.5 * hi
        closed = (mid <= lo) | (mid >= hi)
        c = count(mid, False)
        ge = (c >= kf) & jnp.logical_not(done)
        lt = (c < kf) & jnp.logical_not(done)
        lo = jnp.where(ge, mid, lo)
        cnt_lo = jnp.where(ge, c, cnt_lo)
        hi = jnp.where(lt, mid, hi)
        done = done | closed | (cnt_lo == kf)
        return lo, hi, cnt_lo, as_flag(done), all_done(done)
    thr, _, cnt_thr, _, _ = lax.while_loop(
        bisect_cond, bisect_body, (lo0, hi0, cnt0, as_flag(done0), all_done(done0)))

    tie = has_thr & (cnt_thr > kf)
    any_tie = jnp.max(jnp.where(tie, 1.0, 0.0)) > 0.0
    far_end = t0 - tq
    f32_min = float(jnp.finfo(F32).min)

    def write_mask(ks, sel):
        pos = ks + lax.broadcasted_iota(I32, (tq, LANES), 1)
        mb_ref[:, pl.ds(ks, LANES)] = jnp.where(sel, 0.0, NEG)
        mbf_ref[:, pl.ds(ks, LANES)] = jnp.where(sel & (pos < far_end), 0.0, NEG)

    @pl.when(jnp.logical_not(any_tie))
    def _():
        lo = jnp.broadcast_to(jnp.where(has_thr, thr, f32_min), (tq, LANES))

        def body(kt, _):
            ks = pl.multiple_of(kt * LANES, LANES)
            write_mask(ks, sc_ref[:, pl.ds(ks, LANES)] >= lo)
            return 0
        lax.fori_loop(0, n_sel * (S_TILE // LANES), body, 0)

    @pl.when(any_tie)
    def _():
        need = kf - count(thr, True)
        thr_b = jnp.broadcast_to(thr, (tq, LANES))
        lo = jnp.broadcast_to(jnp.where(has_thr, thr, -jnp.inf), (tq, LANES))
        eq_ok = jnp.broadcast_to(has_thr, (tq, LANES))
        ri = lax.broadcasted_iota(I32, (LANES, LANES), 0)
        ci = lax.broadcasted_iota(I32, (LANES, LANES), 1)
        tri = jnp.where(ri <= ci, 1.0, 0.0).astype(BF16)

        def body(kt, seen):
            ks = pl.multiple_of(kt * LANES, LANES)
            sc = sc_ref[:, pl.ds(ks, LANES)]
            eq = (sc == thr_b) & eq_ok
            prefix = seen + _dot(jnp.where(eq, 1.0, 0.0).astype(BF16), tri)
            write_mask(ks, (sc > lo) | (eq & (prefix <= need)))
            return prefix[:, LANES - 1:LANES]
        lax.fori_loop(0, n_sel * (S_TILE // LANES), body, jnp.zeros((tq, 1), F32))

    n_far = (j + 2) // 4
    ws = pl.multiple_of(jnp.maximum(t0 - tq, 0), tq)
    first = (j == 0).astype(I32)
    lane_o = lax.broadcasted_iota(I32, (tq, LANES), 1)

    def lane_group_max(s):
        m = s[:, :LANES]
        for g in range(1, s.shape[1] // LANES):
            m = jnp.maximum(m, s[:, g * LANES:(g + 1) * LANES])
        return m

    def pair_lanes(p, width=1):
        return slice(width * p * LANES, width * (p + 1) * LANES)

    mrow_ref[...] = jnp.full(mrow_ref.shape, NEG, F32)

    def logits_body(kt, _):
        ks = pl.multiple_of(kt * K_TILE, K_TILE)
        mbt = mbf_ref[:, pl.ds(ks, K_TILE)]
        mb2 = jnp.concatenate([mbt, mbt], axis=0)
        for p in range(n_pairs):
            s = _dot_nt(qp_ref[p], kb_ref[0, pl.ds(ks, K_TILE), pair_lanes(p)]) + mb2
            s_ref[p, :, pl.ds(ks, K_TILE)] = s
            mrow_ref[p] = jnp.maximum(mrow_ref[p], lane_group_max(s))
        return 0
    lax.fori_loop(0, n_far, logits_body, 0)

    mbw = mb_ref[:, pl.ds(ws, wwin)]
    mbw2 = jnp.concatenate([mbw, mbw], axis=0)
    for p in range(n_pairs):
        sw = _dot_nt(qp_ref[p], kb_ref[0, pl.ds(ws, wwin), pair_lanes(p)]) + mbw2 + biasw_ref[first, p]
        m_acc = jnp.maximum(mrow_ref[p], lane_group_max(sw))
        m_row = jnp.broadcast_to(jnp.max(m_acc, axis=-1, keepdims=True), (2 * tq, LANES))
        mrow_ref[p] = m_row
        pw = jnp.exp(sw - jnp.concatenate([m_row] * (wwin // LANES), axis=1)).astype(BF16)
        acc_ref[p] = _dot(pw, vaug_ref[pl.ds(ws, wwin), pair_lanes(p, 2)])

    def pv_body(kt, _):
        ks = pl.multiple_of(kt * K_TILE, K_TILE)
        for p in range(n_pairs):
            m_far = jnp.concatenate([mrow_ref[p]] * (K_TILE // LANES), axis=1)
            pexp = jnp.exp(s_ref[p, :, pl.ds(ks, K_TILE)] - m_far).astype(BF16)
            acc_ref[p] += _dot(pexp, vaug_ref[pl.ds(ks, K_TILE), pair_lanes(p, 2)])
        return 0
    lax.fori_loop(0, n_far, pv_body, 0)

    for p in range(n_pairs):
        a = acc_ref[p]
        o_even = a[:tq, :LANES] / a[:tq, LANES:LANES + 1]
        o_odd = a[tq:, :LANES] / a[tq:, LANES:LANES + 1]
        o_ref[:, pair_lanes(p)] = jnp.where(lane_o < ATT_DH, o_even, o_odd).astype(BF16)

    yb = _dot(o_ref[...], wbr_ref[...])
    mixed = jax.nn.sigmoid(gb_ref[...].astype(F32)) * yb + ya_ref[...].astype(F32)
    out_ref[...] = mixed.astype(BF16)


def _dsa_mixer(p, ya, rel_bias, w_br_att, batch, seq):
    n = batch * seq
    d = w_br_att.shape[1]
    tq = Q_TILE
    nq = seq // tq
    k_top = min(TOPK_MAX, seq // 4)
    kb3 = p["kb"].reshape(batch, seq, ATT_W)
    vb3 = p["vb"].reshape(batch, seq, ATT_W)
    kit = p["sm"][:, SM_KI:SM_KI + IDX_DH].astype(BF16).reshape(batch, seq, IDX_DH).swapaxes(1, 2)
    row = lambda b, j: (b * nq + j, 0)
    per_b = lambda b, j: (b, 0, 0)
    const = lambda b, j: (0, 0)
    return pl.pallas_call(
        functools.partial(_dsa_kernel, k_top=k_top, seq=seq),
        grid=(batch, nq),
        in_specs=[pl.BlockSpec(memory_space=pltpu.SMEM),
                  pl.BlockSpec((tq, ATT_W), row), pl.BlockSpec((tq, IDX_Q_W), row),
                  pl.BlockSpec((tq, LANES), row), pl.BlockSpec((tq, d), row),
                  pl.BlockSpec((tq, d), row),
                  pl.BlockSpec((1, seq, ATT_W), per_b, pipeline_mode=pl.Buffered(1)),
                  pl.BlockSpec((1, seq, ATT_W), per_b, pipeline_mode=pl.Buffered(1)),
                  pl.BlockSpec((1, IDX_DH, seq), per_b, pipeline_mode=pl.Buffered(1)),
                  pl.BlockSpec((ATT_W, d), const)],
        out_specs=pl.BlockSpec((tq, d), row),
        out_shape=jax.ShapeDtypeStruct((n, d), BF16),
        scratch_shapes=[pltpu.VMEM((tq, seq), F32),
                        pltpu.VMEM((tq, seq), F32),
                        pltpu.VMEM((tq, seq), F32),
                        pltpu.VMEM((seq, 2 * ATT_W), BF16),
                        pltpu.VMEM((2, ATT_HEADS // 2, 2 * tq, 2 * tq), F32),
                        pltpu.VMEM((ATT_HEADS // 2, 2 * tq, LANES), BF16),
                        pltpu.VMEM((IDX_HEADS * tq, IDX_DH), BF16),
                        pltpu.VMEM((ATT_HEADS // 2, 2 * tq, seq), F32),
                        pltpu.VMEM((ATT_HEADS // 2, 2 * tq, LANES), F32),
                        pltpu.VMEM((ATT_HEADS // 2, 2 * tq, 2 * LANES), F32),
                        pltpu.VMEM((tq, ATT_W), BF16)],
        compiler_params=pltpu.CompilerParams(dimension_semantics=("arbitrary", "arbitrary"),
                                             vmem_limit_bytes=VMEM_LIMIT),
        name="dsa_mixer",
    )(rel_bias.astype(F32).reshape(-1), p["qb"], p["qi"], p["sm"], p["gb"], ya, kb3, vb3, kit,
      w_br_att.astype(BF16))


RT_G0 = 0
RT_E0 = N_GROUPS


def _moe_kernel(x_ref, mx_ref, wo_ref, gf_ref, wrh_ref, wrl_ref, br_ref, ex_ref, wg_ref, wu_ref,
                wd_ref, out_ref, x1_ref, h2_ref, comb_ref, y_ref):
    g = pl.program_id(1)
    tm = x_ref.shape[0]

    @pl.when(g == 0)
    def _():
        x1 = x_ref[...] + _dot(mx_ref[...], wo_ref[...])
        x1_ref[...] = x1
        ms = jnp.mean(x1 * x1, axis=-1, keepdims=True)
        h = x1 * lax.rsqrt(ms + EPS) * gf_ref[...]
        h_hi, h_lo = _split_bf16(h)
        h2_ref[...] = h_hi
        logits = (_dot(h_hi, wrh_ref[...]) + _dot(h_lo, wrh_ref[...]) + _dot(h_hi, wrl_ref[...])
                  + br_ref[...])
        lane = lax.broadcasted_iota(I32, (tm, LANES), 1)
        big = jnp.int32(LANES)
        is_g = lane < N_GROUPS
        gl = jnp.where(is_g, logits, -jnp.inf)
        gmax = jnp.max(gl, axis=-1, keepdims=True)
        gidx = jnp.min(jnp.where(gl == gmax, lane, big), axis=-1, keepdims=True)
        gsum = jnp.sum(jnp.where(is_g, jnp.exp(gl - gmax), 0.0), axis=-1, keepdims=True)
        g_w = 1.0 / gsum
        in_grp = (lane >= RT_E0) & (lane < RT_E0 + N_EXPERTS) & (((lane - RT_E0) >> 3) == gidx)
        ev = jnp.where(in_grp, logits, -jnp.inf)
        t1 = jnp.max(ev, axis=-1, keepdims=True)
        i1 = jnp.min(jnp.where(ev == t1, lane, big), axis=-1, keepdims=True)
        ev2 = jnp.where(lane == i1, -jnp.inf, ev)
        t2 = jnp.max(ev2, axis=-1, keepdims=True)
        i2 = jnp.min(jnp.where(ev2 == t2, lane, big), axis=-1, keepdims=True)
        e2 = jnp.exp(t2 - t1)
        w1 = 1.0 / (1.0 + e2)
        w2 = e2 * w1
        comb_ref[...] = g_w * (jnp.where(lane == i1, w1, 0.0) + jnp.where(lane == i2, w2, 0.0))
        y_ref[...] = jnp.zeros_like(y_ref)

    c_hi, c_lo = _split_bf16(comb_ref[...])
    cexp = _dot(c_hi, ex_ref[0]) + _dot(c_lo, ex_ref[0])
    h2 = h2_ref[...]
    hg = _dot(h2, wg_ref[0])
    hu = _dot(h2, wu_ref[0])
    hid = (hg * jax.nn.sigmoid(hg) * hu * cexp).astype(BF16)
    y_ref[...] += _dot(hid, wd_ref[0])

    @pl.when(g == N_GROUPS - 1)
    def _():
        out_ref[...] = x1_ref[...] + y_ref[...]


def _out_proj_moe(x2, mixed, w_out, g_ffn, w_rg, b_rg, w_re, b_re, w_gate, w_up, w_down):
    n, d = x2.shape
    tm = ROW_TILE
    gw = EXPERTS_PER_GROUP * D_EXPERT
    w_r = jnp.zeros((d, LANES), F32).at[:, RT_G0:RT_G0 + N_GROUPS].set(w_rg)
    w_r = w_r.at[:, RT_E0:RT_E0 + N_EXPERTS].set(w_re)
    wr_hi = w_r.astype(BF16)
    wr_lo = (w_r - wr_hi.astype(F32)).astype(BF16)
    b_r = jnp.zeros((1, LANES), F32).at[0, RT_G0:RT_G0 + N_GROUPS].set(b_rg)
    b_r = b_r.at[0, RT_E0:RT_E0 + N_EXPERTS].set(b_re)
    lane = jnp.arange(LANES)[None, :, None]
    grp = jnp.arange(N_GROUPS)[:, None, None]
    col = jnp.arange(gw)[None, None, :]
    expand = (lane == RT_E0 + grp * EXPERTS_PER_GROUP + col // D_EXPERT).astype(BF16)

    def by_group(w):
        return (w.reshape(N_GROUPS, EXPERTS_PER_GROUP, d, D_EXPERT).transpose(0, 2, 1, 3)
                .reshape(N_GROUPS, d, gw).astype(BF16))
    wg = by_group(w_gate)
    wu = by_group(w_up)
    wd = w_down.reshape(N_GROUPS, gw, d).astype(BF16)

    row = lambda i, g: (i, 0)
    const = lambda i, g: (0, 0)
    grp3 = lambda i, g: (g, 0, 0)
    return pl.pallas_call(
        _moe_kernel,
        grid=(n // tm, N_GROUPS),
        in_specs=[pl.BlockSpec((tm, d), row), pl.BlockSpec((tm, d), row),
                  pl.BlockSpec((d, d), const), pl.BlockSpec((1, d), const),
                  pl.BlockSpec((d, LANES), const), pl.BlockSpec((d, LANES), const),
                  pl.BlockSpec((1, LANES), const),
                  pl.BlockSpec((1, LANES, gw), grp3),
                  pl.BlockSpec((1, d, gw), grp3), pl.BlockSpec((1, d, gw), grp3),
                  pl.BlockSpec((1, gw, d), grp3)],
        out_specs=pl.BlockSpec((tm, d), row),
        out_shape=jax.ShapeDtypeStruct((n, d), F32),
        scratch_shapes=[pltpu.VMEM((tm, d), F32), pltpu.VMEM((tm, d), BF16),
                        pltpu.VMEM((tm, LANES), F32), pltpu.VMEM((tm, d), F32)],
        compiler_params=pltpu.CompilerParams(dimension_semantics=("arbitrary", "arbitrary"),
                                             vmem_limit_bytes=VMEM_LIMIT),
        name="out_proj_moe",
    )(x2, mixed, w_out.astype(BF16), g_ffn[None, :].astype(F32), wr_hi, wr_lo, b_r, expand,
      wg, wu, wd)


def kernel(x, g_mix, w_in, w_alpha2, b_alpha, g_gla, w_br_gla, g_q, g_k, rel_bias, w_br_att, w_out, g_ffn, w_rg, b_rg, w_re, b_re, w_gate, w_up, w_down):
    batch, seq, d = x.shape
    assert seq % ROW_TILE == 0 and seq % K_TILE == 0 and (batch * seq) % ROW_TILE == 0
    x2 = x.reshape(batch * seq, d)
    for l in range(g_mix.shape[0]):
        p = _in_projection(x2, g_mix[l], w_in[l], g_q[l], g_k[l])
        ya = _gla_mixer(p, w_alpha2[l], b_alpha[l], g_gla[l], w_br_gla[l], batch, seq)
        mixed = _dsa_mixer(p, ya, rel_bias, w_br_att[l], batch, seq)
        x2 = _out_proj_moe(x2, mixed, w_out[l], g_ffn[l], w_rg[l], b_rg[l], w_re[l], b_re[l],
                           w_gate[l], w_up[l], w_down[l])
    return x2.reshape(batch, seq, d)
```

```python
import functools
import math

import jax
import jax.numpy as jnp
from jax import lax
from jax.experimental import pallas as pl
from jax.experimental.pallas import tpu as pltpu

F32 = jnp.float32
BF16 = jnp.bfloat16
I32 = jnp.int32

CHUNK = 64
GLA_HEADS = 4
GLA_DK = 64
GLA_DV = 128
GLA_GATE_RANK = 16
GLA_TAU = 16.0
ATT_HEADS = 8
ATT_DH = 64
IDX_HEADS = 8
IDX_DH = 32
TOPK_MAX = 256
REL_BUCKETS = 32
REL_MAX_DIST = 128
N_GROUPS = 4
EXPERTS_PER_GROUP = 8
N_EXPERTS = N_GROUPS * EXPERTS_PER_GROUP
D_EXPERT = 128
EPS = 1e-6

GLA_QK_W = GLA_HEADS * GLA_DK
GLA_V_W = GLA_HEADS * GLA_DV
ATT_W = ATT_HEADS * ATT_DH
IDX_Q_W = IDX_HEADS * IDX_DH

LANES = 128
VMEM_LIMIT = 56 * 1024 * 1024

SM_ALR = 0
SM_KI = GLA_GATE_RANK
SM_WI = GLA_GATE_RANK + IDX_DH

NEG = -1e30

ROW_TILE = 512
Q_TILE = 128
K_TILE = 512
MOE_COLS = 256
S_TILE = 256
C_TILE = 2 * S_TILE


def _dot(a, b):
    return jnp.dot(a, b, preferred_element_type=F32)


def _dot_nt(a, b):
    return lax.dot_general(a, b, (((1,), (1,)), ((), ())), preferred_element_type=F32)


def _dot_tn(a, b):
    return lax.dot_general(a, b, (((0,), (0,)), ((), ())), preferred_element_type=F32)


def _split_bf16(a):
    hi = a.astype(BF16)
    lo = (a - hi.astype(F32)).astype(BF16)
    return hi, lo


_PROJ = (("qa", GLA_QK_W), ("ka", GLA_QK_W), ("va", GLA_V_W), ("ra", GLA_V_W),
         ("qb", ATT_W), ("kb", ATT_W), ("vb", ATT_W), ("qi", IDX_Q_W),
         ("ga", None), ("gb", None), ("sm", LANES))


def _inproj_kernel(x_ref, g_ref, w_ref, gq_ref, gk_ref, bd_ref, *out_refs, offs):
    x = x_ref[...]
    ms = jnp.mean(x * x, axis=-1, keepdims=True)
    hn = (x * lax.rsqrt(ms + EPS) * g_ref[...]).astype(BF16)

    def proj(name):
        c0, c1 = offs[name]
        return _dot(hn, w_ref[:, c0:c1])

    def head_norm(y, gain):
        ss = _dot((y * y).astype(BF16), bd_ref[...])
        return y * lax.rsqrt(ss * (1.0 / ATT_DH) + EPS) * gain

    qa_ref, ka_ref, va_ref, ra_ref, qb_ref, kb_ref, vb_ref, qi_ref, ga_ref, gb_ref, sm_ref = out_refs
    qa_ref[...] = (proj("qa") * (GLA_DK ** -0.5)).astype(BF16)
    ka_ref[...] = proj("ka").astype(BF16)
    va_ref[...] = proj("va").astype(BF16)
    ra_ref[...] = proj("ra").astype(BF16)
    qb_ref[...] = (head_norm(proj("qb"), gq_ref[...]) * (ATT_DH ** -0.5)).astype(BF16)
    kb_ref[...] = head_norm(proj("kb"), gk_ref[...]).astype(BF16)
    vb_ref[...] = proj("vb").astype(BF16)
    qi_ref[...] = proj("qi").astype(BF16)
    ga_ref[...] = proj("ga").astype(BF16)
    gb_ref[...] = proj("gb").astype(BF16)
    sm_ref[...] = proj("sm")


def _in_projection(x2, g_mix, w_in, g_q, g_k):
    n, d = x2.shape
    widths = {name: (d if w is None else w) for name, w in _PROJ}
    ref_order = (("qa", GLA_QK_W), ("ka", GLA_QK_W), ("va", GLA_V_W), ("ra", GLA_V_W),
                 ("alr", GLA_GATE_RANK), ("qb", ATT_W), ("kb", ATT_W), ("vb", ATT_W),
                 ("qi", IDX_Q_W), ("ki", IDX_DH), ("wi", IDX_HEADS), ("ga", d), ("gb", d))
    cols, c = {}, 0
    for name, w in ref_order:
        cols[name] = w_in[:, c:c + w]
        c += w
    small = jnp.concatenate(
        [cols["alr"], cols["ki"], cols["wi"],
         jnp.zeros((d, LANES - GLA_GATE_RANK - IDX_DH - IDX_HEADS), w_in.dtype)], axis=1)
    cols["sm"] = small
    w_cat = jnp.concatenate([cols[name] for name, _ in _PROJ], axis=1).astype(BF16)
    offs, c = {}, 0
    for name, _ in _PROJ:
        offs[name] = (c, c + widths[name])
        c += widths[name]
    d_cat = c

    head = jnp.arange(ATT_W) // ATT_DH
    blockdiag = (head[:, None] == head[None, :]).astype(BF16)
    gq = jnp.tile(g_q, ATT_HEADS)[None, :].astype(F32)
    gk = jnp.tile(g_k, ATT_HEADS)[None, :].astype(F32)

    tm = ROW_TILE
    const = lambda i: (0, 0)
    out_shape = [jax.ShapeDtypeStruct((n, widths[name]), F32 if name == "sm" else BF16)
                 for name, _ in _PROJ]
    out_specs = [pl.BlockSpec((tm, widths[name]), lambda i: (i, 0)) for name, _ in _PROJ]
    outs = pl.pallas_call(
        functools.partial(_inproj_kernel, offs=offs),
        grid=(n // tm,),
        in_specs=[pl.BlockSpec((tm, d), lambda i: (i, 0)),
                  pl.BlockSpec((1, d), const),
                  pl.BlockSpec((d, d_cat), const),
                  pl.BlockSpec((1, ATT_W), const),
                  pl.BlockSpec((1, ATT_W), const),
                  pl.BlockSpec((ATT_W, ATT_W), const)],
        out_specs=out_specs,
        out_shape=out_shape,
        compiler_params=pltpu.CompilerParams(dimension_semantics=("arbitrary",),
                                             vmem_limit_bytes=VMEM_LIMIT),
        name="in_projection",
    )(x2, g_mix[None, :].astype(F32), w_cat, gq, gk, blockdiag)
    return dict(zip([name for name, _ in _PROJ], outs))


def _gla_kernel(qa_ref, ka_ref, va_ref, ra_ref, sm_ref, ga_ref, wa2_ref, ba_ref, ltri_ref,
                lall_ref, gg_ref, wbr_ref, out_ref, st_ref, o_ref):
    @pl.when(pl.program_id(1) == 0)
    def _():
        st_ref[...] = jnp.zeros_like(st_ref)

    tb = qa_ref.shape[0]
    z = _dot(sm_ref[...].astype(BF16), wa2_ref[...]) + ba_ref[...]
    log_a = (jnp.minimum(z, 0.0) - jnp.log1p(jnp.exp(-jnp.abs(z)))) * (1.0 / GLA_TAU)
    la_hi, la_lo = _split_bf16(log_a)
    ltri = ltri_ref[...]
    lall = lall_ref[...]
    cum = _dot(ltri, la_hi) + _dot(ltri, la_lo)
    tot = _dot(lall, la_hi) + _dot(lall, la_lo)
    k_dec = (ka_ref[...].astype(F32) * jnp.exp(tot - cum)).astype(BF16)
    dec = jnp.exp(tot)

    states = [st_ref[h] for h in range(GLA_HEADS)]
    for c in range(tb // CHUNK):
        rows = slice(c * CHUNK, (c + 1) * CHUNK)
        for h in range(GLA_HEADS):
            kl = slice(h * GLA_DK, (h + 1) * GLA_DK)
            vl = slice(h * GLA_DV, (h + 1) * GLA_DV)
            u_t = _dot_tn(va_ref[rows, vl], k_dec[rows, kl])
            s = states[h] * dec[c * CHUNK:c * CHUNK + 1, kl] + u_t
            states[h] = s
            o_ref[rows, vl] = _dot_nt(qa_ref[rows, kl], s.astype(BF16))
    for h in range(GLA_HEADS):
        st_ref[h] = states[h]

    r = ra_ref[...].astype(F32)
    gated = []
    for h in range(GLA_HEADS):
        vl = slice(h * GLA_DV, (h + 1) * GLA_DV)
        oh = o_ref[:, vl]
        ms = jnp.mean(oh * oh, axis=-1, keepdims=True)
        oh = oh * lax.rsqrt(ms + EPS) * gg_ref[...]
        rh = r[:, vl]
        gated.append((oh * (rh * jax.nn.sigmoid(rh))).astype(BF16))
    og = jnp.concatenate(gated, axis=1)
    ya = _dot(og, wbr_ref[...])
    out_ref[...] = (jax.nn.sigmoid(ga_ref[...].astype(F32)) * ya).astype(BF16)


def _gla_mixer(p, w_alpha2, b_alpha, g_gla, w_br_gla, batch, seq):
    n = batch * seq
    d = w_br_gla.shape[1]
    tb = ROW_TILE
    nt = seq // tb
    wa2 = jnp.zeros((LANES, GLA_QK_W), F32).at[SM_ALR:SM_ALR + GLA_GATE_RANK].set(w_alpha2).astype(BF16)
    r = jnp.arange(tb)
    same = (r[:, None] // CHUNK) == (r[None, :] // CHUNK)
    ltri = (same & (r[None, :] <= r[:, None])).astype(BF16)
    lall = same.astype(BF16)
    row = lambda b, i: (b * nt + i, 0)
    const = lambda b, i: (0, 0)
    return pl.pallas_call(
        _gla_kernel,
        grid=(batch, nt),
        in_specs=[pl.BlockSpec((tb, GLA_QK_W), row), pl.BlockSpec((tb, GLA_QK_W), row),
                  pl.BlockSpec((tb, GLA_V_W), row), pl.BlockSpec((tb, GLA_V_W), row),
                  pl.BlockSpec((tb, LANES), row), pl.BlockSpec((tb, d), row),
                  pl.BlockSpec((LANES, GLA_QK_W), const), pl.BlockSpec((1, GLA_QK_W), const),
                  pl.BlockSpec((tb, tb), const), pl.BlockSpec((tb, tb), const),
                  pl.BlockSpec((1, GLA_DV), const), pl.BlockSpec((GLA_V_W, d), const)],
        out_specs=pl.BlockSpec((tb, d), row),
        out_shape=jax.ShapeDtypeStruct((n, d), BF16),
        scratch_shapes=[pltpu.VMEM((GLA_HEADS, GLA_DV, GLA_DK), F32),
                        pltpu.VMEM((tb, GLA_V_W), F32)],
        compiler_params=pltpu.CompilerParams(dimension_semantics=("arbitrary", "arbitrary"),
                                             vmem_limit_bytes=VMEM_LIMIT),
        name="gla_mixer",
    )(p["qa"], p["ka"], p["va"], p["ra"], p["sm"], p["ga"], wa2, b_alpha[None, :].astype(F32),
      ltri, lall, g_gla[None, :].astype(F32), w_br_gla.astype(BF16))


def _t5_bucket_int(rel):
    half = REL_BUCKETS // 2
    exact = half // 2
    n = jnp.abs(rel)
    large = jnp.full(rel.shape, exact, I32)
    for j in range(1, half - exact):
        thr = math.ceil(exact * (REL_MAX_DIST / exact) ** (j / (half - exact)) - 1e-9)
        large = large + jnp.where(n >= thr, 1, 0)
    return jnp.where(rel > 0, half, 0) + jnp.where(n < exact, n, large)


def _dsa_kernel(relb_ref, qb_ref, qi_ref, sm_ref, gb_ref, ya_ref, kb_ref, vb_ref, kit_ref,
                wbr_ref, out_ref, sc_ref, mb_ref, mbf_ref, vaug_ref, biasw_ref, qp_ref,
                qst_ref, s_ref, mrow_ref, acc_ref, o_ref, *, k_top, seq):
    b = pl.program_id(0)
    j = pl.program_id(1)
    tq = Q_TILE
    t0 = j * tq
    n_pairs = ATT_HEADS // 2
    wwin = 2 * tq

    @pl.when((b == 0) & (j == 0))
    def _():
        rq = lax.broadcasted_iota(I32, (tq, wwin), 0)
        rc = lax.broadcasted_iota(I32, (tq, wwin), 1)
        bucket = _t5_bucket_int(rc - tq - rq)
        far = REL_BUCKETS // 2 - 1
        for h in range(ATT_HEADS):
            a = jnp.zeros((tq, wwin), F32)
            for bk in range(REL_BUCKETS):
                a = jnp.where(bucket == bk, relb_ref[bk * ATT_HEADS + h], a)
            a = a - relb_ref[far * ATT_HEADS + h]
            rows = slice((h % 2) * tq, (h % 2 + 1) * tq)
            biasw_ref[0, h // 2, rows, :] = a
            biasw_ref[1, h // 2, rows, :] = jnp.concatenate(
                [a[:, tq:], jnp.zeros((tq, tq), F32)], axis=1)

    @pl.when(j == 0)
    def _():
        def body(i, _):
            rows = pl.ds(pl.multiple_of(i * K_TILE, K_TILE), K_TILE)
            for p in range(n_pairs):
                vaug_ref[rows, 2 * p * LANES:(2 * p + 1) * LANES] = vb_ref[0, rows, p * LANES:(p + 1) * LANES]
                vaug_ref[rows, (2 * p + 1) * LANES:(2 * p + 2) * LANES] = jnp.ones((K_TILE, LANES), BF16)
            return 0
        lax.fori_loop(0, seq // K_TILE, body, 0)

    lane_q = lax.broadcasted_iota(I32, (tq, LANES), 1)
    for p in range(n_pairs):
        qpair = qb_ref[:, p * LANES:(p + 1) * LANES]
        zero = jnp.zeros_like(qpair)
        qp_ref[p, 0:tq, :] = jnp.where(lane_q < ATT_DH, qpair, zero)
        qp_ref[p, tq:2 * tq, :] = jnp.where(lane_q >= ATT_DH, qpair, zero)
    for h in range(IDX_HEADS):
        qst_ref[h * tq:(h + 1) * tq, :] = qi_ref[:, h * IDX_DH:(h + 1) * IDX_DH]
    wi = sm_ref[:, SM_WI:SM_WI + IDX_HEADS] * (IDX_HEADS ** -0.5 * IDX_DH ** -0.5)
    wb = [jnp.broadcast_to(wi[:, h:h + 1], (tq, S_TILE)) for h in range(IDX_HEADS)]

    n_sel = (j + 2) // 2
    row_s = lax.broadcasted_iota(I32, (tq, S_TILE), 0)
    lane_s = lax.broadcasted_iota(I32, (tq, S_TILE), 1)
    limit = t0 + ((row_s >> 6) + 1) * CHUNK

    def score_body(kt, _):
        ks = pl.multiple_of(kt * S_TILE, S_TILE)
        s_all = _dot(qst_ref[...], kit_ref[0, :, pl.ds(ks, S_TILE)])
        score = jnp.zeros((tq, S_TILE), F32)
        for h in range(IDX_HEADS):
            score = score + jnp.maximum(s_all[h * tq:(h + 1) * tq], 0.0) * wb[h]
        sc_ref[:, pl.ds(ks, S_TILE)] = jnp.where(ks + lane_s < limit, score, -jnp.inf)
        return 0
    lax.fori_loop(0, n_sel, score_body, 0)

    n_cnt = (n_sel + 1) // 2

    @pl.when(n_sel % 2 == 1)
    def _():
        sc_ref[:, pl.ds(pl.multiple_of(n_sel * S_TILE, S_TILE), S_TILE)] = jnp.full((tq, S_TILE), -jnp.inf, F32)

    kf = float(k_top)
    c_groups = C_TILE // LANES

    def score_group(kt, g):
        return sc_ref[:, pl.ds(pl.multiple_of(kt * C_TILE + g * LANES, LANES), LANES)]

    def lane_sum(c):
        return jnp.sum(c, axis=-1, keepdims=True)

    def count(cand, strict):
        cb = jnp.broadcast_to(cand, (tq, LANES))

        def body(kt, c):
            for g in range(c_groups):
                sc = score_group(kt, g)
                c = c + jnp.where((sc > cb) if strict else (sc >= cb), 1.0, 0.0)
            return c
        return lane_sum(lax.fori_loop(0, n_cnt, body, jnp.zeros((tq, LANES), F32)))

    def stats_body(kt, carry):
        lo, hi, n_fin, n_ge0, n_gt0 = carry
        for g in range(c_groups):
            sc = score_group(kt, g)
            fin = sc > -jnp.inf
            hi = jnp.maximum(hi, sc)
            lo = jnp.minimum(lo, jnp.where(fin, sc, jnp.inf))
            n_fin = n_fin + jnp.where(fin, 1.0, 0.0)
            n_ge0 = n_ge0 + jnp.where(sc >= 0.0, 1.0, 0.0)
            n_gt0 = n_gt0 + jnp.where(sc > 0.0, 1.0, 0.0)
        return lo, hi, n_fin, n_ge0, n_gt0
    zeros_l = jnp.zeros((tq, LANES), F32)
    lo_l, hi_l, fin_l, ge0_l, gt0_l = lax.fori_loop(
        0, n_cnt, stats_body,
        (jnp.full((tq, LANES), jnp.inf, F32), jnp.full((tq, LANES), -jnp.inf, F32),
         zeros_l, zeros_l, zeros_l))
    row_min = jnp.min(lo_l, axis=-1, keepdims=True)
    row_max = jnp.max(hi_l, axis=-1, keepdims=True)
    n_adm = lane_sum(fin_l)
    c_ge0 = lane_sum(ge0_l)
    c_gt0 = lane_sum(gt0_l)
    has_thr = n_adm >= kf
    c_max = count(row_max, False)
    at_max = c_max >= kf
    at_zero = (c_gt0 < kf) & (c_ge0 >= kf)
    above_zero = c_gt0 >= kf
    lo0 = jnp.where(at_max, row_max, jnp.where(at_zero | above_zero, 0.0, row_min))
    cnt0 = jnp.where(at_max, c_max, jnp.where(at_zero | above_zero, c_ge0, n_adm))
    hi0 = jnp.where(at_zero | above_zero, row_max, 0.0)
    done0 = jnp.logical_not(has_thr) | at_max | at_zero | (cnt0 == kf)

    def as_flag(done):
        return jnp.where(done, 1.0, 0.0)

    def bisect_cond(carry):
        return carry[3] < 0.5

    def bisect_body(carry):
        lo, hi, cnt_lo, _ = carry
        mid = 0.5 * lo + 0.5 * hi
        open_ = (mid > lo) & (mid < hi)
        c = count(mid, False)
        ge = c >= kf
        lo = jnp.where(ge, mid, lo)
        cnt_lo = jnp.where(ge, c, cnt_lo)
        hi = jnp.where(ge, hi, mid)
        conv = jnp.where(open_ & (cnt_lo != kf), pre_done, 1.0)
        return lo, hi, cnt_lo, jnp.min(conv)
    pre_done = as_flag(done0)
    thr, _, cnt_thr, _ = lax.while_loop(
        bisect_cond, bisect_body, (lo0, hi0, cnt0, jnp.min(pre_done)))

    tie = has_thr & (cnt_thr > kf)
    any_tie = jnp.max(jnp.where(tie, 1.0, 0.0)) > 0.0
    far_end = t0 - tq
    f32_min = float(jnp.finfo(F32).min)

    def write_mask(ks, sel):
        pos = ks + lax.broadcasted_iota(I32, (tq, LANES), 1)
        mb_ref[:, pl.ds(ks, LANES)] = jnp.where(sel, 0.0, NEG)
        mbf_ref[:, pl.ds(ks, LANES)] = jnp.where(sel & (pos < far_end), 0.0, NEG)

    @pl.when(jnp.logical_not(any_tie))
    def _():
        lo = jnp.broadcast_to(jnp.where(has_thr, thr, f32_min), (tq, LANES))

        def body(kt, _):
            ks = pl.multiple_of(kt * LANES, LANES)
            write_mask(ks, sc_ref[:, pl.ds(ks, LANES)] >= lo)
            return 0
        lax.fori_loop(0, n_sel * (S_TILE // LANES), body, 0)

    @pl.when(any_tie)
    def _():
        need = kf - count(thr, True)
        thr_b = jnp.broadcast_to(thr, (tq, LANES))
        lo = jnp.broadcast_to(jnp.where(has_thr, thr, -jnp.inf), (tq, LANES))
        eq_ok = jnp.broadcast_to(has_thr, (tq, LANES))
        ri = lax.broadcasted_iota(I32, (LANES, LANES), 0)
        ci = lax.broadcasted_iota(I32, (LANES, LANES), 1)
        tri = jnp.where(ri <= ci, 1.0, 0.0).astype(BF16)

        def body(kt, seen):
            ks = pl.multiple_of(kt * LANES, LANES)
            sc = sc_ref[:, pl.ds(ks, LANES)]
            eq = (sc == thr_b) & eq_ok
            prefix = seen + _dot(jnp.where(eq, 1.0, 0.0).astype(BF16), tri)
            write_mask(ks, (sc > lo) | (eq & (prefix <= need)))
            return prefix[:, LANES - 1:LANES]
        lax.fori_loop(0, n_sel * (S_TILE // LANES), body, jnp.zeros((tq, 1), F32))

    n_far = (j + 2) // 4
    ws = pl.multiple_of(jnp.maximum(t0 - tq, 0), tq)
    first = (j == 0).astype(I32)
    lane_o = lax.broadcasted_iota(I32, (tq, LANES), 1)

    def lane_group_max(s):
        m = s[:, :LANES]
        for g in range(1, s.shape[1] // LANES):
            m = jnp.maximum(m, s[:, g * LANES:(g + 1) * LANES])
        return m

    def pair_lanes(p, width=1):
        return slice(width * p * LANES, width * (p + 1) * LANES)

    mrow_ref[...] = jnp.full(mrow_ref.shape, NEG, F32)

    def logits_body(kt, _):
        ks = pl.multiple_of(kt * K_TILE, K_TILE)
        mbt = mbf_ref[:, pl.ds(ks, K_TILE)]
        mb2 = jnp.concatenate([mbt, mbt], axis=0)
        for p in range(n_pairs):
            s = _dot_nt(qp_ref[p], kb_ref[0, pl.ds(ks, K_TILE), pair_lanes(p)]) + mb2
            s_ref[p, :, pl.ds(ks, K_TILE)] = s
            mrow_ref[p] = jnp.maximum(mrow_ref[p], lane_group_max(s))
        return 0
    lax.fori_loop(0, n_far, logits_body, 0)

    mbw = mb_ref[:, pl.ds(ws, wwin)]
    mbw2 = jnp.concatenate([mbw, mbw], axis=0)
    for p in range(n_pairs):
        sw = _dot_nt(qp_ref[p], kb_ref[0, pl.ds(ws, wwin), pair_lanes(p)]) + mbw2 + biasw_ref[first, p]
        m_acc = jnp.maximum(mrow_ref[p], lane_group_max(sw))
        m_row = jnp.broadcast_to(jnp.max(m_acc, axis=-1, keepdims=True), (2 * tq, LANES))
        mrow_ref[p] = m_row
        pw = jnp.exp(sw - jnp.concatenate([m_row] * (wwin // LANES), axis=1)).astype(BF16)
        acc_ref[p] = _dot(pw, vaug_ref[pl.ds(ws, wwin), pair_lanes(p, 2)])

    def pv_body(kt, _):
        ks = pl.multiple_of(kt * K_TILE, K_TILE)
        for p in range(n_pairs):
            m_far = jnp.concatenate([mrow_ref[p]] * (K_TILE // LANES), axis=1)
            pexp = jnp.exp(s_ref[p, :, pl.ds(ks, K_TILE)] - m_far).astype(BF16)
            acc_ref[p] += _dot(pexp, vaug_ref[pl.ds(ks, K_TILE), pair_lanes(p, 2)])
        return 0
    lax.fori_loop(0, n_far, pv_body, 0)

    for p in range(n_pairs):
        a = acc_ref[p]
        o_even = a[:tq, :LANES] / a[:tq, LANES:LANES + 1]
        o_odd = a[tq:, :LANES] / a[tq:, LANES:LANES + 1]
        o_ref[:, pair_lanes(p)] = jnp.where(lane_o < ATT_DH, o_even, o_odd).astype(BF16)

    yb = _dot(o_ref[...], wbr_ref[...])
    mixed = jax.nn.sigmoid(gb_ref[...].astype(F32)) * yb + ya_ref[...].astype(F32)
    out_ref[...] = mixed.astype(BF16)


def _dsa_mixer(p, ya, rel_bias, w_br_att, batch, seq):
    n = batch * seq
    d = w_br_att.shape[1]
    tq = Q_TILE
    nq = seq // tq
    k_top = min(TOPK_MAX, seq // 4)
    kb3 = p["kb"].reshape(batch, seq, ATT_W)
    vb3 = p["vb"].reshape(batch, seq, ATT_W)
    kit = p["sm"][:, SM_KI:SM_KI + IDX_DH].astype(BF16).reshape(batch, seq, IDX_DH).swapaxes(1, 2)
    row = lambda b, j: (b * nq + j, 0)
    per_b = lambda b, j: (b, 0, 0)
    const = lambda b, j: (0, 0)
    return pl.pallas_call(
        functools.partial(_dsa_kernel, k_top=k_top, seq=seq),
        grid=(batch, nq),
        in_specs=[pl.BlockSpec(memory_space=pltpu.SMEM),
                  pl.BlockSpec((tq, ATT_W), row), pl.BlockSpec((tq, IDX_Q_W), row),
                  pl.BlockSpec((tq, LANES), row), pl.BlockSpec((tq, d), row),
                  pl.BlockSpec((tq, d), row),
                  pl.BlockSpec((1, seq, ATT_W), per_b, pipeline_mode=pl.Buffered(1)),
                  pl.BlockSpec((1, seq, ATT_W), per_b, pipeline_mode=pl.Buffered(1)),
                  pl.BlockSpec((1, IDX_DH, seq), per_b, pipeline_mode=pl.Buffered(1)),
                  pl.BlockSpec((ATT_W, d), const)],
        out_specs=pl.BlockSpec((tq, d), row),
        out_shape=jax.ShapeDtypeStruct((n, d), BF16),
        scratch_shapes=[pltpu.VMEM((tq, seq), F32),
                        pltpu.VMEM((tq, seq), F32),
                        pltpu.VMEM((tq, seq), F32),
                        pltpu.VMEM((seq, 2 * ATT_W), BF16),
                        pltpu.VMEM((2, ATT_HEADS // 2, 2 * tq, 2 * tq), F32),
                        pltpu.VMEM((ATT_HEADS // 2, 2 * tq, LANES), BF16),
                        pltpu.VMEM((IDX_HEADS * tq, IDX_DH), BF16),
                        pltpu.VMEM((ATT_HEADS // 2, 2 * tq, seq), F32),
                        pltpu.VMEM((ATT_HEADS // 2, 2 * tq, LANES), F32),
                        pltpu.VMEM((ATT_HEADS // 2, 2 * tq, 2 * LANES), F32),
                        pltpu.VMEM((tq, ATT_W), BF16)],
        compiler_params=pltpu.CompilerParams(dimension_semantics=("arbitrary", "arbitrary"),
                                             vmem_limit_bytes=VMEM_LIMIT),
        name="dsa_mixer",
    )(rel_bias.astype(F32).reshape(-1), p["qb"], p["qi"], p["sm"], p["gb"], ya, kb3, vb3, kit,
      w_br_att.astype(BF16))


RT_G0 = 0
RT_E0 = N_GROUPS


def _moe_kernel(x_ref, mx_ref, wo_ref, gf_ref, wrh_ref, wrl_ref, br_ref, wg_ref, wu_ref,
                wd_ref, out_ref, x1_ref, h2_ref, comb_ref, y_ref):
    g = pl.program_id(1)
    tm = x_ref.shape[0]

    @pl.when(g == 0)
    def _():
        x1 = x_ref[...] + _dot(mx_ref[...], wo_ref[...])
        x1_ref[...] = x1
        ms = jnp.mean(x1 * x1, axis=-1, keepdims=True)
        h = x1 * lax.rsqrt(ms + EPS) * gf_ref[...]
        h_hi, h_lo = _split_bf16(h)
        h2_ref[...] = h_hi
        logits = (_dot(h_hi, wrh_ref[...]) + _dot(h_lo, wrh_ref[...]) + _dot(h_hi, wrl_ref[...])
                  + br_ref[...])
        lane = lax.broadcasted_iota(I32, (tm, LANES), 1)
        big = jnp.int32(LANES)
        is_g = lane < N_GROUPS
        gl = jnp.where(is_g, logits, -jnp.inf)
        gmax = jnp.max(gl, axis=-1, keepdims=True)
        gidx = jnp.min(jnp.where(gl == gmax, lane, big), axis=-1, keepdims=True)
        gsum = jnp.sum(jnp.where(is_g, jnp.exp(gl - gmax), 0.0), axis=-1, keepdims=True)
        g_w = 1.0 / gsum
        in_grp = (lane >= RT_E0) & (lane < RT_E0 + N_EXPERTS) & (((lane - RT_E0) >> 3) == gidx)
        ev = jnp.where(in_grp, logits, -jnp.inf)
        t1 = jnp.max(ev, axis=-1, keepdims=True)
        i1 = jnp.min(jnp.where(ev == t1, lane, big), axis=-1, keepdims=True)
        ev2 = jnp.where(lane == i1, -jnp.inf, ev)
        t2 = jnp.max(ev2, axis=-1, keepdims=True)
        i2 = jnp.min(jnp.where(ev2 == t2, lane, big), axis=-1, keepdims=True)
        e2 = jnp.exp(t2 - t1)
        w1 = 1.0 / (1.0 + e2)
        w2 = e2 * w1
        comb = g_w * (jnp.where(lane == i1, w1, 0.0) + jnp.where(lane == i2, w2, 0.0))
        for grp in range(N_GROUPS):
            e0 = RT_E0 + grp * EXPERTS_PER_GROUP
            comb_ref[grp] = comb[:, e0:e0 + EXPERTS_PER_GROUP]
        y_ref[...] = jnp.zeros_like(y_ref)

    h2 = h2_ref[...]
    cw = comb_ref[g]
    for c in range(EXPERTS_PER_GROUP * D_EXPERT // MOE_COLS):
        cols = slice(c * MOE_COLS, (c + 1) * MOE_COLS)
        hg = _dot(h2, wg_ref[0, :, cols])
        hu = _dot(h2, wu_ref[0, :, cols])
        e_first = c * (MOE_COLS // D_EXPERT)
        scale = jnp.concatenate(
            [jnp.broadcast_to(cw[:, e:e + 1], (tm, D_EXPERT))
             for e in range(e_first, e_first + MOE_COLS // D_EXPERT)], axis=1)
        hid = (hg * jax.nn.sigmoid(hg) * hu * scale).astype(BF16)
        y_ref[...] += _dot(hid, wd_ref[0, cols, :])

    @pl.when(g == N_GROUPS - 1)
    def _():
        out_ref[...] = x1_ref[...] + y_ref[...]


def _out_proj_moe(x2, mixed, w_out, g_ffn, w_rg, b_rg, w_re, b_re, w_gate, w_up, w_down):
    n, d = x2.shape
    tm = ROW_TILE
    gw = EXPERTS_PER_GROUP * D_EXPERT
    w_r = jnp.zeros((d, LANES), F32).at[:, RT_G0:RT_G0 + N_GROUPS].set(w_rg)
    w_r = w_r.at[:, RT_E0:RT_E0 + N_EXPERTS].set(w_re)
    wr_hi = w_r.astype(BF16)
    wr_lo = (w_r - wr_hi.astype(F32)).astype(BF16)
    b_r = jnp.zeros((1, LANES), F32).at[0, RT_G0:RT_G0 + N_GROUPS].set(b_rg)
    b_r = b_r.at[0, RT_E0:RT_E0 + N_EXPERTS].set(b_re)

    def by_group(w):
        return (w.reshape(N_GROUPS, EXPERTS_PER_GROUP, d, D_EXPERT).transpose(0, 2, 1, 3)
                .reshape(N_GROUPS, d, gw).astype(BF16))
    wg = by_group(w_gate)
    wu = by_group(w_up)
    wd = w_down.reshape(N_GROUPS, gw, d).astype(BF16)

    row = lambda i, g: (i, 0)
    const = lambda i, g: (0, 0)
    grp3 = lambda i, g: (g, 0, 0)
    return pl.pallas_call(
        _moe_kernel,
        grid=(n // tm, N_GROUPS),
        in_specs=[pl.BlockSpec((tm, d), row), pl.BlockSpec((tm, d), row),
                  pl.BlockSpec((d, d), const), pl.BlockSpec((1, d), const),
                  pl.BlockSpec((d, LANES), const), pl.BlockSpec((d, LANES), const),
                  pl.BlockSpec((1, LANES), const),
                  pl.BlockSpec((1, d, gw), grp3), pl.BlockSpec((1, d, gw), grp3),
                  pl.BlockSpec((1, gw, d), grp3)],
        out_specs=pl.BlockSpec((tm, d), row),
        out_shape=jax.ShapeDtypeStruct((n, d), F32),
        scratch_shapes=[pltpu.VMEM((tm, d), F32), pltpu.VMEM((tm, d), BF16),
                        pltpu.VMEM((N_GROUPS, tm, EXPERTS_PER_GROUP), F32),
                        pltpu.VMEM((tm, d), F32)],
        compiler_params=pltpu.CompilerParams(dimension_semantics=("arbitrary", "arbitrary"),
                                             vmem_limit_bytes=VMEM_LIMIT),
        name="out_proj_moe",
    )(x2, mixed, w_out.astype(BF16), g_ffn[None, :].astype(F32), wr_hi, wr_lo, b_r,
      wg, wu, wd)


def kernel(x, g_mix, w_in, w_alpha2, b_alpha, g_gla, w_br_gla, g_q, g_k, rel_bias, w_br_att, w_out, g_ffn, w_rg, b_rg, w_re, b_re, w_gate, w_up, w_down):
    batch, seq, d = x.shape
    assert seq % ROW_TILE == 0 and seq % K_TILE == 0 and (batch * seq) % ROW_TILE == 0
    x2 = x.reshape(batch * seq, d)
    for l in range(g_mix.shape[0]):
        p = _in_projection(x2, g_mix[l], w_in[l], g_q[l], g_k[l])
        ya = _gla_mixer(p, w_alpha2[l], b_alpha[l], g_gla[l], w_br_gla[l], batch, seq)
        mixed = _dsa_mixer(p, ya, rel_bias, w_br_att[l], batch, seq)
        x2 = _out_proj_moe(x2, mixed, w_out[l], g_ffn[l], w_rg[l], b_rg[l], w_re[l], b_re[l],
                           w_gate[l], w_up[l], w_down[l])
    return x2.reshape(batch, seq, d)
```

```python
import functools
import math

import jax
import jax.numpy as jnp
from jax import lax
from jax.experimental import pallas as pl
from jax.experimental.pallas import tpu as pltpu

F32 = jnp.float32
BF16 = jnp.bfloat16
I32 = jnp.int32

CHUNK = 64
GLA_HEADS = 4
GLA_DK = 64
GLA_DV = 128
GLA_GATE_RANK = 16
GLA_TAU = 16.0
ATT_HEADS = 8
ATT_DH = 64
IDX_HEADS = 8
IDX_DH = 32
TOPK_MAX = 256
REL_BUCKETS = 32
REL_MAX_DIST = 128
N_GROUPS = 4
EXPERTS_PER_GROUP = 8
N_EXPERTS = N_GROUPS * EXPERTS_PER_GROUP
D_EXPERT = 128
EPS = 1e-6

GLA_QK_W = GLA_HEADS * GLA_DK
GLA_V_W = GLA_HEADS * GLA_DV
ATT_W = ATT_HEADS * ATT_DH
IDX_Q_W = IDX_HEADS * IDX_DH

LANES = 128
VMEM_LIMIT = 56 * 1024 * 1024

SM_ALR = 0
SM_KI = GLA_GATE_RANK
SM_WI = GLA_GATE_RANK + IDX_DH

NEG = -1e30

ROW_TILE = 512
Q_TILE = 128
K_TILE = 512
MOE_COLS = 256
S_TILE = 256
C_TILE = 2 * S_TILE
BISECT_STEPS = 2


def _dot(a, b):
    return jnp.dot(a, b, preferred_element_type=F32)


def _dot_nt(a, b):
    return lax.dot_general(a, b, (((1,), (1,)), ((), ())), preferred_element_type=F32)


def _dot_tn(a, b):
    return lax.dot_general(a, b, (((0,), (0,)), ((), ())), preferred_element_type=F32)


def _split_bf16(a):
    hi = a.astype(BF16)
    lo = (a - hi.astype(F32)).astype(BF16)
    return hi, lo


_PROJ = (("qa", GLA_QK_W), ("ka", GLA_QK_W), ("va", GLA_V_W), ("ra", GLA_V_W),
         ("qb", ATT_W), ("kb", ATT_W), ("vb", ATT_W), ("qi", IDX_Q_W),
         ("ga", None), ("gb", None), ("sm", LANES))


def _inproj_kernel(x_ref, g_ref, w_ref, gq_ref, gk_ref, bd_ref, *out_refs, offs):
    x = x_ref[...]
    ms = jnp.mean(x * x, axis=-1, keepdims=True)
    hn = (x * lax.rsqrt(ms + EPS) * g_ref[...]).astype(BF16)

    def proj(name):
        c0, c1 = offs[name]
        return _dot(hn, w_ref[:, c0:c1])

    def head_norm(y, gain):
        ss = _dot((y * y).astype(BF16), bd_ref[...])
        return y * lax.rsqrt(ss * (1.0 / ATT_DH) + EPS) * gain

    qa_ref, ka_ref, va_ref, ra_ref, qb_ref, kb_ref, vb_ref, qi_ref, ga_ref, gb_ref, sm_ref = out_refs
    qa_ref[...] = (proj("qa") * (GLA_DK ** -0.5)).astype(BF16)
    ka_ref[...] = proj("ka").astype(BF16)
    va_ref[...] = proj("va").astype(BF16)
    ra_ref[...] = proj("ra").astype(BF16)
    qb_ref[...] = (head_norm(proj("qb"), gq_ref[...]) * (ATT_DH ** -0.5)).astype(BF16)
    kb_ref[...] = head_norm(proj("kb"), gk_ref[...]).astype(BF16)
    vb_ref[...] = proj("vb").astype(BF16)
    qi_ref[...] = proj("qi").astype(BF16)
    ga_ref[...] = proj("ga").astype(BF16)
    gb_ref[...] = proj("gb").astype(BF16)
    sm_ref[...] = proj("sm")


def _in_projection(x2, g_mix, w_in, g_q, g_k):
    n, d = x2.shape
    widths = {name: (d if w is None else w) for name, w in _PROJ}
    ref_order = (("qa", GLA_QK_W), ("ka", GLA_QK_W), ("va", GLA_V_W), ("ra", GLA_V_W),
                 ("alr", GLA_GATE_RANK), ("qb", ATT_W), ("kb", ATT_W), ("vb", ATT_W),
                 ("qi", IDX_Q_W), ("ki", IDX_DH), ("wi", IDX_HEADS), ("ga", d), ("gb", d))
    cols, c = {}, 0
    for name, w in ref_order:
        cols[name] = w_in[:, c:c + w]
        c += w
    small = jnp.concatenate(
        [cols["alr"], cols["ki"], cols["wi"],
         jnp.zeros((d, LANES - GLA_GATE_RANK - IDX_DH - IDX_HEADS), w_in.dtype)], axis=1)
    cols["sm"] = small
    w_cat = jnp.concatenate([cols[name] for name, _ in _PROJ], axis=1).astype(BF16)
    offs, c = {}, 0
    for name, _ in _PROJ:
        offs[name] = (c, c + widths[name])
        c += widths[name]
    d_cat = c

    head = jnp.arange(ATT_W) // ATT_DH
    blockdiag = (head[:, None] == head[None, :]).astype(BF16)
    gq = jnp.tile(g_q, ATT_HEADS)[None, :].astype(F32)
    gk = jnp.tile(g_k, ATT_HEADS)[None, :].astype(F32)

    tm = ROW_TILE
    const = lambda i: (0, 0)
    out_shape = [jax.ShapeDtypeStruct((n, widths[name]), F32 if name == "sm" else BF16)
                 for name, _ in _PROJ]
    out_specs = [pl.BlockSpec((tm, widths[name]), lambda i: (i, 0)) for name, _ in _PROJ]
    outs = pl.pallas_call(
        functools.partial(_inproj_kernel, offs=offs),
        grid=(n // tm,),
        in_specs=[pl.BlockSpec((tm, d), lambda i: (i, 0)),
                  pl.BlockSpec((1, d), const),
                  pl.BlockSpec((d, d_cat), const),
                  pl.BlockSpec((1, ATT_W), const),
                  pl.BlockSpec((1, ATT_W), const),
                  pl.BlockSpec((ATT_W, ATT_W), const)],
        out_specs=out_specs,
        out_shape=out_shape,
        compiler_params=pltpu.CompilerParams(dimension_semantics=("arbitrary",),
                                             vmem_limit_bytes=VMEM_LIMIT),
        name="in_projection",
    )(x2, g_mix[None, :].astype(F32), w_cat, gq, gk, blockdiag)
    return dict(zip([name for name, _ in _PROJ], outs))


def _gla_kernel(qa_ref, ka_ref, va_ref, ra_ref, sm_ref, ga_ref, wa2_ref, ba_ref, ltri_ref,
                lall_ref, gg_ref, wbr_ref, out_ref, st_ref, o_ref):
    @pl.when(pl.program_id(1) == 0)
    def _():
        st_ref[...] = jnp.zeros_like(st_ref)

    tb = qa_ref.shape[0]
    z = _dot(sm_ref[...].astype(BF16), wa2_ref[...]) + ba_ref[...]
    log_a = (jnp.minimum(z, 0.0) - jnp.log1p(jnp.exp(-jnp.abs(z)))) * (1.0 / GLA_TAU)
    la_hi, la_lo = _split_bf16(log_a)
    ltri = ltri_ref[...]
    lall = lall_ref[...]
    cum = _dot(ltri, la_hi) + _dot(ltri, la_lo)
    tot = _dot(lall, la_hi) + _dot(lall, la_lo)
    k_dec = (ka_ref[...].astype(F32) * jnp.exp(tot - cum)).astype(BF16)
    dec = jnp.exp(tot)

    states = [st_ref[h] for h in range(GLA_HEADS)]
    for c in range(tb // CHUNK):
        rows = slice(c * CHUNK, (c + 1) * CHUNK)
        for h in range(GLA_HEADS):
            kl = slice(h * GLA_DK, (h + 1) * GLA_DK)
            vl = slice(h * GLA_DV, (h + 1) * GLA_DV)
            u_t = _dot_tn(va_ref[rows, vl], k_dec[rows, kl])
            s = states[h] * dec[c * CHUNK:c * CHUNK + 1, kl] + u_t
            states[h] = s
            o_ref[rows, vl] = _dot_nt(qa_ref[rows, kl], s.astype(BF16))
    for h in range(GLA_HEADS):
        st_ref[h] = states[h]

    r = ra_ref[...].astype(F32)
    gated = []
    for h in range(GLA_HEADS):
        vl = slice(h * GLA_DV, (h + 1) * GLA_DV)
        oh = o_ref[:, vl]
        ms = jnp.mean(oh * oh, axis=-1, keepdims=True)
        oh = oh * lax.rsqrt(ms + EPS) * gg_ref[...]
        rh = r[:, vl]
        gated.append((oh * (rh * jax.nn.sigmoid(rh))).astype(BF16))
    og = jnp.concatenate(gated, axis=1)
    ya = _dot(og, wbr_ref[...])
    out_ref[...] = (jax.nn.sigmoid(ga_ref[...].astype(F32)) * ya).astype(BF16)


def _gla_mixer(p, w_alpha2, b_alpha, g_gla, w_br_gla, batch, seq):
    n = batch * seq
    d = w_br_gla.shape[1]
    tb = ROW_TILE
    nt = seq // tb
    wa2 = jnp.zeros((LANES, GLA_QK_W), F32).at[SM_ALR:SM_ALR + GLA_GATE_RANK].set(w_alpha2).astype(BF16)
    r = jnp.arange(tb)
    same = (r[:, None] // CHUNK) == (r[None, :] // CHUNK)
    ltri = (same & (r[None, :] <= r[:, None])).astype(BF16)
    lall = same.astype(BF16)
    row = lambda b, i: (b * nt + i, 0)
    const = lambda b, i: (0, 0)
    return pl.pallas_call(
        _gla_kernel,
        grid=(batch, nt),
        in_specs=[pl.BlockSpec((tb, GLA_QK_W), row), pl.BlockSpec((tb, GLA_QK_W), row),
                  pl.BlockSpec((tb, GLA_V_W), row), pl.BlockSpec((tb, GLA_V_W), row),
                  pl.BlockSpec((tb, LANES), row), pl.BlockSpec((tb, d), row),
                  pl.BlockSpec((LANES, GLA_QK_W), const), pl.BlockSpec((1, GLA_QK_W), const),
                  pl.BlockSpec((tb, tb), const), pl.BlockSpec((tb, tb), const),
                  pl.BlockSpec((1, GLA_DV), const), pl.BlockSpec((GLA_V_W, d), const)],
        out_specs=pl.BlockSpec((tb, d), row),
        out_shape=jax.ShapeDtypeStruct((n, d), BF16),
        scratch_shapes=[pltpu.VMEM((GLA_HEADS, GLA_DV, GLA_DK), F32),
                        pltpu.VMEM((tb, GLA_V_W), F32)],
        compiler_params=pltpu.CompilerParams(dimension_semantics=("arbitrary", "arbitrary"),
                                             vmem_limit_bytes=VMEM_LIMIT),
        name="gla_mixer",
    )(p["qa"], p["ka"], p["va"], p["ra"], p["sm"], p["ga"], wa2, b_alpha[None, :].astype(F32),
      ltri, lall, g_gla[None, :].astype(F32), w_br_gla.astype(BF16))


def _t5_bucket_int(rel):
    half = REL_BUCKETS // 2
    exact = half // 2
    n = jnp.abs(rel)
    large = jnp.full(rel.shape, exact, I32)
    for j in range(1, half - exact):
        thr = math.ceil(exact * (REL_MAX_DIST / exact) ** (j / (half - exact)) - 1e-9)
        large = large + jnp.where(n >= thr, 1, 0)
    return jnp.where(rel > 0, half, 0) + jnp.where(n < exact, n, large)


def _old_dsa_kernel(relb_ref, qb_ref, qi_ref, sm_ref, gb_ref, ya_ref, kb_ref, vb_ref, kit_ref,
                wbr_ref, out_ref, sc_ref, mb_ref, mbf_ref, vaug_ref, biasw_ref, qp_ref,
                qst_ref, s_ref, mrow_ref, acc_ref, o_ref, *, k_top, seq):
    b = pl.program_id(0)
    j = pl.program_id(1)
    tq = Q_TILE
    t0 = j * tq
    n_pairs = ATT_HEADS // 2
    wwin = 2 * tq

    @pl.when((b == 0) & (j == 0))
    def _():
        rq = lax.broadcasted_iota(I32, (tq, wwin), 0)
        rc = lax.broadcasted_iota(I32, (tq, wwin), 1)
        bucket = _t5_bucket_int(rc - tq - rq)
        far = REL_BUCKETS // 2 - 1
        for h in range(ATT_HEADS):
            a = jnp.zeros((tq, wwin), F32)
            for bk in range(REL_BUCKETS):
                a = jnp.where(bucket == bk, relb_ref[bk * ATT_HEADS + h], a)
            a = a - relb_ref[far * ATT_HEADS + h]
            rows = slice((h % 2) * tq, (h % 2 + 1) * tq)
            biasw_ref[0, h // 2, rows, :] = a
            biasw_ref[1, h // 2, rows, :] = jnp.concatenate(
                [a[:, tq:], jnp.zeros((tq, tq), F32)], axis=1)

    @pl.when(j == 0)
    def _():
        def body(i, _):
            rows = pl.ds(pl.multiple_of(i * K_TILE, K_TILE), K_TILE)
            for p in range(n_pairs):
                vaug_ref[rows, 2 * p * LANES:(2 * p + 1) * LANES] = vb_ref[0, rows, p * LANES:(p + 1) * LANES]
                vaug_ref[rows, (2 * p + 1) * LANES:(2 * p + 2) * LANES] = jnp.ones((K_TILE, LANES), BF16)
            return 0
        lax.fori_loop(0, seq // K_TILE, body, 0)

    lane_q = lax.broadcasted_iota(I32, (tq, LANES), 1)
    for p in range(n_pairs):
        qpair = qb_ref[:, p * LANES:(p + 1) * LANES]
        zero = jnp.zeros_like(qpair)
        qp_ref[p, 0:tq, :] = jnp.where(lane_q < ATT_DH, qpair, zero)
        qp_ref[p, tq:2 * tq, :] = jnp.where(lane_q >= ATT_DH, qpair, zero)
    for h in range(IDX_HEADS):
        qst_ref[h * tq:(h + 1) * tq, :] = qi_ref[:, h * IDX_DH:(h + 1) * IDX_DH]
    wi = sm_ref[:, SM_WI:SM_WI + IDX_HEADS] * (IDX_HEADS ** -0.5 * IDX_DH ** -0.5)
    wb = [jnp.broadcast_to(wi[:, h:h + 1], (tq, S_TILE)) for h in range(IDX_HEADS)]

    n_sel = (j + 2) // 2
    row_s = lax.broadcasted_iota(I32, (tq, S_TILE), 0)
    lane_s = lax.broadcasted_iota(I32, (tq, S_TILE), 1)
    limit = t0 + ((row_s >> 6) + 1) * CHUNK

    def score_body(kt, _):
        ks = pl.multiple_of(kt * S_TILE, S_TILE)
        s_all = _dot(qst_ref[...], kit_ref[0, :, pl.ds(ks, S_TILE)])
        score = jnp.zeros((tq, S_TILE), F32)
        for h in range(IDX_HEADS):
            score = score + jnp.maximum(s_all[h * tq:(h + 1) * tq], 0.0) * wb[h]
        sc_ref[:, pl.ds(ks, S_TILE)] = jnp.where(ks + lane_s < limit, score, -jnp.inf)
        return 0
    lax.fori_loop(0, n_sel, score_body, 0)

    n_cnt = (n_sel + 1) // 2

    @pl.when(n_sel % 2 == 1)
    def _():
        sc_ref[:, pl.ds(pl.multiple_of(n_sel * S_TILE, S_TILE), S_TILE)] = jnp.full((tq, S_TILE), -jnp.inf, F32)

    kf = float(k_top)
    c_groups = C_TILE // LANES

    def score_group(kt, g):
        return sc_ref[:, pl.ds(pl.multiple_of(kt * C_TILE + g * LANES, LANES), LANES)]

    def lane_sum(c):
        return jnp.sum(c, axis=-1, keepdims=True)

    def count(cand, strict):
        cb = jnp.broadcast_to(cand, (tq, LANES))

        def body(kt, c):
            for g in range(c_groups):
                sc = score_group(kt, g)
                c = c + jnp.where((sc > cb) if strict else (sc >= cb), 1.0, 0.0)
            return c
        return lane_sum(lax.fori_loop(0, n_cnt, body, jnp.zeros((tq, LANES), F32)))

    def stats_body(kt, carry):
        lo, hi, n_fin, n_ge0, n_gt0 = carry
        for g in range(c_groups):
            sc = score_group(kt, g)
            fin = sc > -jnp.inf
            hi = jnp.maximum(hi, sc)
            lo = jnp.minimum(lo, jnp.where(fin, sc, jnp.inf))
            n_fin = n_fin + jnp.where(fin, 1.0, 0.0)
            n_ge0 = n_ge0 + jnp.where(sc >= 0.0, 1.0, 0.0)
            n_gt0 = n_gt0 + jnp.where(sc > 0.0, 1.0, 0.0)
        return lo, hi, n_fin, n_ge0, n_gt0
    zeros_l = jnp.zeros((tq, LANES), F32)
    lo_l, hi_l, fin_l, ge0_l, gt0_l = lax.fori_loop(
        0, n_cnt, stats_body,
        (jnp.full((tq, LANES), jnp.inf, F32), jnp.full((tq, LANES), -jnp.inf, F32),
         zeros_l, zeros_l, zeros_l))
    row_min = jnp.min(lo_l, axis=-1, keepdims=True)
    row_max = jnp.max(hi_l, axis=-1, keepdims=True)
    n_adm = lane_sum(fin_l)
    c_ge0 = lane_sum(ge0_l)
    c_gt0 = lane_sum(gt0_l)
    has_thr = n_adm >= kf
    c_max = count(row_max, False)
    at_max = c_max >= kf
    at_zero = (c_gt0 < kf) & (c_ge0 >= kf)
    above_zero = c_gt0 >= kf
    lo0 = jnp.where(at_max, row_max, jnp.where(at_zero | above_zero, 0.0, row_min))
    cnt0 = jnp.where(at_max, c_max, jnp.where(at_zero | above_zero, c_ge0, n_adm))
    hi0 = jnp.where(at_zero | above_zero, row_max, 0.0)
    done0 = jnp.logical_not(has_thr) | at_max | at_zero | (cnt0 == kf)

    def as_flag(done):
        return jnp.where(done, 1.0, 0.0)

    def bisect_cond(carry):
        return carry[3] < 0.5

    def bisect_body(carry):
        lo, hi, cnt_lo, _ = carry
        mid = 0.5 * lo + 0.5 * hi
        open_ = (mid > lo) & (mid < hi)
        c = count(mid, False)
        ge = c >= kf
        lo = jnp.where(ge, mid, lo)
        cnt_lo = jnp.where(ge, c, cnt_lo)
        hi = jnp.where(ge, hi, mid)
        conv = jnp.where(open_ & (cnt_lo != kf), pre_done, 1.0)
        return lo, hi, cnt_lo, jnp.min(conv)
    pre_done = as_flag(done0)
    thr, _, cnt_thr, _ = lax.while_loop(
        bisect_cond, bisect_body, (lo0, hi0, cnt0, jnp.min(pre_done)))

    tie = has_thr & (cnt_thr > kf)
    any_tie = jnp.max(jnp.where(tie, 1.0, 0.0)) > 0.0
    far_end = t0 - tq
    f32_min = float(jnp.finfo(F32).min)

    def write_mask(ks, sel):
        pos = ks + lax.broadcasted_iota(I32, (tq, LANES), 1)
        mb_ref[:, pl.ds(ks, LANES)] = jnp.where(sel, 0.0, NEG)
        mbf_ref[:, pl.ds(ks, LANES)] = jnp.where(sel & (pos < far_end), 0.0, NEG)

    @pl.when(jnp.logical_not(any_tie))
    def _():
        lo = jnp.broadcast_to(jnp.where(has_thr, thr, f32_min), (tq, LANES))

        def body(kt, _):
            ks = pl.multiple_of(kt * LANES, LANES)
            write_mask(ks, sc_ref[:, pl.ds(ks, LANES)] >= lo)
            return 0
        lax.fori_loop(0, n_sel * (S_TILE // LANES), body, 0)

    @pl.when(any_tie)
    def _():
        need = kf - count(thr, True)
        thr_b = jnp.broadcast_to(thr, (tq, LANES))
        lo = jnp.broadcast_to(jnp.where(has_thr, thr, -jnp.inf), (tq, LANES))
        eq_ok = jnp.broadcast_to(has_thr, (tq, LANES))
        ri = lax.broadcasted_iota(I32, (LANES, LANES), 0)
        ci = lax.broadcasted_iota(I32, (LANES, LANES), 1)
        tri = jnp.where(ri <= ci, 1.0, 0.0).astype(BF16)

        def body(kt, seen):
            ks = pl.multiple_of(kt * LANES, LANES)
            sc = sc_ref[:, pl.ds(ks, LANES)]
            eq = (sc == thr_b) & eq_ok
            prefix = seen + _dot(jnp.where(eq, 1.0, 0.0).astype(BF16), tri)
            write_mask(ks, (sc > lo) | (eq & (prefix <= need)))
            return prefix[:, LANES - 1:LANES]
        lax.fori_loop(0, n_sel * (S_TILE // LANES), body, jnp.zeros((tq, 1), F32))

    n_far = (j + 2) // 4
    ws = pl.multiple_of(jnp.maximum(t0 - tq, 0), tq)
    first = (j == 0).astype(I32)
    lane_o = lax.broadcasted_iota(I32, (tq, LANES), 1)

    def lane_group_max(s):
        m = s[:, :LANES]
        for g in range(1, s.shape[1] // LANES):
            m = jnp.maximum(m, s[:, g * LANES:(g + 1) * LANES])
        return m

    def pair_lanes(p, width=1):
        return slice(width * p * LANES, width * (p + 1) * LANES)

    mrow_ref[...] = jnp.full(mrow_ref.shape, NEG, F32)

    def logits_body(kt, _):
        ks = pl.multiple_of(kt * K_TILE, K_TILE)
        mbt = mbf_ref[:, pl.ds(ks, K_TILE)]
        mb2 = jnp.concatenate([mbt, mbt], axis=0)
        for p in range(n_pairs):
            s = _dot_nt(qp_ref[p], kb_ref[0, pl.ds(ks, K_TILE), pair_lanes(p)]) + mb2
            s_ref[p, :, pl.ds(ks, K_TILE)] = s
            mrow_ref[p] = jnp.maximum(mrow_ref[p], lane_group_max(s))
        return 0
    lax.fori_loop(0, n_far, logits_body, 0)

    mbw = mb_ref[:, pl.ds(ws, wwin)]
    mbw2 = jnp.concatenate([mbw, mbw], axis=0)
    for p in range(n_pairs):
        sw = _dot_nt(qp_ref[p], kb_ref[0, pl.ds(ws, wwin), pair_lanes(p)]) + mbw2 + biasw_ref[first, p]
        m_acc = jnp.maximum(mrow_ref[p], lane_group_max(sw))
        m_row = jnp.broadcast_to(jnp.max(m_acc, axis=-1, keepdims=True), (2 * tq, LANES))
        mrow_ref[p] = m_row
        pw = jnp.exp(sw - jnp.concatenate([m_row] * (wwin // LANES), axis=1)).astype(BF16)
        acc_ref[p] = _dot(pw, vaug_ref[pl.ds(ws, wwin), pair_lanes(p, 2)])

    def pv_body(kt, _):
        ks = pl.multiple_of(kt * K_TILE, K_TILE)
        for p in range(n_pairs):
            m_far = jnp.concatenate([mrow_ref[p]] * (K_TILE // LANES), axis=1)
            pexp = jnp.exp(s_ref[p, :, pl.ds(ks, K_TILE)] - m_far).astype(BF16)
            acc_ref[p] += _dot(pexp, vaug_ref[pl.ds(ks, K_TILE), pair_lanes(p, 2)])
        return 0
    lax.fori_loop(0, n_far, pv_body, 0)

    for p in range(n_pairs):
        a = acc_ref[p]
        o_even = a[:tq, :LANES] / a[:tq, LANES:LANES + 1]
        o_odd = a[tq:, :LANES] / a[tq:, LANES:LANES + 1]
        o_ref[:, pair_lanes(p)] = jnp.where(lane_o < ATT_DH, o_even, o_odd).astype(BF16)

    yb = _dot(o_ref[...], wbr_ref[...])
    mixed = jax.nn.sigmoid(gb_ref[...].astype(F32)) * yb + ya_ref[...].astype(F32)
    out_ref[...] = mixed.astype(BF16)


def _old_dsa_mixer(p, ya, rel_bias, w_br_att, batch, seq):
    n = batch * seq
    d = w_br_att.shape[1]
    tq = Q_TILE
    nq = seq // tq
    k_top = min(TOPK_MAX, seq // 4)
    kb3 = p["kb"].reshape(batch, seq, ATT_W)
    vb3 = p["vb"].reshape(batch, seq, ATT_W)
    kit = p["sm"][:, SM_KI:SM_KI + IDX_DH].astype(BF16).reshape(batch, seq, IDX_DH).swapaxes(1, 2)
    row = lambda b, j: (b * nq + j, 0)
    per_b = lambda b, j: (b, 0, 0)
    const = lambda b, j: (0, 0)
    return pl.pallas_call(
        functools.partial(_dsa_kernel, k_top=k_top, seq=seq),
        grid=(batch, nq),
        in_specs=[pl.BlockSpec(memory_space=pltpu.SMEM),
                  pl.BlockSpec((tq, ATT_W), row), pl.BlockSpec((tq, IDX_Q_W), row),
                  pl.BlockSpec((tq, LANES), row), pl.BlockSpec((tq, d), row),
                  pl.BlockSpec((tq, d), row),
                  pl.BlockSpec((1, seq, ATT_W), per_b, pipeline_mode=pl.Buffered(1)),
                  pl.BlockSpec((1, seq, ATT_W), per_b, pipeline_mode=pl.Buffered(1)),
                  pl.BlockSpec((1, IDX_DH, seq), per_b, pipeline_mode=pl.Buffered(1)),
                  pl.BlockSpec((ATT_W, d), const)],
        out_specs=pl.BlockSpec((tq, d), row),
        out_shape=jax.ShapeDtypeStruct((n, d), BF16),
        scratch_shapes=[pltpu.VMEM((tq, seq), F32),
                        pltpu.VMEM((tq, seq), F32),
                        pltpu.VMEM((tq, seq), F32),
                        pltpu.VMEM((seq, 2 * ATT_W), BF16),
                        pltpu.VMEM((2, ATT_HEADS // 2, 2 * tq, 2 * tq), F32),
                        pltpu.VMEM((ATT_HEADS // 2, 2 * tq, LANES), BF16),
                        pltpu.VMEM((IDX_HEADS * tq, IDX_DH), BF16),
                        pltpu.VMEM((ATT_HEADS // 2, 2 * tq, seq), F32),
                        pltpu.VMEM((ATT_HEADS // 2, 2 * tq, LANES), F32),
                        pltpu.VMEM((ATT_HEADS // 2, 2 * tq, 2 * LANES), F32),
                        pltpu.VMEM((tq, ATT_W), BF16)],
        compiler_params=pltpu.CompilerParams(dimension_semantics=("arbitrary", "arbitrary"),
                                             vmem_limit_bytes=VMEM_LIMIT),
        name="dsa_mixer",
    )(rel_bias.astype(F32).reshape(-1), p["qb"], p["qi"], p["sm"], p["gb"], ya, kb3, vb3, kit,
      w_br_att.astype(BF16))


def _dsa_kernel(relb_ref, qit_ref, wit_ref, qbt_ref, gb_ref, ya_ref, kb_ref, vbt_ref, ki_ref,
                wbr_ref, out_ref, sc_ref, mb_ref, mbf_ref, bias_ref, qp_ref, s_ref, mrow_ref,
                acc_ref, ot_ref, *, k_top):
    b = pl.program_id(0)
    j = pl.program_id(1)
    tq = Q_TILE
    t0 = j * tq
    n_pairs = ATT_HEADS // 2
    wwin = 2 * tq
    sub = 8

    @pl.when((b == 0) & (j == 0))
    def _():
        rk = lax.broadcasted_iota(I32, (wwin, tq), 0)
        rq = lax.broadcasted_iota(I32, (wwin, tq), 1)
        bucket = _t5_bucket_int(rk - tq - rq)
        far = REL_BUCKETS // 2 - 1
        for h in range(ATT_HEADS):
            a = jnp.zeros((wwin, tq), F32)
            for bk in range(REL_BUCKETS):
                a = jnp.where(bucket == bk, relb_ref[bk * ATT_HEADS + h], a)
            a = a - relb_ref[far * ATT_HEADS + h]
            cols = slice((h % 2) * tq, (h % 2 + 1) * tq)
            bias_ref[0, h // 2, :, cols] = a
            bias_ref[1, h // 2, :, cols] = jnp.concatenate([a[tq:], jnp.zeros((tq, tq), F32)], axis=0)

    row_q = lax.broadcasted_iota(I32, (LANES, tq), 0)
    for p in range(n_pairs):
        qpair = qbt_ref[0, p * LANES:(p + 1) * LANES, :]
        zero = jnp.zeros_like(qpair)
        qp_ref[p] = jnp.concatenate([jnp.where(row_q < ATT_DH, qpair, zero),
                                     jnp.where(row_q >= ATT_DH, qpair, zero)], axis=1)
    wi = wit_ref[0] * (IDX_HEADS ** -0.5 * IDX_DH ** -0.5)

    n_sel = (j + 2) // 2
    row_s = lax.broadcasted_iota(I32, (S_TILE, tq), 0)
    lane_s = lax.broadcasted_iota(I32, (S_TILE, tq), 1)
    limit = t0 + ((lane_s >> 6) + 1) * CHUNK

    def score_body(kt, _):
        ks = pl.multiple_of(kt * S_TILE, S_TILE)
        s_all = _dot(ki_ref[0, pl.ds(ks, S_TILE), :], qit_ref[0])
        score = jnp.zeros((S_TILE, tq), F32)
        for h in range(IDX_HEADS):
            score = score + jnp.maximum(s_all[:, h * tq:(h + 1) * tq], 0.0) * wi[h:h + 1, :]
        sc_ref[pl.ds(ks, S_TILE), :] = jnp.where(ks + row_s < limit, score, -jnp.inf)
        return 0
    lax.fori_loop(0, n_sel, score_body, 0)

    n_cnt = (n_sel + 1) // 2

    @pl.when(n_sel % 2 == 1)
    def _():
        sc_ref[pl.ds(pl.multiple_of(n_sel * S_TILE, S_TILE), S_TILE), :] = jnp.full((S_TILE, tq), -jnp.inf, F32)

    kf = float(k_top)
    grp_rows = 64
    c_groups = C_TILE // grp_rows

    def score_group(kt, g):
        return sc_ref[pl.ds(pl.multiple_of(kt * C_TILE + g * grp_rows, grp_rows), grp_rows), :]

    def key_sum(c):
        return jnp.sum(c, axis=0, keepdims=True)

    def count(cand, strict):
        def body(kt, c):
            for g in range(c_groups):
                sc = score_group(kt, g)
                c = c + jnp.where((sc > cand) if strict else (sc >= cand), 1.0, 0.0)
            return c
        return key_sum(lax.fori_loop(0, n_cnt, body, jnp.zeros((grp_rows, tq), F32)))

    def stats_body(kt, carry):
        lo, hi, n_fin, n_ge0, n_gt0 = carry
        for g in range(c_groups):
            sc = score_group(kt, g)
            fin = sc > -jnp.inf
            hi = jnp.maximum(hi, sc)
            lo = jnp.minimum(lo, jnp.where(fin, sc, jnp.inf))
            n_fin = n_fin + jnp.where(fin, 1.0, 0.0)
            n_ge0 = n_ge0 + jnp.where(sc >= 0.0, 1.0, 0.0)
            n_gt0 = n_gt0 + jnp.where(sc > 0.0, 1.0, 0.0)
        return lo, hi, n_fin, n_ge0, n_gt0
    zeros_g = jnp.zeros((grp_rows, tq), F32)
    lo_g, hi_g, fin_g, ge0_g, gt0_g = lax.fori_loop(
        0, n_cnt, stats_body,
        (jnp.full((grp_rows, tq), jnp.inf, F32), jnp.full((grp_rows, tq), -jnp.inf, F32),
         zeros_g, zeros_g, zeros_g))
    row_min = jnp.min(lo_g, axis=0, keepdims=True)
    row_max = jnp.max(hi_g, axis=0, keepdims=True)
    n_adm = key_sum(fin_g)
    c_ge0 = key_sum(ge0_g)
    c_gt0 = key_sum(gt0_g)
    has_thr = n_adm >= kf
    c_max = count(row_max, False)
    at_max = c_max >= kf
    at_zero = (c_gt0 < kf) & (c_ge0 >= kf)
    above_zero = c_gt0 >= kf
    lo0 = jnp.where(at_max, row_max, jnp.where(at_zero | above_zero, 0.0, row_min))
    cnt0 = jnp.where(at_max, c_max, jnp.where(at_zero | above_zero, c_ge0, n_adm))
    hi0 = jnp.where(at_zero | above_zero, row_max, 0.0)
    done0 = jnp.logical_not(has_thr) | at_max | at_zero | (cnt0 == kf)
    pre_done = jnp.where(done0, 1.0, 0.0)

    def bisect_step(lo, hi, cnt_lo):
        mid = 0.5 * lo + 0.5 * hi
        open_ = (mid > lo) & (mid < hi)
        c = count(mid, False)
        ge = c >= kf
        lo = jnp.where(ge, mid, lo)
        cnt_lo = jnp.where(ge, c, cnt_lo)
        hi = jnp.where(ge, hi, mid)
        return lo, hi, cnt_lo, jnp.where(open_ & (cnt_lo != kf), pre_done, 1.0)

    def bisect_cond(carry):
        return carry[3] < 0.5

    def bisect_body(carry):
        lo, hi, cnt_lo, _ = carry
        for _ in range(BISECT_STEPS):
            lo, hi, cnt_lo, conv = bisect_step(lo, hi, cnt_lo)
        return lo, hi, cnt_lo, jnp.min(conv)
    thr, _, cnt_thr, _ = lax.while_loop(
        bisect_cond, bisect_body, (lo0, hi0, cnt0, jnp.min(pre_done)))

    tie = has_thr & (cnt_thr > kf)
    any_tie = jnp.max(jnp.where(tie, 1.0, 0.0)) > 0.0
    far_end = t0 - tq
    f32_min = float(jnp.finfo(F32).min)

    def write_mask(ks, sel):
        pos = ks + lax.broadcasted_iota(I32, (LANES, tq), 0)
        mb_ref[pl.ds(ks, LANES), :] = jnp.where(sel, 0.0, NEG)
        mbf_ref[pl.ds(ks, LANES), :] = jnp.where(sel & (pos < far_end), 0.0, NEG)

    @pl.when(jnp.logical_not(any_tie))
    def _():
        lo = jnp.where(has_thr, thr, f32_min)

        def body(kt, _):
            ks = pl.multiple_of(kt * LANES, LANES)
            write_mask(ks, sc_ref[pl.ds(ks, LANES), :] >= lo)
            return 0
        lax.fori_loop(0, n_sel * (S_TILE // LANES), body, 0)

    @pl.when(any_tie)
    def _():
        need = kf - count(thr, True)
        lo = jnp.where(has_thr, thr, -jnp.inf)
        ri = lax.broadcasted_iota(I32, (LANES, LANES), 0)
        ci = lax.broadcasted_iota(I32, (LANES, LANES), 1)
        tri = jnp.where(ci <= ri, 1.0, 0.0).astype(BF16)

        def body(kt, seen):
            ks = pl.multiple_of(kt * LANES, LANES)
            sc = sc_ref[pl.ds(ks, LANES), :]
            eq = (sc == thr) & has_thr
            prefix = seen + _dot(tri, jnp.where(eq, 1.0, 0.0).astype(BF16))
            write_mask(ks, (sc > lo) | (eq & (prefix <= need)))
            return prefix[LANES - 1:LANES, :]
        lax.fori_loop(0, n_sel * (S_TILE // LANES), body, jnp.zeros((1, tq), F32))

    n_far = (j + 2) // 4
    ws = pl.multiple_of(jnp.maximum(t0 - tq, 0), tq)
    first = (j == 0).astype(I32)
    ones_rows = 16

    def pair_lanes(p):
        return slice(p * LANES, (p + 1) * LANES)

    def group_max(s):
        return jnp.max(s.reshape(s.shape[0] // sub, sub, s.shape[1]), axis=0)

    def values_t(p, ks, width):
        return jnp.concatenate([vbt_ref[0, pair_lanes(p), pl.ds(ks, width)],
                                jnp.ones((ones_rows, width), BF16)], axis=0)

    mrow_ref[...] = jnp.full(mrow_ref.shape, NEG, F32)

    def logits_body(kt, _):
        ks = pl.multiple_of(kt * K_TILE, K_TILE)
        mbt = mbf_ref[pl.ds(ks, K_TILE), :]
        mb2 = jnp.concatenate([mbt, mbt], axis=1)
        for p in range(n_pairs):
            s = _dot(kb_ref[0, pl.ds(ks, K_TILE), pair_lanes(p)], qp_ref[p]) + mb2
            s_ref[p, pl.ds(ks, K_TILE), :] = s
            mrow_ref[p] = jnp.maximum(mrow_ref[p], group_max(s))
        return 0
    lax.fori_loop(0, n_far, logits_body, 0)

    mbw = mb_ref[pl.ds(ws, wwin), :]
    mbw2 = jnp.concatenate([mbw, mbw], axis=1)
    for p in range(n_pairs):
        sw = _dot(kb_ref[0, pl.ds(ws, wwin), pair_lanes(p)], qp_ref[p]) + mbw2 + bias_ref[first, p]
        m8 = jnp.maximum(mrow_ref[p], group_max(sw))
        m1 = jnp.max(m8, axis=0, keepdims=True)
        mrow_ref[p] = jnp.broadcast_to(m1, (sub, 2 * tq))
        pw = jnp.exp(sw - m1).astype(BF16)
        acc_ref[p] = _dot(values_t(p, ws, wwin), pw)

    def pv_body(kt, _):
        ks = pl.multiple_of(kt * K_TILE, K_TILE)
        for p in range(n_pairs):
            pexp = jnp.exp(s_ref[p, pl.ds(ks, K_TILE), :] - mrow_ref[p, 0:1, :]).astype(BF16)
            acc_ref[p] += _dot(values_t(p, ks, K_TILE), pexp)
        return 0
    lax.fori_loop(0, n_far, pv_body, 0)

    row_o = lax.broadcasted_iota(I32, (LANES, tq), 0)
    for p in range(n_pairs):
        a = acc_ref[p]
        o = a[:LANES, :] / a[LANES:LANES + 1, :]
        ot_ref[pair_lanes(p), :] = jnp.where(row_o < ATT_DH, o[:, :tq], o[:, tq:]).astype(BF16)

    yb = _dot_tn(ot_ref[...], wbr_ref[...])
    mixed = jax.nn.sigmoid(gb_ref[...].astype(F32)) * yb + ya_ref[...].astype(F32)
    out_ref[...] = mixed.astype(BF16)


def _dsa_mixer(p, ya, rel_bias, w_br_att, batch, seq):
    n = batch * seq
    d = w_br_att.shape[1]
    tq = Q_TILE
    nq = seq // tq
    nb = n // tq
    k_top = min(TOPK_MAX, seq // 4)
    qit = (p["qi"].reshape(nb, tq, IDX_HEADS, IDX_DH).transpose(0, 3, 2, 1)
           .reshape(nb, IDX_DH, IDX_HEADS * tq))
    wit = p["sm"][:, SM_WI:SM_WI + IDX_HEADS].reshape(nb, tq, IDX_HEADS).swapaxes(1, 2)
    qbt = p["qb"].reshape(nb, tq, ATT_W).swapaxes(1, 2)
    kb3 = p["kb"].reshape(batch, seq, ATT_W)
    vbt = p["vb"].reshape(batch, seq, ATT_W).swapaxes(1, 2)
    ki3 = p["sm"][:, SM_KI:SM_KI + IDX_DH].astype(BF16).reshape(batch, seq, IDX_DH)
    blk = lambda b, j: (b * nq + j, 0, 0)
    row = lambda b, j: (b * nq + j, 0)
    per_b = lambda b, j: (b, 0, 0)
    const = lambda b, j: (0, 0)
    once = pl.Buffered(1)
    return pl.pallas_call(
        functools.partial(_dsa_kernel, k_top=k_top),
        grid=(batch, nq),
        in_specs=[pl.BlockSpec(memory_space=pltpu.SMEM),
                  pl.BlockSpec((1, IDX_DH, IDX_HEADS * tq), blk),
                  pl.BlockSpec((1, IDX_HEADS, tq), blk),
                  pl.BlockSpec((1, ATT_W, tq), blk),
                  pl.BlockSpec((tq, d), row), pl.BlockSpec((tq, d), row),
                  pl.BlockSpec((1, seq, ATT_W), per_b, pipeline_mode=once),
                  pl.BlockSpec((1, ATT_W, seq), per_b, pipeline_mode=once),
                  pl.BlockSpec((1, seq, IDX_DH), per_b, pipeline_mode=once),
                  pl.BlockSpec((ATT_W, d), const)],
        out_specs=pl.BlockSpec((tq, d), row),
        out_shape=jax.ShapeDtypeStruct((n, d), BF16),
        scratch_shapes=[pltpu.VMEM((seq, tq), F32),
                        pltpu.VMEM((seq, tq), F32),
                        pltpu.VMEM((seq, tq), F32),
                        pltpu.VMEM((2, ATT_HEADS // 2, 2 * tq, 2 * tq), F32),
                        pltpu.VMEM((ATT_HEADS // 2, LANES, 2 * tq), BF16),
                        pltpu.VMEM((ATT_HEADS // 2, seq, 2 * tq), F32),
                        pltpu.VMEM((ATT_HEADS // 2, 8, 2 * tq), F32),
                        pltpu.VMEM((ATT_HEADS // 2, LANES + 16, 2 * tq), F32),
                        pltpu.VMEM((ATT_W, tq), BF16)],
        compiler_params=pltpu.CompilerParams(dimension_semantics=("arbitrary", "arbitrary"),
                                             vmem_limit_bytes=VMEM_LIMIT),
        name="dsa_mixer",
    )(rel_bias.astype(F32).reshape(-1), qit, wit, qbt, p["gb"], ya, kb3, vbt, ki3,
      w_br_att.astype(BF16))


RT_G0 = 0
RT_E0 = N_GROUPS


def _moe_kernel(x_ref, mx_ref, wo_ref, gf_ref, wrh_ref, wrl_ref, br_ref, wg_ref, wu_ref,
                wd_ref, out_ref, x1_ref, h2_ref, comb_ref, y_ref):
    g = pl.program_id(1)
    tm = x_ref.shape[0]

    @pl.when(g == 0)
    def _():
        x1 = x_ref[...] + _dot(mx_ref[...], wo_ref[...])
        x1_ref[...] = x1
        ms = jnp.mean(x1 * x1, axis=-1, keepdims=True)
        h = x1 * lax.rsqrt(ms + EPS) * gf_ref[...]
        h_hi, h_lo = _split_bf16(h)
        h2_ref[...] = h_hi
        logits = (_dot(h_hi, wrh_ref[...]) + _dot(h_lo, wrh_ref[...]) + _dot(h_hi, wrl_ref[...])
                  + br_ref[...])
        lane = lax.broadcasted_iota(I32, (tm, LANES), 1)
        big = jnp.int32(LANES)
        is_g = lane < N_GROUPS
        gl = jnp.where(is_g, logits, -jnp.inf)
        gmax = jnp.max(gl, axis=-1, keepdims=True)
        gidx = jnp.min(jnp.where(gl == gmax, lane, big), axis=-1, keepdims=True)
        gsum = jnp.sum(jnp.where(is_g, jnp.exp(gl - gmax), 0.0), axis=-1, keepdims=True)
        g_w = 1.0 / gsum
        in_grp = (lane >= RT_E0) & (lane < RT_E0 + N_EXPERTS) & (((lane - RT_E0) >> 3) == gidx)
        ev = jnp.where(in_grp, logits, -jnp.inf)
        t1 = jnp.max(ev, axis=-1, keepdims=True)
        i1 = jnp.min(jnp.where(ev == t1, lane, big), axis=-1, keepdims=True)
        ev2 = jnp.where(lane == i1, -jnp.inf, ev)
        t2 = jnp.max(ev2, axis=-1, keepdims=True)
        i2 = jnp.min(jnp.where(ev2 == t2, lane, big), axis=-1, keepdims=True)
        e2 = jnp.exp(t2 - t1)
        w1 = 1.0 / (1.0 + e2)
        w2 = e2 * w1
        comb = g_w * (jnp.where(lane == i1, w1, 0.0) + jnp.where(lane == i2, w2, 0.0))
        for grp in range(N_GROUPS):
            e0 = RT_E0 + grp * EXPERTS_PER_GROUP
            comb_ref[grp] = comb[:, e0:e0 + EXPERTS_PER_GROUP]
        y_ref[...] = jnp.zeros_like(y_ref)

    h2 = h2_ref[...]
    cw = comb_ref[g]
    for c in range(EXPERTS_PER_GROUP * D_EXPERT // MOE_COLS):
        cols = slice(c * MOE_COLS, (c + 1) * MOE_COLS)
        hg = _dot(h2, wg_ref[0, :, cols])
        hu = _dot(h2, wu_ref[0, :, cols])
        e_first = c * (MOE_COLS // D_EXPERT)
        scale = jnp.concatenate(
            [jnp.broadcast_to(cw[:, e:e + 1], (tm, D_EXPERT))
             for e in range(e_first, e_first + MOE_COLS // D_EXPERT)], axis=1)
        hid = (hg * jax.nn.sigmoid(hg) * hu * scale).astype(BF16)
        y_ref[...] += _dot(hid, wd_ref[0, cols, :])

    @pl.when(g == N_GROUPS - 1)
    def _():
        out_ref[...] = x1_ref[...] + y_ref[...]


def _out_proj_moe(x2, mixed, w_out, g_ffn, w_rg, b_rg, w_re, b_re, w_gate, w_up, w_down):
    n, d = x2.shape
    tm = ROW_TILE
    gw = EXPERTS_PER_GROUP * D_EXPERT
    w_r = jnp.zeros((d, LANES), F32).at[:, RT_G0:RT_G0 + N_GROUPS].set(w_rg)
    w_r = w_r.at[:, RT_E0:RT_E0 + N_EXPERTS].set(w_re)
    wr_hi = w_r.astype(BF16)
    wr_lo = (w_r - wr_hi.astype(F32)).astype(BF16)
    b_r = jnp.zeros((1, LANES), F32).at[0, RT_G0:RT_G0 + N_GROUPS].set(b_rg)
    b_r = b_r.at[0, RT_E0:RT_E0 + N_EXPERTS].set(b_re)

    def by_group(w):
        return (w.reshape(N_GROUPS, EXPERTS_PER_GROUP, d, D_EXPERT).transpose(0, 2, 1, 3)
                .reshape(N_GROUPS, d, gw).astype(BF16))
    wg = by_group(w_gate)
    wu = by_group(w_up)
    wd = w_down.reshape(N_GROUPS, gw, d).astype(BF16)

    row = lambda i, g: (i, 0)
    const = lambda i, g: (0, 0)
    grp3 = lambda i, g: (g, 0, 0)
    return pl.pallas_call(
        _moe_kernel,
        grid=(n // tm, N_GROUPS),
        in_specs=[pl.BlockSpec((tm, d), row), pl.BlockSpec((tm, d), row),
                  pl.BlockSpec((d, d), const), pl.BlockSpec((1, d), const),
                  pl.BlockSpec((d, LANES), const), pl.BlockSpec((d, LANES), const),
                  pl.BlockSpec((1, LANES), const),
                  pl.BlockSpec((1, d, gw), grp3), pl.BlockSpec((1, d, gw), grp3),
                  pl.BlockSpec((1, gw, d), grp3)],
        out_specs=pl.BlockSpec((tm, d), row),
        out_shape=jax.ShapeDtypeStruct((n, d), F32),
        scratch_shapes=[pltpu.VMEM((tm, d), F32), pltpu.VMEM((tm, d), BF16),
                        pltpu.VMEM((N_GROUPS, tm, EXPERTS_PER_GROUP), F32),
                        pltpu.VMEM((tm, d), F32)],
        compiler_params=pltpu.CompilerParams(dimension_semantics=("arbitrary", "arbitrary"),
                                             vmem_limit_bytes=VMEM_LIMIT),
        name="out_proj_moe",
    )(x2, mixed, w_out.astype(BF16), g_ffn[None, :].astype(F32), wr_hi, wr_lo, b_r,
      wg, wu, wd)


def kernel(x, g_mix, w_in, w_alpha2, b_alpha, g_gla, w_br_gla, g_q, g_k, rel_bias, w_br_att, w_out, g_ffn, w_rg, b_rg, w_re, b_re, w_gate, w_up, w_down):
    batch, seq, d = x.shape
    assert seq % ROW_TILE == 0 and seq % K_TILE == 0 and (batch * seq) % ROW_TILE == 0
    x2 = x.reshape(batch * seq, d)
    for l in range(g_mix.shape[0]):
        p = _in_projection(x2, g_mix[l], w_in[l], g_q[l], g_k[l])
        ya = _gla_mixer(p, w_alpha2[l], b_alpha[l], g_gla[l], w_br_gla[l], batch, seq)
        mixed = _dsa_mixer(p, ya, rel_bias, w_br_att[l], batch, seq)
        x2 = _out_proj_moe(x2, mixed, w_out[l], g_ffn[l], w_rg[l], b_rg[l], w_re[l], b_re[l],
                           w_gate[l], w_up[l], w_down[l])
    return x2.reshape(batch, seq, d)
```

```python
import functools
import math

import jax
import jax.numpy as jnp
from jax import lax
from jax.experimental import pallas as pl
from jax.experimental.pallas import tpu as pltpu

F32 = jnp.float32
BF16 = jnp.bfloat16
I32 = jnp.int32

CHUNK = 64
GLA_HEADS = 4
GLA_DK = 64
GLA_DV = 128
GLA_GATE_RANK = 16
GLA_TAU = 16.0
ATT_HEADS = 8
ATT_DH = 64
IDX_HEADS = 8
IDX_DH = 32
TOPK_MAX = 256
REL_BUCKETS = 32
REL_MAX_DIST = 128
N_GROUPS = 4
EXPERTS_PER_GROUP = 8
N_EXPERTS = N_GROUPS * EXPERTS_PER_GROUP
D_EXPERT = 128
EPS = 1e-6

GLA_QK_W = GLA_HEADS * GLA_DK
GLA_V_W = GLA_HEADS * GLA_DV
ATT_W = ATT_HEADS * ATT_DH
IDX_Q_W = IDX_HEADS * IDX_DH

LANES = 128
VMEM_LIMIT = 56 * 1024 * 1024

SM_ALR = 0
SM_KI = GLA_GATE_RANK
SM_WI = GLA_GATE_RANK + IDX_DH

NEG = -1e30

ROW_TILE = 512
Q_TILE = 128
K_TILE = 512
MOE_COLS = 256
S_TILE = 256
C_TILE = 2 * S_TILE
BISECT_STEPS = 2


def _dot(a, b):
    return jnp.dot(a, b, preferred_element_type=F32)


def _dot_nt(a, b):
    return lax.dot_general(a, b, (((1,), (1,)), ((), ())), preferred_element_type=F32)


def _dot_tn(a, b):
    return lax.dot_general(a, b, (((0,), (0,)), ((), ())), preferred_element_type=F32)


def _split_bf16(a):
    hi = a.astype(BF16)
    lo = (a - hi.astype(F32)).astype(BF16)
    return hi, lo


_PROJ = (("qa", GLA_QK_W), ("ka", GLA_QK_W), ("va", GLA_V_W), ("ra", GLA_V_W),
         ("qb", ATT_W), ("kb", ATT_W), ("vb", ATT_W), ("qi", IDX_Q_W),
         ("ga", None), ("gb", None), ("sm", LANES))


def _inproj_kernel(x_ref, g_ref, w_ref, gq_ref, gk_ref, bd_ref, *out_refs, offs):
    x = x_ref[...]
    ms = jnp.mean(x * x, axis=-1, keepdims=True)
    hn = (x * lax.rsqrt(ms + EPS) * g_ref[...]).astype(BF16)

    def proj(name):
        c0, c1 = offs[name]
        return _dot(hn, w_ref[:, c0:c1])

    def head_norm(y, gain):
        ss = _dot((y * y).astype(BF16), bd_ref[...])
        return y * lax.rsqrt(ss * (1.0 / ATT_DH) + EPS) * gain

    qa_ref, ka_ref, va_ref, ra_ref, qb_ref, kb_ref, vb_ref, qi_ref, ga_ref, gb_ref, sm_ref = out_refs
    qa_ref[...] = (proj("qa") * (GLA_DK ** -0.5)).astype(BF16)
    ka_ref[...] = proj("ka").astype(BF16)
    va_ref[...] = proj("va").astype(BF16)
    ra_ref[...] = proj("ra").astype(BF16)
    qb_ref[...] = (head_norm(proj("qb"), gq_ref[...]) * (ATT_DH ** -0.5)).astype(BF16)
    kb_ref[...] = head_norm(proj("kb"), gk_ref[...]).astype(BF16)
    vb_ref[...] = proj("vb").astype(BF16)
    qi_ref[...] = proj("qi").astype(BF16)
    ga_ref[...] = proj("ga").astype(BF16)
    gb_ref[...] = proj("gb").astype(BF16)
    sm_ref[...] = proj("sm")


def _in_projection(x2, g_mix, w_in, g_q, g_k):
    n, d = x2.shape
    widths = {name: (d if w is None else w) for name, w in _PROJ}
    ref_order = (("qa", GLA_QK_W), ("ka", GLA_QK_W), ("va", GLA_V_W), ("ra", GLA_V_W),
                 ("alr", GLA_GATE_RANK), ("qb", ATT_W), ("kb", ATT_W), ("vb", ATT_W),
                 ("qi", IDX_Q_W), ("ki", IDX_DH), ("wi", IDX_HEADS), ("ga", d), ("gb", d))
    cols, c = {}, 0
    for name, w in ref_order:
        cols[name] = w_in[:, c:c + w]
        c += w
    small = jnp.concatenate(
        [cols["alr"], cols["ki"], cols["wi"],
         jnp.zeros((d, LANES - GLA_GATE_RANK - IDX_DH - IDX_HEADS), w_in.dtype)], axis=1)
    cols["sm"] = small
    w_cat = jnp.concatenate([cols[name] for name, _ in _PROJ], axis=1).astype(BF16)
    offs, c = {}, 0
    for name, _ in _PROJ:
        offs[name] = (c, c + widths[name])
        c += widths[name]
    d_cat = c

    head = jnp.arange(ATT_W) // ATT_DH
    blockdiag = (head[:, None] == head[None, :]).astype(BF16)
    gq = jnp.tile(g_q, ATT_HEADS)[None, :].astype(F32)
    gk = jnp.tile(g_k, ATT_HEADS)[None, :].astype(F32)

    tm = ROW_TILE
    const = lambda i: (0, 0)
    out_shape = [jax.ShapeDtypeStruct((n, widths[name]), F32 if name == "sm" else BF16)
                 for name, _ in _PROJ]
    out_specs = [pl.BlockSpec((tm, widths[name]), lambda i: (i, 0)) for name, _ in _PROJ]
    outs = pl.pallas_call(
        functools.partial(_inproj_kernel, offs=offs),
        grid=(n // tm,),
        in_specs=[pl.BlockSpec((tm, d), lambda i: (i, 0)),
                  pl.BlockSpec((1, d), const),
                  pl.BlockSpec((d, d_cat), const),
                  pl.BlockSpec((1, ATT_W), const),
                  pl.BlockSpec((1, ATT_W), const),
                  pl.BlockSpec((ATT_W, ATT_W), const)],
        out_specs=out_specs,
        out_shape=out_shape,
        compiler_params=pltpu.CompilerParams(dimension_semantics=("arbitrary",),
                                             vmem_limit_bytes=VMEM_LIMIT),
        name="in_projection",
    )(x2, g_mix[None, :].astype(F32), w_cat, gq, gk, blockdiag)
    return dict(zip([name for name, _ in _PROJ], outs))


def _gla_kernel(qa_ref, ka_ref, va_ref, ra_ref, sm_ref, ga_ref, wa2_ref, ba_ref, ltri_ref,
                lall_ref, gg_ref, wbr_ref, out_ref, st_ref, o_ref):
    @pl.when(pl.program_id(1) == 0)
    def _():
        st_ref[...] = jnp.zeros_like(st_ref)

    tb = qa_ref.shape[0]
    z = _dot(sm_ref[...].astype(BF16), wa2_ref[...]) + ba_ref[...]
    log_a = (jnp.minimum(z, 0.0) - jnp.log1p(jnp.exp(-jnp.abs(z)))) * (1.0 / GLA_TAU)
    la_hi, la_lo = _split_bf16(log_a)
    ltri = ltri_ref[...]
    lall = lall_ref[...]
    cum = _dot(ltri, la_hi) + _dot(ltri, la_lo)
    tot = _dot(lall, la_hi) + _dot(lall, la_lo)
    k_dec = (ka_ref[...].astype(F32) * jnp.exp(tot - cum)).astype(BF16)
    dec = jnp.exp(tot)

    states = [st_ref[h] for h in range(GLA_HEADS)]
    for c in range(tb // CHUNK):
        rows = slice(c * CHUNK, (c + 1) * CHUNK)
        for h in range(GLA_HEADS):
            kl = slice(h * GLA_DK, (h + 1) * GLA_DK)
            vl = slice(h * GLA_DV, (h + 1) * GLA_DV)
            u_t = _dot_tn(va_ref[rows, vl], k_dec[rows, kl])
            s = states[h] * dec[c * CHUNK:c * CHUNK + 1, kl] + u_t
            states[h] = s
            o_ref[rows, vl] = _dot_nt(qa_ref[rows, kl], s.astype(BF16))
    for h in range(GLA_HEADS):
        st_ref[h] = states[h]

    r = ra_ref[...].astype(F32)
    gated = []
    for h in range(GLA_HEADS):
        vl = slice(h * GLA_DV, (h + 1) * GLA_DV)
        oh = o_ref[:, vl]
        ms = jnp.mean(oh * oh, axis=-1, keepdims=True)
        oh = oh * lax.rsqrt(ms + EPS) * gg_ref[...]
        rh = r[:, vl]
        gated.append((oh * (rh * jax.nn.sigmoid(rh))).astype(BF16))
    og = jnp.concatenate(gated, axis=1)
    ya = _dot(og, wbr_ref[...])
    out_ref[...] = (jax.nn.sigmoid(ga_ref[...].astype(F32)) * ya).astype(BF16)


def _gla_mixer(p, w_alpha2, b_alpha, g_gla, w_br_gla, batch, seq):
    n = batch * seq
    d = w_br_gla.shape[1]
    tb = ROW_TILE
    nt = seq // tb
    wa2 = jnp.zeros((LANES, GLA_QK_W), F32).at[SM_ALR:SM_ALR + GLA_GATE_RANK].set(w_alpha2).astype(BF16)
    r = jnp.arange(tb)
    same = (r[:, None] // CHUNK) == (r[None, :] // CHUNK)
    ltri = (same & (r[None, :] <= r[:, None])).astype(BF16)
    lall = same.astype(BF16)
    row = lambda b, i: (b * nt + i, 0)
    const = lambda b, i: (0, 0)
    return pl.pallas_call(
        _gla_kernel,
        grid=(batch, nt),
        in_specs=[pl.BlockSpec((tb, GLA_QK_W), row), pl.BlockSpec((tb, GLA_QK_W), row),
                  pl.BlockSpec((tb, GLA_V_W), row), pl.BlockSpec((tb, GLA_V_W), row),
                  pl.BlockSpec((tb, LANES), row), pl.BlockSpec((tb, d), row),
                  pl.BlockSpec((LANES, GLA_QK_W), const), pl.BlockSpec((1, GLA_QK_W), const),
                  pl.BlockSpec((tb, tb), const), pl.BlockSpec((tb, tb), const),
                  pl.BlockSpec((1, GLA_DV), const), pl.BlockSpec((GLA_V_W, d), const)],
        out_specs=pl.BlockSpec((tb, d), row),
        out_shape=jax.ShapeDtypeStruct((n, d), BF16),
        scratch_shapes=[pltpu.VMEM((GLA_HEADS, GLA_DV, GLA_DK), F32),
                        pltpu.VMEM((tb, GLA_V_W), F32)],
        compiler_params=pltpu.CompilerParams(dimension_semantics=("arbitrary", "arbitrary"),
                                             vmem_limit_bytes=VMEM_LIMIT),
        name="gla_mixer",
    )(p["qa"], p["ka"], p["va"], p["ra"], p["sm"], p["ga"], wa2, b_alpha[None, :].astype(F32),
      ltri, lall, g_gla[None, :].astype(F32), w_br_gla.astype(BF16))


def _t5_bucket_int(rel):
    half = REL_BUCKETS // 2
    exact = half // 2
    n = jnp.abs(rel)
    large = jnp.full(rel.shape, exact, I32)
    for j in range(1, half - exact):
        thr = math.ceil(exact * (REL_MAX_DIST / exact) ** (j / (half - exact)) - 1e-9)
        large = large + jnp.where(n >= thr, 1, 0)
    return jnp.where(rel > 0, half, 0) + jnp.where(n < exact, n, large)


def _old_dsa_kernel(relb_ref, qb_ref, qi_ref, sm_ref, gb_ref, ya_ref, kb_ref, vb_ref, kit_ref,
                wbr_ref, out_ref, sc_ref, mb_ref, mbf_ref, vaug_ref, biasw_ref, qp_ref,
                qst_ref, s_ref, mrow_ref, acc_ref, o_ref, *, k_top, seq):
    b = pl.program_id(0)
    j = pl.program_id(1)
    tq = Q_TILE
    t0 = j * tq
    n_pairs = ATT_HEADS // 2
    wwin = 2 * tq

    @pl.when((b == 0) & (j == 0))
    def _():
        rq = lax.broadcasted_iota(I32, (tq, wwin), 0)
        rc = lax.broadcasted_iota(I32, (tq, wwin), 1)
        bucket = _t5_bucket_int(rc - tq - rq)
        far = REL_BUCKETS // 2 - 1
        for h in range(ATT_HEADS):
            a = jnp.zeros((tq, wwin), F32)
            for bk in range(REL_BUCKETS):
                a = jnp.where(bucket == bk, relb_ref[bk * ATT_HEADS + h], a)
            a = a - relb_ref[far * ATT_HEADS + h]
            rows = slice((h % 2) * tq, (h % 2 + 1) * tq)
            biasw_ref[0, h // 2, rows, :] = a
            biasw_ref[1, h // 2, rows, :] = jnp.concatenate(
                [a[:, tq:], jnp.zeros((tq, tq), F32)], axis=1)

    @pl.when(j == 0)
    def _():
        def body(i, _):
            rows = pl.ds(pl.multiple_of(i * K_TILE, K_TILE), K_TILE)
            for p in range(n_pairs):
                vaug_ref[rows, 2 * p * LANES:(2 * p + 1) * LANES] = vb_ref[0, rows, p * LANES:(p + 1) * LANES]
                vaug_ref[rows, (2 * p + 1) * LANES:(2 * p + 2) * LANES] = jnp.ones((K_TILE, LANES), BF16)
            return 0
        lax.fori_loop(0, seq // K_TILE, body, 0)

    lane_q = lax.broadcasted_iota(I32, (tq, LANES), 1)
    for p in range(n_pairs):
        qpair = qb_ref[:, p * LANES:(p + 1) * LANES]
        zero = jnp.zeros_like(qpair)
        qp_ref[p, 0:tq, :] = jnp.where(lane_q < ATT_DH, qpair, zero)
        qp_ref[p, tq:2 * tq, :] = jnp.where(lane_q >= ATT_DH, qpair, zero)
    for h in range(IDX_HEADS):
        qst_ref[h * tq:(h + 1) * tq, :] = qi_ref[:, h * IDX_DH:(h + 1) * IDX_DH]
    wi = sm_ref[:, SM_WI:SM_WI + IDX_HEADS] * (IDX_HEADS ** -0.5 * IDX_DH ** -0.5)
    wb = [jnp.broadcast_to(wi[:, h:h + 1], (tq, S_TILE)) for h in range(IDX_HEADS)]

    n_sel = (j + 2) // 2
    row_s = lax.broadcasted_iota(I32, (tq, S_TILE), 0)
    lane_s = lax.broadcasted_iota(I32, (tq, S_TILE), 1)
    limit = t0 + ((row_s >> 6) + 1) * CHUNK

    def score_body(kt, _):
        ks = pl.multiple_of(kt * S_TILE, S_TILE)
        s_all = _dot(qst_ref[...], kit_ref[0, :, pl.ds(ks, S_TILE)])
        score = jnp.zeros((tq, S_TILE), F32)
        for h in range(IDX_HEADS):
            score = score + jnp.maximum(s_all[h * tq:(h + 1) * tq], 0.0) * wb[h]
        sc_ref[:, pl.ds(ks, S_TILE)] = jnp.where(ks + lane_s < limit, score, -jnp.inf)
        return 0
    lax.fori_loop(0, n_sel, score_body, 0)

    n_cnt = (n_sel + 1) // 2

    @pl.when(n_sel % 2 == 1)
    def _():
        sc_ref[:, pl.ds(pl.multiple_of(n_sel * S_TILE, S_TILE), S_TILE)] = jnp.full((tq, S_TILE), -jnp.inf, F32)

    kf = float(k_top)
    c_groups = C_TILE // LANES

    def score_group(kt, g):
        return sc_ref[:, pl.ds(pl.multiple_of(kt * C_TILE + g * LANES, LANES), LANES)]

    def lane_sum(c):
        return jnp.sum(c, axis=-1, keepdims=True)

    def count(cand, strict):
        cb = jnp.broadcast_to(cand, (tq, LANES))

        def body(kt, c):
            for g in range(c_groups):
                sc = score_group(kt, g)
                c = c + jnp.where((sc > cb) if strict else (sc >= cb), 1.0, 0.0)
            return c
        return lane_sum(lax.fori_loop(0, n_cnt, body, jnp.zeros((tq, LANES), F32)))

    def stats_body(kt, carry):
        lo, hi, n_fin, n_ge0, n_gt0 = carry
        for g in range(c_groups):
            sc = score_group(kt, g)
            fin = sc > -jnp.inf
            hi = jnp.maximum(hi, sc)
            lo = jnp.minimum(lo, jnp.where(fin, sc, jnp.inf))
            n_fin = n_fin + jnp.where(fin, 1.0, 0.0)
            n_ge0 = n_ge0 + jnp.where(sc >= 0.0, 1.0, 0.0)
            n_gt0 = n_gt0 + jnp.where(sc > 0.0, 1.0, 0.0)
        return lo, hi, n_fin, n_ge0, n_gt0
    zeros_l = jnp.zeros((tq, LANES), F32)
    lo_l, hi_l, fin_l, ge0_l, gt0_l = lax.fori_loop(
        0, n_cnt, stats_body,
        (jnp.full((tq, LANES), jnp.inf, F32), jnp.full((tq, LANES), -jnp.inf, F32),
         zeros_l, zeros_l, zeros_l))
    row_min = jnp.min(lo_l, axis=-1, keepdims=True)
    row_max = jnp.max(hi_l, axis=-1, keepdims=True)
    n_adm = lane_sum(fin_l)
    c_ge0 = lane_sum(ge0_l)
    c_gt0 = lane_sum(gt0_l)
    has_thr = n_adm >= kf
    c_max = count(row_max, False)
    at_max = c_max >= kf
    at_zero = (c_gt0 < kf) & (c_ge0 >= kf)
    above_zero = c_gt0 >= kf
    lo0 = jnp.where(at_max, row_max, jnp.where(at_zero | above_zero, 0.0, row_min))
    cnt0 = jnp.where(at_max, c_max, jnp.where(at_zero | above_zero, c_ge0, n_adm))
    hi0 = jnp.where(at_zero | above_zero, row_max, 0.0)
    done0 = jnp.logical_not(has_thr) | at_max | at_zero | (cnt0 == kf)

    def as_flag(done):
        return jnp.where(done, 1.0, 0.0)

    def bisect_cond(carry):
        return carry[3] < 0.5

    def bisect_body(carry):
        lo, hi, cnt_lo, _ = carry
        mid = 0.5 * lo + 0.5 * hi
        open_ = (mid > lo) & (mid < hi)
        c = count(mid, False)
        ge = c >= kf
        lo = jnp.where(ge, mid, lo)
        cnt_lo = jnp.where(ge, c, cnt_lo)
        hi = jnp.where(ge, hi, mid)
        conv = jnp.where(open_ & (cnt_lo != kf), pre_done, 1.0)
        return lo, hi, cnt_lo, jnp.min(conv)
    pre_done = as_flag(done0)
    thr, _, cnt_thr, _ = lax.while_loop(
        bisect_cond, bisect_body, (lo0, hi0, cnt0, jnp.min(pre_done)))

    tie = has_thr & (cnt_thr > kf)
    any_tie = jnp.max(jnp.where(tie, 1.0, 0.0)) > 0.0
    far_end = t0 - tq
    f32_min = float(jnp.finfo(F32).min)

    def write_mask(ks, sel):
        pos = ks + lax.broadcasted_iota(I32, (tq, LANES), 1)
        mb_ref[:, pl.ds(ks, LANES)] = jnp.where(sel, 0.0, NEG)
        mbf_ref[:, pl.ds(ks, LANES)] = jnp.where(sel & (pos < far_end), 0.0, NEG)

    @pl.when(jnp.logical_not(any_tie))
    def _():
        lo = jnp.broadcast_to(jnp.where(has_thr, thr, f32_min), (tq, LANES))

        def body(kt, _):
            ks = pl.multiple_of(kt * LANES, LANES)
            write_mask(ks, sc_ref[:, pl.ds(ks, LANES)] >= lo)
            return 0
        lax.fori_loop(0, n_sel * (S_TILE // LANES), body, 0)

    @pl.when(any_tie)
    def _():
        need = kf - count(thr, True)
        thr_b = jnp.broadcast_to(thr, (tq, LANES))
        lo = jnp.broadcast_to(jnp.where(has_thr, thr, -jnp.inf), (tq, LANES))
        eq_ok = jnp.broadcast_to(has_thr, (tq, LANES))
        ri = lax.broadcasted_iota(I32, (LANES, LANES), 0)
        ci = lax.broadcasted_iota(I32, (LANES, LANES), 1)
        tri = jnp.where(ri <= ci, 1.0, 0.0).astype(BF16)

        def body(kt, seen):
            ks = pl.multiple_of(kt * LANES, LANES)
            sc = sc_ref[:, pl.ds(ks, LANES)]
            eq = (sc == thr_b) & eq_ok
            prefix = seen + _dot(jnp.where(eq, 1.0, 0.0).astype(BF16), tri)
            write_mask(ks, (sc > lo) | (eq & (prefix <= need)))
            return prefix[:, LANES - 1:LANES]
        lax.fori_loop(0, n_sel * (S_TILE // LANES), body, jnp.zeros((tq, 1), F32))

    n_far = (j + 2) // 4
    ws = pl.multiple_of(jnp.maximum(t0 - tq, 0), tq)
    first = (j == 0).astype(I32)
    lane_o = lax.broadcasted_iota(I32, (tq, LANES), 1)

    def lane_group_max(s):
        m = s[:, :LANES]
        for g in range(1, s.shape[1] // LANES):
            m = jnp.maximum(m, s[:, g * LANES:(g + 1) * LANES])
        return m

    def pair_lanes(p, width=1):
        return slice(width * p * LANES, width * (p + 1) * LANES)

    mrow_ref[...] = jnp.full(mrow_ref.shape, NEG, F32)

    def logits_body(kt, _):
        ks = pl.multiple_of(kt * K_TILE, K_TILE)
        mbt = mbf_ref[:, pl.ds(ks, K_TILE)]
        mb2 = jnp.concatenate([mbt, mbt], axis=0)
        for p in range(n_pairs):
            s = _dot_nt(qp_ref[p], kb_ref[0, pl.ds(ks, K_TILE), pair_lanes(p)]) + mb2
            s_ref[p, :, pl.ds(ks, K_TILE)] = s
            mrow_ref[p] = jnp.maximum(mrow_ref[p], lane_group_max(s))
        return 0
    lax.fori_loop(0, n_far, logits_body, 0)

    mbw = mb_ref[:, pl.ds(ws, wwin)]
    mbw2 = jnp.concatenate([mbw, mbw], axis=0)
    for p in range(n_pairs):
        sw = _dot_nt(qp_ref[p], kb_ref[0, pl.ds(ws, wwin), pair_lanes(p)]) + mbw2 + biasw_ref[first, p]
        m_acc = jnp.maximum(mrow_ref[p], lane_group_max(sw))
        m_row = jnp.broadcast_to(jnp.max(m_acc, axis=-1, keepdims=True), (2 * tq, LANES))
        mrow_ref[p] = m_row
        pw = jnp.exp(sw - jnp.concatenate([m_row] * (wwin // LANES), axis=1)).astype(BF16)
        acc_ref[p] = _dot(pw, vaug_ref[pl.ds(ws, wwin), pair_lanes(p, 2)])

    def pv_body(kt, _):
        ks = pl.multiple_of(kt * K_TILE, K_TILE)
        for p in range(n_pairs):
            m_far = jnp.concatenate([mrow_ref[p]] * (K_TILE // LANES), axis=1)
            pexp = jnp.exp(s_ref[p, :, pl.ds(ks, K_TILE)] - m_far).astype(BF16)
            acc_ref[p] += _dot(pexp, vaug_ref[pl.ds(ks, K_TILE), pair_lanes(p, 2)])
        return 0
    lax.fori_loop(0, n_far, pv_body, 0)

    for p in range(n_pairs):
        a = acc_ref[p]
        o_even = a[:tq, :LANES] / a[:tq, LANES:LANES + 1]
        o_odd = a[tq:, :LANES] / a[tq:, LANES:LANES + 1]
        o_ref[:, pair_lanes(p)] = jnp.where(lane_o < ATT_DH, o_even, o_odd).astype(BF16)

    yb = _dot(o_ref[...], wbr_ref[...])
    mixed = jax.nn.sigmoid(gb_ref[...].astype(F32)) * yb + ya_ref[...].astype(F32)
    out_ref[...] = mixed.astype(BF16)


def _old_dsa_mixer(p, ya, rel_bias, w_br_att, batch, seq):
    n = batch * seq
    d = w_br_att.shape[1]
    tq = Q_TILE
    nq = seq // tq
    k_top = min(TOPK_MAX, seq // 4)
    kb3 = p["kb"].reshape(batch, seq, ATT_W)
    vb3 = p["vb"].reshape(batch, seq, ATT_W)
    kit = p["sm"][:, SM_KI:SM_KI + IDX_DH].astype(BF16).reshape(batch, seq, IDX_DH).swapaxes(1, 2)
    row = lambda b, j: (b * nq + j, 0)
    per_b = lambda b, j: (b, 0, 0)
    const = lambda b, j: (0, 0)
    return pl.pallas_call(
        functools.partial(_dsa_kernel, k_top=k_top, seq=seq),
        grid=(batch, nq),
        in_specs=[pl.BlockSpec(memory_space=pltpu.SMEM),
                  pl.BlockSpec((tq, ATT_W), row), pl.BlockSpec((tq, IDX_Q_W), row),
                  pl.BlockSpec((tq, LANES), row), pl.BlockSpec((tq, d), row),
                  pl.BlockSpec((tq, d), row),
                  pl.BlockSpec((1, seq, ATT_W), per_b, pipeline_mode=pl.Buffered(1)),
                  pl.BlockSpec((1, seq, ATT_W), per_b, pipeline_mode=pl.Buffered(1)),
                  pl.BlockSpec((1, IDX_DH, seq), per_b, pipeline_mode=pl.Buffered(1)),
                  pl.BlockSpec((ATT_W, d), const)],
        out_specs=pl.BlockSpec((tq, d), row),
        out_shape=jax.ShapeDtypeStruct((n, d), BF16),
        scratch_shapes=[pltpu.VMEM((tq, seq), F32),
                        pltpu.VMEM((tq, seq), F32),
                        pltpu.VMEM((tq, seq), F32),
                        pltpu.VMEM((seq, 2 * ATT_W), BF16),
                        pltpu.VMEM((2, ATT_HEADS // 2, 2 * tq, 2 * tq), F32),
                        pltpu.VMEM((ATT_HEADS // 2, 2 * tq, LANES), BF16),
                        pltpu.VMEM((IDX_HEADS * tq, IDX_DH), BF16),
                        pltpu.VMEM((ATT_HEADS // 2, 2 * tq, seq), F32),
                        pltpu.VMEM((ATT_HEADS // 2, 2 * tq, LANES), F32),
                        pltpu.VMEM((ATT_HEADS // 2, 2 * tq, 2 * LANES), F32),
                        pltpu.VMEM((tq, ATT_W), BF16)],
        compiler_params=pltpu.CompilerParams(dimension_semantics=("arbitrary", "arbitrary"),
                                             vmem_limit_bytes=VMEM_LIMIT),
        name="dsa_mixer",
    )(rel_bias.astype(F32).reshape(-1), p["qb"], p["qi"], p["sm"], p["gb"], ya, kb3, vb3, kit,
      w_br_att.astype(BF16))


def _dsa_kernel(relb_ref, qit_ref, wit_ref, qbt_ref, gb_ref, ya_ref, kb_ref, vbt_ref, ki_ref,
                wbr_ref, out_ref, sc_ref, mb_ref, mbf_ref, bias_ref, qp_ref, s_ref, mrow_ref,
                acc_ref, ot_ref, *, k_top):
    b = pl.program_id(0)
    j = pl.program_id(1)
    tq = Q_TILE
    t0 = j * tq
    n_pairs = ATT_HEADS // 2
    wwin = 2 * tq
    sub = 8

    @pl.when((b == 0) & (j == 0))
    def _():
        rk = lax.broadcasted_iota(I32, (wwin, tq), 0)
        rq = lax.broadcasted_iota(I32, (wwin, tq), 1)
        bucket = _t5_bucket_int(rk - tq - rq)
        far = REL_BUCKETS // 2 - 1
        for h in range(ATT_HEADS):
            a = jnp.zeros((wwin, tq), F32)
            for bk in range(REL_BUCKETS):
                a = jnp.where(bucket == bk, relb_ref[bk * ATT_HEADS + h], a)
            a = a - relb_ref[far * ATT_HEADS + h]
            cols = slice((h % 2) * tq, (h % 2 + 1) * tq)
            bias_ref[0, h // 2, :, cols] = a
            bias_ref[1, h // 2, :, cols] = jnp.concatenate([a[tq:], jnp.zeros((tq, tq), F32)], axis=0)

    row_q = lax.broadcasted_iota(I32, (LANES, tq), 0)
    for p in range(n_pairs):
        qpair = qbt_ref[0, p * LANES:(p + 1) * LANES, :]
        zero = jnp.zeros_like(qpair)
        qp_ref[p] = jnp.concatenate([jnp.where(row_q < ATT_DH, qpair, zero),
                                     jnp.where(row_q >= ATT_DH, qpair, zero)], axis=1)
    wi = wit_ref[0] * (IDX_HEADS ** -0.5 * IDX_DH ** -0.5)

    n_sel = (j + 2) // 2
    row_s = lax.broadcasted_iota(I32, (S_TILE, tq), 0)
    lane_s = lax.broadcasted_iota(I32, (S_TILE, tq), 1)
    limit = t0 + ((lane_s >> 6) + 1) * CHUNK

    def score_body(kt, _):
        ks = pl.multiple_of(kt * S_TILE, S_TILE)
        s_all = _dot(ki_ref[0, pl.ds(ks, S_TILE), :], qit_ref[0])
        score = jnp.zeros((S_TILE, tq), F32)
        for h in range(IDX_HEADS):
            score = score + jnp.maximum(s_all[:, h * tq:(h + 1) * tq], 0.0) * wi[h:h + 1, :]
        sc_ref[pl.ds(ks, S_TILE), :] = jnp.where(ks + row_s < limit, score, -jnp.inf)
        return 0
    lax.fori_loop(0, n_sel, score_body, 0)

    n_cnt = (n_sel + 1) // 2

    @pl.when(n_sel % 2 == 1)
    def _():
        sc_ref[pl.ds(pl.multiple_of(n_sel * S_TILE, S_TILE), S_TILE), :] = jnp.full((S_TILE, tq), -jnp.inf, F32)

    kf = float(k_top)
    grp_rows = 64
    c_groups = C_TILE // grp_rows

    def score_group(kt, g):
        return sc_ref[pl.ds(pl.multiple_of(kt * C_TILE + g * grp_rows, grp_rows), grp_rows), :]

    def key_sum(c):
        return jnp.sum(c, axis=0, keepdims=True)

    def count(cand, strict):
        def body(kt, c):
            for g in range(c_groups):
                sc = score_group(kt, g)
                c = c + jnp.where((sc > cand) if strict else (sc >= cand), 1.0, 0.0)
            return c
        return key_sum(lax.fori_loop(0, n_cnt, body, jnp.zeros((grp_rows, tq), F32)))

    def stats_body(kt, carry):
        lo, hi, n_fin, n_ge0, n_gt0 = carry
        for g in range(c_groups):
            sc = score_group(kt, g)
            fin = sc > -jnp.inf
            hi = jnp.maximum(hi, sc)
            lo = jnp.minimum(lo, jnp.where(fin, sc, jnp.inf))
            n_fin = n_fin + jnp.where(fin, 1.0, 0.0)
            n_ge0 = n_ge0 + jnp.where(sc >= 0.0, 1.0, 0.0)
            n_gt0 = n_gt0 + jnp.where(sc > 0.0, 1.0, 0.0)
        return lo, hi, n_fin, n_ge0, n_gt0
    zeros_g = jnp.zeros((grp_rows, tq), F32)
    lo_g, hi_g, fin_g, ge0_g, gt0_g = lax.fori_loop(
        0, n_cnt, stats_body,
        (jnp.full((grp_rows, tq), jnp.inf, F32), jnp.full((grp_rows, tq), -jnp.inf, F32),
         zeros_g, zeros_g, zeros_g))
    row_min = jnp.min(lo_g, axis=0, keepdims=True)
    row_max = jnp.max(hi_g, axis=0, keepdims=True)
    n_adm = key_sum(fin_g)
    c_ge0 = key_sum(ge0_g)
    c_gt0 = key_sum(gt0_g)
    has_thr = n_adm >= kf
    c_max = count(row_max, False)
    at_max = c_max >= kf
    at_zero = (c_gt0 < kf) & (c_ge0 >= kf)
    above_zero = c_gt0 >= kf
    lo0 = jnp.where(at_max, row_max, jnp.where(at_zero | above_zero, 0.0, row_min))
    cnt0 = jnp.where(at_max, c_max, jnp.where(at_zero | above_zero, c_ge0, n_adm))
    hi0 = jnp.where(at_zero | above_zero, row_max, 0.0)
    done0 = jnp.logical_not(has_thr) | at_max | at_zero | (cnt0 == kf)
    pre_done = jnp.where(done0, 1.0, 0.0)

    def bisect_step(lo, hi, cnt_lo):
        mid = 0.5 * lo + 0.5 * hi
        open_ = (mid > lo) & (mid < hi)
        c = count(mid, False)
        ge = c >= kf
        lo = jnp.where(ge, mid, lo)
        cnt_lo = jnp.where(ge, c, cnt_lo)
        hi = jnp.where(ge, hi, mid)
        return lo, hi, cnt_lo, jnp.where(open_ & (cnt_lo != kf), pre_done, 1.0)

    def bisect_cond(carry):
        return carry[3] < 0.5

    def bisect_body(carry):
        lo, hi, cnt_lo, _ = carry
        for _ in range(BISECT_STEPS):
            lo, hi, cnt_lo, conv = bisect_step(lo, hi, cnt_lo)
        return lo, hi, cnt_lo, jnp.min(conv)
    thr, _, cnt_thr, _ = lax.while_loop(
        bisect_cond, bisect_body, (lo0, hi0, cnt0, jnp.min(pre_done)))

    tie = has_thr & (cnt_thr > kf)
    any_tie = jnp.max(jnp.where(tie, 1.0, 0.0)) > 0.0
    far_end = t0 - tq
    f32_min = float(jnp.finfo(F32).min)

    def write_mask(ks, sel):
        pos = ks + lax.broadcasted_iota(I32, (LANES, tq), 0)
        mb_ref[pl.ds(ks, LANES), :] = jnp.where(sel, 0.0, NEG)
        mbf_ref[pl.ds(ks, LANES), :] = jnp.where(sel & (pos < far_end), 0.0, NEG)

    @pl.when(jnp.logical_not(any_tie))
    def _():
        lo = jnp.where(has_thr, thr, f32_min)

        def body(kt, _):
            ks = pl.multiple_of(kt * LANES, LANES)
            write_mask(ks, sc_ref[pl.ds(ks, LANES), :] >= lo)
            return 0
        lax.fori_loop(0, n_sel * (S_TILE // LANES), body, 0)

    @pl.when(any_tie)
    def _():
        need = kf - count(thr, True)
        lo = jnp.where(has_thr, thr, -jnp.inf)
        ri = lax.broadcasted_iota(I32, (LANES, LANES), 0)
        ci = lax.broadcasted_iota(I32, (LANES, LANES), 1)
        tri = jnp.where(ci <= ri, 1.0, 0.0).astype(BF16)

        def body(kt, seen):
            ks = pl.multiple_of(kt * LANES, LANES)
            sc = sc_ref[pl.ds(ks, LANES), :]
            eq = (sc == thr) & has_thr
            prefix = seen + _dot(tri, jnp.where(eq, 1.0, 0.0).astype(BF16))
            write_mask(ks, (sc > lo) | (eq & (prefix <= need)))
            return prefix[LANES - 1:LANES, :]
        lax.fori_loop(0, n_sel * (S_TILE // LANES), body, jnp.zeros((1, tq), F32))

    n_far = (j + 2) // 4
    ws = pl.multiple_of(jnp.maximum(t0 - tq, 0), tq)
    first = (j == 0).astype(I32)
    ones_rows = 16

    def pair_lanes(p):
        return slice(p * LANES, (p + 1) * LANES)

    def group_max(s):
        return jnp.max(s.reshape(s.shape[0] // sub, sub, s.shape[1]), axis=0)

    def values_t(p, ks, width):
        return jnp.concatenate([vbt_ref[0, pair_lanes(p), pl.ds(ks, width)],
                                jnp.ones((ones_rows, width), BF16)], axis=0)

    mrow_ref[...] = jnp.full(mrow_ref.shape, NEG, F32)

    def logits_body(kt, _):
        ks = pl.multiple_of(kt * K_TILE, K_TILE)
        mbt = mbf_ref[pl.ds(ks, K_TILE), :]
        mb2 = jnp.concatenate([mbt, mbt], axis=1)
        for p in range(n_pairs):
            s = _dot(kb_ref[0, pl.ds(ks, K_TILE), pair_lanes(p)], qp_ref[p]) + mb2
            s_ref[p, pl.ds(ks, K_TILE), :] = s
            mrow_ref[p] = jnp.maximum(mrow_ref[p], group_max(s))
        return 0
    lax.fori_loop(0, n_far, logits_body, 0)

    mbw = mb_ref[pl.ds(ws, wwin), :]
    mbw2 = jnp.concatenate([mbw, mbw], axis=1)
    for p in range(n_pairs):
        sw = _dot(kb_ref[0, pl.ds(ws, wwin), pair_lanes(p)], qp_ref[p]) + mbw2 + bias_ref[first, p]
        m8 = jnp.maximum(mrow_ref[p], group_max(sw))
        m1 = jnp.max(m8, axis=0, keepdims=True)
        mrow_ref[p] = jnp.broadcast_to(m1, (sub, 2 * tq))
        pw = jnp.exp(sw - m1).astype(BF16)
        acc_ref[p] = _dot(values_t(p, ws, wwin), pw)

    def pv_body(kt, _):
        ks = pl.multiple_of(kt * K_TILE, K_TILE)
        for p in range(n_pairs):
            pexp = jnp.exp(s_ref[p, pl.ds(ks, K_TILE), :] - mrow_ref[p, 0:1, :]).astype(BF16)
            acc_ref[p] += _dot(values_t(p, ks, K_TILE), pexp)
        return 0
    lax.fori_loop(0, n_far, pv_body, 0)

    row_o = lax.broadcasted_iota(I32, (LANES, tq), 0)
    for p in range(n_pairs):
        a = acc_ref[p]
        o = a[:LANES, :] / a[LANES:LANES + 1, :]
        ot_ref[pair_lanes(p), :] = jnp.where(row_o < ATT_DH, o[:, :tq], o[:, tq:]).astype(BF16)

    yb = _dot_tn(ot_ref[...], wbr_ref[...])
    mixed = jax.nn.sigmoid(gb_ref[...].astype(F32)) * yb + ya_ref[...].astype(F32)
    out_ref[...] = mixed.astype(BF16)


def _dsa_mixer(p, ya, rel_bias, w_br_att, batch, seq):
    n = batch * seq
    d = w_br_att.shape[1]
    tq = Q_TILE
    nq = seq // tq
    nb = n // tq
    k_top = min(TOPK_MAX, seq // 4)
    qit = (p["qi"].reshape(nb, tq, IDX_HEADS, IDX_DH).transpose(0, 3, 2, 1)
           .reshape(nb, IDX_DH, IDX_HEADS * tq))
    wit = p["sm"][:, SM_WI:SM_WI + IDX_HEADS].reshape(nb, tq, IDX_HEADS).swapaxes(1, 2)
    qbt = p["qb"].reshape(nb, tq, ATT_W).swapaxes(1, 2)
    kb3 = p["kb"].reshape(batch, seq, ATT_W)
    vbt = p["vb"].reshape(batch, seq, ATT_W).swapaxes(1, 2)
    ki3 = p["sm"][:, SM_KI:SM_KI + IDX_DH].astype(BF16).reshape(batch, seq, IDX_DH)
    blk = lambda b, j: (b * nq + j, 0, 0)
    row = lambda b, j: (b * nq + j, 0)
    per_b = lambda b, j: (b, 0, 0)
    const = lambda b, j: (0, 0)
    once = pl.Buffered(1)
    return pl.pallas_call(
        functools.partial(_dsa_kernel, k_top=k_top),
        grid=(batch, nq),
        in_specs=[pl.BlockSpec(memory_space=pltpu.SMEM),
                  pl.BlockSpec((1, IDX_DH, IDX_HEADS * tq), blk),
                  pl.BlockSpec((1, IDX_HEADS, tq), blk),
                  pl.BlockSpec((1, ATT_W, tq), blk),
                  pl.BlockSpec((tq, d), row), pl.BlockSpec((tq, d), row),
                  pl.BlockSpec((1, seq, ATT_W), per_b, pipeline_mode=once),
                  pl.BlockSpec((1, ATT_W, seq), per_b, pipeline_mode=once),
                  pl.BlockSpec((1, seq, IDX_DH), per_b, pipeline_mode=once),
                  pl.BlockSpec((ATT_W, d), const)],
        out_specs=pl.BlockSpec((tq, d), row),
        out_shape=jax.ShapeDtypeStruct((n, d), BF16),
        scratch_shapes=[pltpu.VMEM((seq, tq), F32),
                        pltpu.VMEM((seq, tq), F32),
                        pltpu.VMEM((seq, tq), F32),
                        pltpu.VMEM((2, ATT_HEADS // 2, 2 * tq, 2 * tq), F32),
                        pltpu.VMEM((ATT_HEADS // 2, LANES, 2 * tq), BF16),
                        pltpu.VMEM((ATT_HEADS // 2, seq, 2 * tq), F32),
                        pltpu.VMEM((ATT_HEADS // 2, 8, 2 * tq), F32),
                        pltpu.VMEM((ATT_HEADS // 2, LANES + 16, 2 * tq), F32),
                        pltpu.VMEM((ATT_W, tq), BF16)],
        compiler_params=pltpu.CompilerParams(dimension_semantics=("arbitrary", "arbitrary"),
                                             vmem_limit_bytes=VMEM_LIMIT),
        name="dsa_mixer",
    )(rel_bias.astype(F32).reshape(-1), qit, wit, qbt, p["gb"], ya, kb3, vbt, ki3,
      w_br_att.astype(BF16))


RT_G0 = 0
RT_E0 = N_GROUPS


def _moe_kernel(x_ref, mx_ref, wo_ref, gf_ref, wrh_ref, wrl_ref, br_ref, wg_ref, wu_ref,
                wd_ref, out_ref, h2_ref):
    tm = x_ref.shape[0]
    x1 = x_ref[...] + _dot(mx_ref[...], wo_ref[...])
    out_ref[...] = x1
    ms = jnp.mean(x1 * x1, axis=-1, keepdims=True)
    h = x1 * lax.rsqrt(ms + EPS) * gf_ref[...]
    h_hi, h_lo = _split_bf16(h)
    h2_ref[...] = h_hi
    logits = (_dot(h_hi, wrh_ref[...]) + _dot(h_lo, wrh_ref[...]) + _dot(h_hi, wrl_ref[...])
              + br_ref[...])
    lane = lax.broadcasted_iota(I32, (tm, LANES), 1)
    big = jnp.int32(LANES)
    is_g = lane < N_GROUPS
    gl = jnp.where(is_g, logits, -jnp.inf)
    gmax = jnp.max(gl, axis=-1, keepdims=True)
    gidx = jnp.min(jnp.where(gl == gmax, lane, big), axis=-1, keepdims=True)
    gsum = jnp.sum(jnp.where(is_g, jnp.exp(gl - gmax), 0.0), axis=-1, keepdims=True)
    g_w = 1.0 / gsum
    in_grp = (lane >= RT_E0) & (lane < RT_E0 + N_EXPERTS) & (((lane - RT_E0) >> 3) == gidx)
    ev = jnp.where(in_grp, logits, -jnp.inf)
    t1 = jnp.max(ev, axis=-1, keepdims=True)
    i1 = jnp.min(jnp.where(ev == t1, lane, big), axis=-1, keepdims=True)
    ev2 = jnp.where(lane == i1, -jnp.inf, ev)
    t2 = jnp.max(ev2, axis=-1, keepdims=True)
    i2 = jnp.min(jnp.where(ev2 == t2, lane, big), axis=-1, keepdims=True)
    e2 = jnp.exp(t2 - t1)
    w1 = 1.0 / (1.0 + e2)
    w2 = e2 * w1
    comb = g_w * (jnp.where(lane == i1, w1, 0.0) + jnp.where(lane == i2, w2, 0.0))

    h2 = h2_ref[...]
    per_chunk = MOE_COLS // D_EXPERT
    for grp in range(N_GROUPS):
        for c in range(EXPERTS_PER_GROUP // per_chunk):
            cols = slice(c * MOE_COLS, (c + 1) * MOE_COLS)
            hg = _dot(h2, wg_ref[grp, :, cols])
            hu = _dot(h2, wu_ref[grp, :, cols])
            e0 = RT_E0 + grp * EXPERTS_PER_GROUP + c * per_chunk
            scale = jnp.concatenate(
                [jnp.broadcast_to(comb[:, e:e + 1], (tm, D_EXPERT)) for e in range(e0, e0 + per_chunk)],
                axis=1)
            hid = (hg * jax.nn.sigmoid(hg) * hu * scale).astype(BF16)
            out_ref[...] += _dot(hid, wd_ref[grp, cols, :])


def _out_proj_moe(x2, mixed, w_out, g_ffn, w_rg, b_rg, w_re, b_re, w_gate, w_up, w_down):
    n, d = x2.shape
    tm = ROW_TILE
    gw = EXPERTS_PER_GROUP * D_EXPERT
    w_r = jnp.zeros((d, LANES), F32).at[:, RT_G0:RT_G0 + N_GROUPS].set(w_rg)
    w_r = w_r.at[:, RT_E0:RT_E0 + N_EXPERTS].set(w_re)
    wr_hi = w_r.astype(BF16)
    wr_lo = (w_r - wr_hi.astype(F32)).astype(BF16)
    b_r = jnp.zeros((1, LANES), F32).at[0, RT_G0:RT_G0 + N_GROUPS].set(b_rg)
    b_r = b_r.at[0, RT_E0:RT_E0 + N_EXPERTS].set(b_re)

    def by_group(w):
        return (w.reshape(N_GROUPS, EXPERTS_PER_GROUP, d, D_EXPERT).transpose(0, 2, 1, 3)
                .reshape(N_GROUPS, d, gw).astype(BF16))
    wg = by_group(w_gate)
    wu = by_group(w_up)
    wd = w_down.reshape(N_GROUPS, gw, d).astype(BF16)

    row = lambda i: (i, 0)
    const = lambda i: (0, 0)
    const3 = lambda i: (0, 0, 0)
    once = pl.Buffered(1)
    return pl.pallas_call(
        _moe_kernel,
        grid=(n // tm,),
        in_specs=[pl.BlockSpec((tm, d), row), pl.BlockSpec((tm, d), row),
                  pl.BlockSpec((d, d), const, pipeline_mode=once), pl.BlockSpec((1, d), const),
                  pl.BlockSpec((d, LANES), const), pl.BlockSpec((d, LANES), const),
                  pl.BlockSpec((1, LANES), const),
                  pl.BlockSpec((N_GROUPS, d, gw), const3, pipeline_mode=once),
                  pl.BlockSpec((N_GROUPS, d, gw), const3, pipeline_mode=once),
                  pl.BlockSpec((N_GROUPS, gw, d), const3, pipeline_mode=once)],
        out_specs=pl.BlockSpec((tm, d), row),
        out_shape=jax.ShapeDtypeStruct((n, d), F32),
        scratch_shapes=[pltpu.VMEM((tm, d), BF16)],
        compiler_params=pltpu.CompilerParams(dimension_semantics=("arbitrary",),
                                             vmem_limit_bytes=VMEM_LIMIT),
        name="out_proj_moe",
    )(x2, mixed, w_out.astype(BF16), g_ffn[None, :].astype(F32), wr_hi, wr_lo, b_r,
      wg, wu, wd)


def kernel(x, g_mix, w_in, w_alpha2, b_alpha, g_gla, w_br_gla, g_q, g_k, rel_bias, w_br_att, w_out, g_ffn, w_rg, b_rg, w_re, b_re, w_gate, w_up, w_down):
    batch, seq, d = x.shape
    assert seq % ROW_TILE == 0 and seq % K_TILE == 0 and (batch * seq) % ROW_TILE == 0
    x2 = x.reshape(batch * seq, d)
    for l in range(g_mix.shape[0]):
        p = _in_projection(x2, g_mix[l], w_in[l], g_q[l], g_k[l])
        ya = _gla_mixer(p, w_alpha2[l], b_alpha[l], g_gla[l], w_br_gla[l], batch, seq)
        mixed = _dsa_mixer(p, ya, rel_bias, w_br_att[l], batch, seq)
        x2 = _out_proj_moe(x2, mixed, w_out[l], g_ffn[l], w_rg[l], b_rg[l], w_re[l], b_re[l],
                           w_gate[l], w_up[l], w_down[l])
    return x2.reshape(batch, seq, d)
```

```python
import functools
import math

import jax
import jax.numpy as jnp
from jax import lax
from jax.experimental import pallas as pl
from jax.experimental.pallas import tpu as pltpu

F32 = jnp.float32
BF16 = jnp.bfloat16
I32 = jnp.int32

CHUNK = 64
GLA_HEADS = 4
GLA_DK = 64
GLA_DV = 128
GLA_GATE_RANK = 16
GLA_TAU = 16.0
ATT_HEADS = 8
ATT_DH = 64
IDX_HEADS = 8
IDX_DH = 32
TOPK_MAX = 256
REL_BUCKETS = 32
REL_MAX_DIST = 128
N_GROUPS = 4
EXPERTS_PER_GROUP = 8
N_EXPERTS = N_GROUPS * EXPERTS_PER_GROUP
D_EXPERT = 128
EPS = 1e-6

GLA_QK_W = GLA_HEADS * GLA_DK
GLA_V_W = GLA_HEADS * GLA_DV
ATT_W = ATT_HEADS * ATT_DH
IDX_Q_W = IDX_HEADS * IDX_DH

LANES = 128
VMEM_LIMIT = 56 * 1024 * 1024

SM_KI = 0
SM_ALR = IDX_DH
SM_WI = IDX_DH + GLA_GATE_RANK

NEG = -1e30
LOG2_E = math.log2(math.e)

ROW_TILE = 512
Q_TILE = 128
K_TILE = 512
MOE_COLS = 256
S_TILE = 256
C_TILE = 2 * S_TILE
BISECT_STEPS = 2


def _dot(a, b):
    return jnp.dot(a, b, preferred_element_type=F32)


def _dot_nt(a, b):
    return lax.dot_general(a, b, (((1,), (1,)), ((), ())), preferred_element_type=F32)


def _dot_tn(a, b):
    return lax.dot_general(a, b, (((0,), (0,)), ((), ())), preferred_element_type=F32)


def _split_bf16(a):
    hi = a.astype(BF16)
    lo = (a - hi.astype(F32)).astype(BF16)
    return hi, lo


_PROJ = (("qa", GLA_QK_W), ("ka", GLA_QK_W), ("va", GLA_V_W), ("ra", GLA_V_W),
         ("qb", ATT_W), ("kb", ATT_W), ("vb", ATT_W), ("qi", IDX_Q_W),
         ("ga", None), ("gb", None), ("sm", LANES))


def _inproj_kernel(x_ref, g_ref, w_ref, gq_ref, gk_ref, bd_ref, *out_refs, offs):
    x = x_ref[...]
    ms = jnp.mean(x * x, axis=-1, keepdims=True)
    hn = (x * lax.rsqrt(ms + EPS) * g_ref[...]).astype(BF16)

    def proj(name):
        c0, c1 = offs[name]
        return _dot(hn, w_ref[:, c0:c1])

    def head_norm(y, gain):
        ss = _dot((y * y).astype(BF16), bd_ref[...])
        return y * lax.rsqrt(ss * (1.0 / ATT_DH) + EPS) * gain

    (qa_ref, ka_ref, va_ref, ra_ref, qbt_ref, kb_ref, vbt_ref, qit_ref, ga_ref, gb_ref, sm_ref,
     ki_ref, wit_ref) = out_refs
    tq = Q_TILE
    n_blk = x.shape[0] // tq

    def blocks_t(y):
        return [y[r * tq:(r + 1) * tq, :].T for r in range(n_blk)]

    qa_ref[...] = (proj("qa") * (GLA_DK ** -0.5)).astype(BF16)
    ka_ref[...] = proj("ka").astype(BF16)
    va_ref[...] = proj("va").astype(BF16)
    ra_ref[...] = proj("ra").astype(BF16)
    qb = head_norm(proj("qb"), gq_ref[...]) * (ATT_DH ** -0.5 * LOG2_E)
    for r, yt in enumerate(blocks_t(qb)):
        qbt_ref[r] = yt.astype(BF16)
    kb_ref[...] = head_norm(proj("kb"), gk_ref[...]).astype(BF16)
    vbt_ref[0] = jnp.concatenate(blocks_t(proj("vb")), axis=1).astype(BF16)
    for r, yt in enumerate(blocks_t(proj("qi"))):
        qit_ref[r] = jnp.concatenate(
            [yt[h * IDX_DH:(h + 1) * IDX_DH, :] for h in range(IDX_HEADS)], axis=1).astype(BF16)
    ga_ref[...] = proj("ga").astype(BF16)
    gb_ref[...] = proj("gb").astype(BF16)
    small = proj("sm")
    sm_ref[...] = small
    ki_ref[...] = small[:, SM_KI:SM_KI + IDX_DH].astype(BF16)
    for r, yt in enumerate(blocks_t(small)):
        wit_ref[r] = yt[SM_WI:SM_WI + IDX_HEADS, :]


def _in_projection(x2, g_mix, w_in, g_q, g_k, seq):
    n, d = x2.shape
    widths = {name: (d if w is None else w) for name, w in _PROJ}
    ref_order = (("qa", GLA_QK_W), ("ka", GLA_QK_W), ("va", GLA_V_W), ("ra", GLA_V_W),
                 ("alr", GLA_GATE_RANK), ("qb", ATT_W), ("kb", ATT_W), ("vb", ATT_W),
                 ("qi", IDX_Q_W), ("ki", IDX_DH), ("wi", IDX_HEADS), ("ga", d), ("gb", d))
    cols, c = {}, 0
    for name, w in ref_order:
        cols[name] = w_in[:, c:c + w]
        c += w
    small = jnp.concatenate(
        [cols["ki"], cols["alr"], cols["wi"],
         jnp.zeros((d, LANES - GLA_GATE_RANK - IDX_DH - IDX_HEADS), w_in.dtype)], axis=1)
    cols["sm"] = small
    w_cat = jnp.concatenate([cols[name] for name, _ in _PROJ], axis=1).astype(BF16)
    offs, c = {}, 0
    for name, _ in _PROJ:
        offs[name] = (c, c + widths[name])
        c += widths[name]
    d_cat = c

    head = jnp.arange(ATT_W) // ATT_DH
    blockdiag = (head[:, None] == head[None, :]).astype(BF16)
    gq = jnp.tile(g_q, ATT_HEADS)[None, :].astype(F32)
    gk = jnp.tile(g_k, ATT_HEADS)[None, :].astype(F32)

    tm = ROW_TILE
    tq = Q_TILE
    nt = seq // tm
    nb = n // tq
    const = lambda i: (0, 0)
    row = lambda i: (i, 0)
    blk = lambda i: (i, 0, 0)
    shapes = {name: ((n, widths[name]), F32 if name == "sm" else BF16, (tm, widths[name]), row)
              for name, _ in _PROJ}
    shapes["qb"] = ((nb, ATT_W, tq), BF16, (tm // tq, ATT_W, tq), blk)
    shapes["vb"] = ((n // seq, ATT_W, seq), BF16, (1, ATT_W, tm), lambda i: (i // nt, 0, i % nt))
    shapes["qi"] = ((nb, IDX_DH, IDX_HEADS * tq), BF16, (tm // tq, IDX_DH, IDX_HEADS * tq), blk)
    names = [name for name, _ in _PROJ] + ["ki", "wi"]
    shapes["ki"] = ((n, IDX_DH), BF16, (tm, IDX_DH), row)
    shapes["wi"] = ((nb, IDX_HEADS, tq), F32, (tm // tq, IDX_HEADS, tq), blk)
    out_shape = [jax.ShapeDtypeStruct(shapes[k][0], shapes[k][1]) for k in names]
    out_specs = [pl.BlockSpec(shapes[k][2], shapes[k][3]) for k in names]
    outs = pl.pallas_call(
        functools.partial(_inproj_kernel, offs=offs),
        grid=(n // tm,),
        in_specs=[pl.BlockSpec((tm, d), lambda i: (i, 0)),
                  pl.BlockSpec((1, d), const),
                  pl.BlockSpec((d, d_cat), const),
                  pl.BlockSpec((1, ATT_W), const),
                  pl.BlockSpec((1, ATT_W), const),
                  pl.BlockSpec((ATT_W, ATT_W), const)],
        out_specs=out_specs,
        out_shape=out_shape,
        compiler_params=pltpu.CompilerParams(dimension_semantics=("arbitrary",),
                                             vmem_limit_bytes=VMEM_LIMIT),
        name="in_projection",
    )(x2, g_mix[None, :].astype(F32), w_cat, gq, gk, blockdiag)
    return dict(zip(names, outs))


def _gla_kernel(qa_ref, ka_ref, va_ref, ra_ref, sm_ref, ga_ref, wa2_ref, ba_ref, ltri_ref,
                lall_ref, gg_ref, wbr_ref, out_ref, st_ref, o_ref):
    @pl.when(pl.program_id(1) == 0)
    def _():
        st_ref[...] = jnp.zeros_like(st_ref)

    tb = qa_ref.shape[0]
    z = _dot(sm_ref[...].astype(BF16), wa2_ref[...]) + ba_ref[...]
    log_a = (jnp.minimum(z, 0.0) - jnp.log1p(jnp.exp(-jnp.abs(z)))) * (1.0 / GLA_TAU)
    la_hi, la_lo = _split_bf16(log_a)
    ltri = ltri_ref[...]
    lall = lall_ref[...]
    cum = _dot(ltri, la_hi) + _dot(ltri, la_lo)
    tot = _dot(lall, la_hi) + _dot(lall, la_lo)
    k_dec = (ka_ref[...].astype(F32) * jnp.exp(tot - cum)).astype(BF16)
    dec = jnp.exp(tot)

    states = [st_ref[h] for h in range(GLA_HEADS)]
    for c in range(tb // CHUNK):
        rows = slice(c * CHUNK, (c + 1) * CHUNK)
        for h in range(GLA_HEADS):
            kl = slice(h * GLA_DK, (h + 1) * GLA_DK)
            vl = slice(h * GLA_DV, (h + 1) * GLA_DV)
            u_t = _dot_tn(va_ref[rows, vl], k_dec[rows, kl])
            s = states[h] * dec[c * CHUNK:c * CHUNK + 1, kl] + u_t
            states[h] = s
            o_ref[rows, vl] = _dot_nt(qa_ref[rows, kl], s.astype(BF16))
    for h in range(GLA_HEADS):
        st_ref[h] = states[h]

    r = ra_ref[...].astype(F32)
    gated = []
    for h in range(GLA_HEADS):
        vl = slice(h * GLA_DV, (h + 1) * GLA_DV)
        oh = o_ref[:, vl]
        ms = jnp.mean(oh * oh, axis=-1, keepdims=True)
        oh = oh * lax.rsqrt(ms + EPS) * gg_ref[...]
        rh = r[:, vl]
        gated.append((oh * (rh * jax.nn.sigmoid(rh))).astype(BF16))
    og = jnp.concatenate(gated, axis=1)
    ya = _dot(og, wbr_ref[...])
    out_ref[...] = (jax.nn.sigmoid(ga_ref[...].astype(F32)) * ya).astype(BF16)


def _gla_mixer(p, w_alpha2, b_alpha, g_gla, w_br_gla, batch, seq):
    n = batch * seq
    d = w_br_gla.shape[1]
    tb = ROW_TILE
    nt = seq // tb
    wa2 = jnp.zeros((LANES, GLA_QK_W), F32).at[SM_ALR:SM_ALR + GLA_GATE_RANK].set(w_alpha2).astype(BF16)
    r = jnp.arange(tb)
    same = (r[:, None] // CHUNK) == (r[None, :] // CHUNK)
    ltri = (same & (r[None, :] <= r[:, None])).astype(BF16)
    lall = same.astype(BF16)
    row = lambda b, i: (b * nt + i, 0)
    const = lambda b, i: (0, 0)
    return pl.pallas_call(
        _gla_kernel,
        grid=(batch, nt),
        in_specs=[pl.BlockSpec((tb, GLA_QK_W), row), pl.BlockSpec((tb, GLA_QK_W), row),
                  pl.BlockSpec((tb, GLA_V_W), row), pl.BlockSpec((tb, GLA_V_W), row),
                  pl.BlockSpec((tb, LANES), row), pl.BlockSpec((tb, d), row),
                  pl.BlockSpec((LANES, GLA_QK_W), const), pl.BlockSpec((1, GLA_QK_W), const),
                  pl.BlockSpec((tb, tb), const), pl.BlockSpec((tb, tb), const),
                  pl.BlockSpec((1, GLA_DV), const), pl.BlockSpec((GLA_V_W, d), const)],
        out_specs=pl.BlockSpec((tb, d), row),
        out_shape=jax.ShapeDtypeStruct((n, d), BF16),
        scratch_shapes=[pltpu.VMEM((GLA_HEADS, GLA_DV, GLA_DK), F32),
                        pltpu.VMEM((tb, GLA_V_W), F32)],
        compiler_params=pltpu.CompilerParams(dimension_semantics=("arbitrary", "arbitrary"),
                                             vmem_limit_bytes=VMEM_LIMIT),
        name="gla_mixer",
    )(p["qa"], p["ka"], p["va"], p["ra"], p["sm"], p["ga"], wa2, b_alpha[None, :].astype(F32),
      ltri, lall, g_gla[None, :].astype(F32), w_br_gla.astype(BF16))


def _t5_bucket_int(rel):
    half = REL_BUCKETS // 2
    exact = half // 2
    n = jnp.abs(rel)
    large = jnp.full(rel.shape, exact, I32)
    for j in range(1, half - exact):
        thr = math.ceil(exact * (REL_MAX_DIST / exact) ** (j / (half - exact)) - 1e-9)
        large = large + jnp.where(n >= thr, 1, 0)
    return jnp.where(rel > 0, half, 0) + jnp.where(n < exact, n, large)


def _old_dsa_kernel(relb_ref, qb_ref, qi_ref, sm_ref, gb_ref, ya_ref, kb_ref, vb_ref, kit_ref,
                wbr_ref, out_ref, sc_ref, mb_ref, mbf_ref, vaug_ref, biasw_ref, qp_ref,
                qst_ref, s_ref, mrow_ref, acc_ref, o_ref, *, k_top, seq):
    b = pl.program_id(0)
    j = pl.program_id(1)
    tq = Q_TILE
    t0 = j * tq
    n_pairs = ATT_HEADS // 2
    wwin = 2 * tq

    @pl.when((b == 0) & (j == 0))
    def _():
        rq = lax.broadcasted_iota(I32, (tq, wwin), 0)
        rc = lax.broadcasted_iota(I32, (tq, wwin), 1)
        bucket = _t5_bucket_int(rc - tq - rq)
        far = REL_BUCKETS // 2 - 1
        for h in range(ATT_HEADS):
            a = jnp.zeros((tq, wwin), F32)
            for bk in range(REL_BUCKETS):
                a = jnp.where(bucket == bk, relb_ref[bk * ATT_HEADS + h], a)
            a = a - relb_ref[far * ATT_HEADS + h]
            rows = slice((h % 2) * tq, (h % 2 + 1) * tq)
            biasw_ref[0, h // 2, rows, :] = a
            biasw_ref[1, h // 2, rows, :] = jnp.concatenate(
                [a[:, tq:], jnp.zeros((tq, tq), F32)], axis=1)

    @pl.when(j == 0)
    def _():
        def body(i, _):
            rows = pl.ds(pl.multiple_of(i * K_TILE, K_TILE), K_TILE)
            for p in range(n_pairs):
                vaug_ref[rows, 2 * p * LANES:(2 * p + 1) * LANES] = vb_ref[0, rows, p * LANES:(p + 1) * LANES]
                vaug_ref[rows, (2 * p + 1) * LANES:(2 * p + 2) * LANES] = jnp.ones((K_TILE, LANES), BF16)
            return 0
        lax.fori_loop(0, seq // K_TILE, body, 0)

    lane_q = lax.broadcasted_iota(I32, (tq, LANES), 1)
    for p in range(n_pairs):
        qpair = qb_ref[:, p * LANES:(p + 1) * LANES]
        zero = jnp.zeros_like(qpair)
        qp_ref[p, 0:tq, :] = jnp.where(lane_q < ATT_DH, qpair, zero)
        qp_ref[p, tq:2 * tq, :] = jnp.where(lane_q >= ATT_DH, qpair, zero)
    for h in range(IDX_HEADS):
        qst_ref[h * tq:(h + 1) * tq, :] = qi_ref[:, h * IDX_DH:(h + 1) * IDX_DH]
    wi = sm_ref[:, SM_WI:SM_WI + IDX_HEADS] * (IDX_HEADS ** -0.5 * IDX_DH ** -0.5)
    wb = [jnp.broadcast_to(wi[:, h:h + 1], (tq, S_TILE)) for h in range(IDX_HEADS)]

    n_sel = (j + 2) // 2
    row_s = lax.broadcasted_iota(I32, (tq, S_TILE), 0)
    lane_s = lax.broadcasted_iota(I32, (tq, S_TILE), 1)
    limit = t0 + ((row_s >> 6) + 1) * CHUNK

    def score_body(kt, _):
        ks = pl.multiple_of(kt * S_TILE, S_TILE)
        s_all = _dot(qst_ref[...], kit_ref[0, :, pl.ds(ks, S_TILE)])
        score = jnp.zeros((tq, S_TILE), F32)
        for h in range(IDX_HEADS):
            score = score + jnp.maximum(s_all[h * tq:(h + 1) * tq], 0.0) * wb[h]
        sc_ref[:, pl.ds(ks, S_TILE)] = jnp.where(ks + lane_s < limit, score, -jnp.inf)
        return 0
    lax.fori_loop(0, n_sel, score_body, 0)

    n_cnt = (n_sel + 1) // 2

    @pl.when(n_sel % 2 == 1)
    def _():
        sc_ref[:, pl.ds(pl.multiple_of(n_sel * S_TILE, S_TILE), S_TILE)] = jnp.full((tq, S_TILE), -jnp.inf, F32)

    kf = float(k_top)
    c_groups = C_TILE // LANES

    def score_group(kt, g):
        return sc_ref[:, pl.ds(pl.multiple_of(kt * C_TILE + g * LANES, LANES), LANES)]

    def lane_sum(c):
        return jnp.sum(c, axis=-1, keepdims=True)

    def count(cand, strict):
        cb = jnp.broadcast_to(cand, (tq, LANES))

        def body(kt, c):
            for g in range(c_groups):
                sc = score_group(kt, g)
                c = c + jnp.where((sc > cb) if strict else (sc >= cb), 1.0, 0.0)
            return c
        return lane_sum(lax.fori_loop(0, n_cnt, body, jnp.zeros((tq, LANES), F32)))

    def stats_body(kt, carry):
        lo, hi, n_fin, n_ge0, n_gt0 = carry
        for g in range(c_groups):
            sc = score_group(kt, g)
            fin = sc > -jnp.inf
            hi = jnp.maximum(hi, sc)
            lo = jnp.minimum(lo, jnp.where(fin, sc, jnp.inf))
            n_fin = n_fin + jnp.where(fin, 1.0, 0.0)
            n_ge0 = n_ge0 + jnp.where(sc >= 0.0, 1.0, 0.0)
            n_gt0 = n_gt0 + jnp.where(sc > 0.0, 1.0, 0.0)
        return lo, hi, n_fin, n_ge0, n_gt0
    zeros_l = jnp.zeros((tq, LANES), F32)
    lo_l, hi_l, fin_l, ge0_l, gt0_l = lax.fori_loop(
        0, n_cnt, stats_body,
        (jnp.full((tq, LANES), jnp.inf, F32), jnp.full((tq, LANES), -jnp.inf, F32),
         zeros_l, zeros_l, zeros_l))
    row_min = jnp.min(lo_l, axis=-1, keepdims=True)
    row_max = jnp.max(hi_l, axis=-1, keepdims=True)
    n_adm = lane_sum(fin_l)
    c_ge0 = lane_sum(ge0_l)
    c_gt0 = lane_sum(gt0_l)
    has_thr = n_adm >= kf
    c_max = count(row_max, False)
    at_max = c_max >= kf
    at_zero = (c_gt0 < kf) & (c_ge0 >= kf)
    above_zero = c_gt0 >= kf
    lo0 = jnp.where(at_max, row_max, jnp.where(at_zero | above_zero, 0.0, row_min))
    cnt0 = jnp.where(at_max, c_max, jnp.where(at_zero | above_zero, c_ge0, n_adm))
    hi0 = jnp.where(at_zero | above_zero, row_max, 0.0)
    done0 = jnp.logical_not(has_thr) | at_max | at_zero | (cnt0 == kf)

    def as_flag(done):
        return jnp.where(done, 1.0, 0.0)

    def bisect_cond(carry):
        return carry[3] < 0.5

    def bisect_body(carry):
        lo, hi, cnt_lo, _ = carry
        mid = 0.5 * lo + 0.5 * hi
        open_ = (mid > lo) & (mid < hi)
        c = count(mid, False)
        ge = c >= kf
        lo = jnp.where(ge, mid, lo)
        cnt_lo = jnp.where(ge, c, cnt_lo)
        hi = jnp.where(ge, hi, mid)
        conv = jnp.where(open_ & (cnt_lo != kf), pre_done, 1.0)
        return lo, hi, cnt_lo, jnp.min(conv)
    pre_done = as_flag(done0)
    thr, _, cnt_thr, _ = lax.while_loop(
        bisect_cond, bisect_body, (lo0, hi0, cnt0, jnp.min(pre_done)))

    tie = has_thr & (cnt_thr > kf)
    any_tie = jnp.max(jnp.where(tie, 1.0, 0.0)) > 0.0
    far_end = t0 - tq
    f32_min = float(jnp.finfo(F32).min)

    def write_mask(ks, sel):
        pos = ks + lax.broadcasted_iota(I32, (tq, LANES), 1)
        mb_ref[:, pl.ds(ks, LANES)] = jnp.where(sel, 0.0, NEG)
        mbf_ref[:, pl.ds(ks, LANES)] = jnp.where(sel & (pos < far_end), 0.0, NEG)

    @pl.when(jnp.logical_not(any_tie))
    def _():
        lo = jnp.broadcast_to(jnp.where(has_thr, thr, f32_min), (tq, LANES))

        def body(kt, _):
            ks = pl.multiple_of(kt * LANES, LANES)
            write_mask(ks, sc_ref[:, pl.ds(ks, LANES)] >= lo)
            return 0
        lax.fori_loop(0, n_sel * (S_TILE // LANES), body, 0)

    @pl.when(any_tie)
    def _():
        need = kf - count(thr, True)
        thr_b = jnp.broadcast_to(thr, (tq, LANES))
        lo = jnp.broadcast_to(jnp.where(has_thr, thr, -jnp.inf), (tq, LANES))
        eq_ok = jnp.broadcast_to(has_thr, (tq, LANES))
        ri = lax.broadcasted_iota(I32, (LANES, LANES), 0)
        ci = lax.broadcasted_iota(I32, (LANES, LANES), 1)
        tri = jnp.where(ri <= ci, 1.0, 0.0).astype(BF16)

        def body(kt, seen):
            ks = pl.multiple_of(kt * LANES, LANES)
            sc = sc_ref[:, pl.ds(ks, LANES)]
            eq = (sc == thr_b) & eq_ok
            prefix = seen + _dot(jnp.where(eq, 1.0, 0.0).astype(BF16), tri)
            write_mask(ks, (sc > lo) | (eq & (prefix <= need)))
            return prefix[:, LANES - 1:LANES]
        lax.fori_loop(0, n_sel * (S_TILE // LANES), body, jnp.zeros((tq, 1), F32))

    n_far = (j + 2) // 4
    ws = pl.multiple_of(jnp.maximum(t0 - tq, 0), tq)
    first = (j == 0).astype(I32)
    lane_o = lax.broadcasted_iota(I32, (tq, LANES), 1)

    def lane_group_max(s):
        m = s[:, :LANES]
        for g in range(1, s.shape[1] // LANES):
            m = jnp.maximum(m, s[:, g * LANES:(g + 1) * LANES])
        return m

    def pair_lanes(p, width=1):
        return slice(width * p * LANES, width * (p + 1) * LANES)

    mrow_ref[...] = jnp.full(mrow_ref.shape, NEG, F32)

    def logits_body(kt, _):
        ks = pl.multiple_of(kt * K_TILE, K_TILE)
        mbt = mbf_ref[:, pl.ds(ks, K_TILE)]
        mb2 = jnp.concatenate([mbt, mbt], axis=0)
        for p in range(n_pairs):
            s = _dot_nt(qp_ref[p], kb_ref[0, pl.ds(ks, K_TILE), pair_lanes(p)]) + mb2
            s_ref[p, :, pl.ds(ks, K_TILE)] = s
            mrow_ref[p] = jnp.maximum(mrow_ref[p], lane_group_max(s))
        return 0
    lax.fori_loop(0, n_far, logits_body, 0)

    mbw = mb_ref[:, pl.ds(ws, wwin)]
    mbw2 = jnp.concatenate([mbw, mbw], axis=0)
    for p in range(n_pairs):
        sw = _dot_nt(qp_ref[p], kb_ref[0, pl.ds(ws, wwin), pair_lanes(p)]) + mbw2 + biasw_ref[first, p]
        m_acc = jnp.maximum(mrow_ref[p], lane_group_max(sw))
        m_row = jnp.broadcast_to(jnp.max(m_acc, axis=-1, keepdims=True), (2 * tq, LANES))
        mrow_ref[p] = m_row
        pw = jnp.exp(sw - jnp.concatenate([m_row] * (wwin // LANES), axis=1)).astype(BF16)
        acc_ref[p] = _dot(pw, vaug_ref[pl.ds(ws, wwin), pair_lanes(p, 2)])

    def pv_body(kt, _):
        ks = pl.multiple_of(kt * K_TILE, K_TILE)
        for p in range(n_pairs):
            m_far = jnp.concatenate([mrow_ref[p]] * (K_TILE // LANES), axis=1)
            pexp = jnp.exp(s_ref[p, :, pl.ds(ks, K_TILE)] - m_far).astype(BF16)
            acc_ref[p] += _dot(pexp, vaug_ref[pl.ds(ks, K_TILE), pair_lanes(p, 2)])
        return 0
    lax.fori_loop(0, n_far, pv_body, 0)

    for p in range(n_pairs):
        a = acc_ref[p]
        o_even = a[:tq, :LANES] / a[:tq, LANES:LANES + 1]
        o_odd = a[tq:, :LANES] / a[tq:, LANES:LANES + 1]
        o_ref[:, pair_lanes(p)] = jnp.where(lane_o < ATT_DH, o_even, o_odd).astype(BF16)

    yb = _dot(o_ref[...], wbr_ref[...])
    mixed = jax.nn.sigmoid(gb_ref[...].astype(F32)) * yb + ya_ref[...].astype(F32)
    out_ref[...] = mixed.astype(BF16)


def _old_dsa_mixer(p, ya, rel_bias, w_br_att, batch, seq):
    n = batch * seq
    d = w_br_att.shape[1]
    tq = Q_TILE
    nq = seq // tq
    k_top = min(TOPK_MAX, seq // 4)
    kb3 = p["kb"].reshape(batch, seq, ATT_W)
    vb3 = p["vb"].reshape(batch, seq, ATT_W)
    kit = p["sm"][:, SM_KI:SM_KI + IDX_DH].astype(BF16).reshape(batch, seq, IDX_DH).swapaxes(1, 2)
    row = lambda b, j: (b * nq + j, 0)
    per_b = lambda b, j: (b, 0, 0)
    const = lambda b, j: (0, 0)
    return pl.pallas_call(
        functools.partial(_dsa_kernel, k_top=k_top, seq=seq),
        grid=(batch, nq),
        in_specs=[pl.BlockSpec(memory_space=pltpu.SMEM),
                  pl.BlockSpec((tq, ATT_W), row), pl.BlockSpec((tq, IDX_Q_W), row),
                  pl.BlockSpec((tq, LANES), row), pl.BlockSpec((tq, d), row),
                  pl.BlockSpec((tq, d), row),
                  pl.BlockSpec((1, seq, ATT_W), per_b, pipeline_mode=pl.Buffered(1)),
                  pl.BlockSpec((1, seq, ATT_W), per_b, pipeline_mode=pl.Buffered(1)),
                  pl.BlockSpec((1, IDX_DH, seq), per_b, pipeline_mode=pl.Buffered(1)),
                  pl.BlockSpec((ATT_W, d), const)],
        out_specs=pl.BlockSpec((tq, d), row),
        out_shape=jax.ShapeDtypeStruct((n, d), BF16),
        scratch_shapes=[pltpu.VMEM((tq, seq), F32),
                        pltpu.VMEM((tq, seq), F32),
                        pltpu.VMEM((tq, seq), F32),
                        pltpu.VMEM((seq, 2 * ATT_W), BF16),
                        pltpu.VMEM((2, ATT_HEADS // 2, 2 * tq, 2 * tq), F32),
                        pltpu.VMEM((ATT_HEADS // 2, 2 * tq, LANES), BF16),
                        pltpu.VMEM((IDX_HEADS * tq, IDX_DH), BF16),
                        pltpu.VMEM((ATT_HEADS // 2, 2 * tq, seq), F32),
                        pltpu.VMEM((ATT_HEADS // 2, 2 * tq, LANES), F32),
                        pltpu.VMEM((ATT_HEADS // 2, 2 * tq, 2 * LANES), F32),
                        pltpu.VMEM((tq, ATT_W), BF16)],
        compiler_params=pltpu.CompilerParams(dimension_semantics=("arbitrary", "arbitrary"),
                                             vmem_limit_bytes=VMEM_LIMIT),
        name="dsa_mixer",
    )(rel_bias.astype(F32).reshape(-1), p["qb"], p["qi"], p["sm"], p["gb"], ya, kb3, vb3, kit,
      w_br_att.astype(BF16))


def _dsa_kernel(relb_ref, qit_ref, wit_ref, qbt_ref, gb_ref, ya_ref, kb_ref, vbt_ref, ki_ref,
                wbr_ref, out_ref, sc_ref, mb_ref, mbf_ref, bias_ref, qp_ref, s_ref, mrow_ref,
                acc_ref, ot_ref, *, k_top):
    b = pl.program_id(0)
    j = pl.program_id(1)
    tq = Q_TILE
    t0 = j * tq
    n_pairs = ATT_HEADS // 2
    wwin = 2 * tq
    sub = 8

    @pl.when((b == 0) & (j == 0))
    def _():
        rk = lax.broadcasted_iota(I32, (wwin, tq), 0)
        rq = lax.broadcasted_iota(I32, (wwin, tq), 1)
        bucket = _t5_bucket_int(rk - tq - rq)
        far = REL_BUCKETS // 2 - 1
        for h in range(ATT_HEADS):
            a = jnp.zeros((wwin, tq), F32)
            for bk in range(REL_BUCKETS):
                a = jnp.where(bucket == bk, relb_ref[bk * ATT_HEADS + h], a)
            a = (a - relb_ref[far * ATT_HEADS + h]) * LOG2_E
            cols = slice((h % 2) * tq, (h % 2 + 1) * tq)
            bias_ref[0, h // 2, :, cols] = a
            bias_ref[1, h // 2, :, cols] = jnp.concatenate([a[tq:], jnp.zeros((tq, tq), F32)], axis=0)

    row_q = lax.broadcasted_iota(I32, (LANES, tq), 0)
    for p in range(n_pairs):
        qpair = qbt_ref[0, p * LANES:(p + 1) * LANES, :]
        zero = jnp.zeros_like(qpair)
        qp_ref[p] = jnp.concatenate([jnp.where(row_q < ATT_DH, qpair, zero),
                                     jnp.where(row_q >= ATT_DH, qpair, zero)], axis=1)
    wi = wit_ref[0] * (IDX_HEADS ** -0.5 * IDX_DH ** -0.5)

    n_sel = (j + 2) // 2
    row_s = lax.broadcasted_iota(I32, (S_TILE, tq), 0)
    lane_s = lax.broadcasted_iota(I32, (S_TILE, tq), 1)
    limit = t0 + ((lane_s >> 6) + 1) * CHUNK

    def score_body(kt, _):
        ks = pl.multiple_of(kt * S_TILE, S_TILE)
        s_all = _dot(ki_ref[0, pl.ds(ks, S_TILE), :], qit_ref[0])
        score = jnp.zeros((S_TILE, tq), F32)
        for h in range(IDX_HEADS):
            score = score + jnp.maximum(s_all[:, h * tq:(h + 1) * tq], 0.0) * wi[h:h + 1, :]
        sc_ref[pl.ds(ks, S_TILE), :] = jnp.where(ks + row_s < limit, score, -jnp.inf)
        return 0
    lax.fori_loop(0, n_sel, score_body, 0)

    n_cnt = (n_sel + 1) // 2

    @pl.when(n_sel % 2 == 1)
    def _():
        sc_ref[pl.ds(pl.multiple_of(n_sel * S_TILE, S_TILE), S_TILE), :] = jnp.full((S_TILE, tq), -jnp.inf, F32)

    kf = float(k_top)
    grp_rows = 64
    c_groups = C_TILE // grp_rows

    def score_group(kt, g):
        return sc_ref[pl.ds(pl.multiple_of(kt * C_TILE + g * grp_rows, grp_rows), grp_rows), :]

    def key_sum(c):
        return jnp.sum(c, axis=0, keepdims=True)

    def count(cand, strict):
        def body(kt, c):
            for g in range(c_groups):
                sc = score_group(kt, g)
                c = c + jnp.where((sc > cand) if strict else (sc >= cand), 1.0, 0.0)
            return c
        return key_sum(lax.fori_loop(0, n_cnt, body, jnp.zeros((grp_rows, tq), F32)))

    def stats_body(kt, carry):
        lo, hi, n_fin, n_ge0, n_gt0 = carry
        for g in range(c_groups):
            sc = score_group(kt, g)
            fin = sc > -jnp.inf
            hi = jnp.maximum(hi, sc)
            lo = jnp.minimum(lo, jnp.where(fin, sc, jnp.inf))
            n_fin = n_fin + jnp.where(fin, 1.0, 0.0)
            n_ge0 = n_ge0 + jnp.where(sc >= 0.0, 1.0, 0.0)
            n_gt0 = n_gt0 + jnp.where(sc > 0.0, 1.0, 0.0)
        return lo, hi, n_fin, n_ge0, n_gt0
    zeros_g = jnp.zeros((grp_rows, tq), F32)
    lo_g, hi_g, fin_g, ge0_g, gt0_g = lax.fori_loop(
        0, n_cnt, stats_body,
        (jnp.full((grp_rows, tq), jnp.inf, F32), jnp.full((grp_rows, tq), -jnp.inf, F32),
         zeros_g, zeros_g, zeros_g))
    row_min = jnp.min(lo_g, axis=0, keepdims=True)
    row_max = jnp.max(hi_g, axis=0, keepdims=True)
    n_adm = key_sum(fin_g)
    c_ge0 = key_sum(ge0_g)
    c_gt0 = key_sum(gt0_g)
    has_thr = n_adm >= kf
    c_max = count(row_max, False)
    at_max = c_max >= kf
    at_zero = (c_gt0 < kf) & (c_ge0 >= kf)
    above_zero = c_gt0 >= kf
    lo0 = jnp.where(at_max, row_max, jnp.where(at_zero | above_zero, 0.0, row_min))
    cnt0 = jnp.where(at_max, c_max, jnp.where(at_zero | above_zero, c_ge0, n_adm))
    hi0 = jnp.where(at_zero | above_zero, row_max, 0.0)
    done0 = jnp.logical_not(has_thr) | at_max | at_zero | (cnt0 == kf)
    pre_done = jnp.where(done0, 1.0, 0.0)

    def bisect_step(lo, hi, cnt_lo):
        mid = 0.5 * lo + 0.5 * hi
        open_ = (mid > lo) & (mid < hi)
        c = count(mid, False)
        ge = c >= kf
        lo = jnp.where(ge, mid, lo)
        cnt_lo = jnp.where(ge, c, cnt_lo)
        hi = jnp.where(ge, hi, mid)
        return lo, hi, cnt_lo, jnp.where(open_ & (cnt_lo != kf), pre_done, 1.0)

    def bisect_cond(carry):
        return carry[3] < 0.5

    def bisect_body(carry):
        lo, hi, cnt_lo, _ = carry
        for _ in range(BISECT_STEPS):
            lo, hi, cnt_lo, conv = bisect_step(lo, hi, cnt_lo)
        return lo, hi, cnt_lo, jnp.min(conv)
    thr, _, cnt_thr, _ = lax.while_loop(
        bisect_cond, bisect_body, (lo0, hi0, cnt0, jnp.min(pre_done)))

    tie = has_thr & (cnt_thr > kf)
    any_tie = jnp.max(jnp.where(tie, 1.0, 0.0)) > 0.0
    far_end = t0 - tq
    f32_min = float(jnp.finfo(F32).min)

    def write_mask(ks, sel):
        pos = ks + lax.broadcasted_iota(I32, (LANES, tq), 0)
        mb_ref[pl.ds(ks, LANES), :] = jnp.where(sel, 0.0, NEG)
        mbf_ref[pl.ds(ks, LANES), :] = jnp.where(sel & (pos < far_end), 0.0, NEG)

    @pl.when(jnp.logical_not(any_tie))
    def _():
        lo = jnp.where(has_thr, thr, f32_min)

        def body(kt, _):
            ks = pl.multiple_of(kt * LANES, LANES)
            write_mask(ks, sc_ref[pl.ds(ks, LANES), :] >= lo)
            return 0
        lax.fori_loop(0, n_sel * (S_TILE // LANES), body, 0)

    @pl.when(any_tie)
    def _():
        need = kf - count(thr, True)
        lo = jnp.where(has_thr, thr, -jnp.inf)
        ri = lax.broadcasted_iota(I32, (LANES, LANES), 0)
        ci = lax.broadcasted_iota(I32, (LANES, LANES), 1)
        tri = jnp.where(ci <= ri, 1.0, 0.0).astype(BF16)

        def body(kt, seen):
            ks = pl.multiple_of(kt * LANES, LANES)
            sc = sc_ref[pl.ds(ks, LANES), :]
            eq = (sc == thr) & has_thr
            prefix = seen + _dot(tri, jnp.where(eq, 1.0, 0.0).astype(BF16))
            write_mask(ks, (sc > lo) | (eq & (prefix <= need)))
            return prefix[LANES - 1:LANES, :]
        lax.fori_loop(0, n_sel * (S_TILE // LANES), body, jnp.zeros((1, tq), F32))

    n_far = (j + 2) // 4
    ws = pl.multiple_of(jnp.maximum(t0 - tq, 0), tq)
    first = (j == 0).astype(I32)
    ones_rows = 16

    def pair_lanes(p):
        return slice(p * LANES, (p + 1) * LANES)

    def group_max(s):
        return jnp.max(s.reshape(s.shape[0] // sub, sub, s.shape[1]), axis=0)

    def values_t(p, ks, width):
        return jnp.concatenate([vbt_ref[0, pair_lanes(p), pl.ds(ks, width)],
                                jnp.ones((ones_rows, width), BF16)], axis=0)

    mrow_ref[...] = jnp.full(mrow_ref.shape, NEG, F32)

    def logits_body(kt, _):
        ks = pl.multiple_of(kt * K_TILE, K_TILE)
        mbt = mbf_ref[pl.ds(ks, K_TILE), :]
        mb2 = jnp.concatenate([mbt, mbt], axis=1)
        for p in range(n_pairs):
            s = _dot(kb_ref[0, pl.ds(ks, K_TILE), pair_lanes(p)], qp_ref[p]) + mb2
            s_ref[p, pl.ds(ks, K_TILE), :] = s
            mrow_ref[p] = jnp.maximum(mrow_ref[p], group_max(s))
        return 0
    lax.fori_loop(0, n_far, logits_body, 0)

    mbw = mb_ref[pl.ds(ws, wwin), :]
    mbw2 = jnp.concatenate([mbw, mbw], axis=1)
    for p in range(n_pairs):
        sw = _dot(kb_ref[0, pl.ds(ws, wwin), pair_lanes(p)], qp_ref[p]) + mbw2 + bias_ref[first, p]
        m8 = jnp.maximum(mrow_ref[p], group_max(sw))
        m1 = jnp.max(m8, axis=0, keepdims=True)
        mrow_ref[p] = jnp.broadcast_to(m1, (sub, 2 * tq))
        pw = jnp.exp2(sw - m1).astype(BF16)
        acc_ref[p] = _dot(values_t(p, ws, wwin), pw)

    def pv_body(kt, _):
        ks = pl.multiple_of(kt * K_TILE, K_TILE)
        for p in range(n_pairs):
            pexp = jnp.exp2(s_ref[p, pl.ds(ks, K_TILE), :] - mrow_ref[p, 0:1, :]).astype(BF16)
            acc_ref[p] += _dot(values_t(p, ks, K_TILE), pexp)
        return 0
    lax.fori_loop(0, n_far, pv_body, 0)

    row_o = lax.broadcasted_iota(I32, (LANES, tq), 0)
    for p in range(n_pairs):
        a = acc_ref[p]
        o = a[:LANES, :] / a[LANES:LANES + 1, :]
        ot_ref[pair_lanes(p), :] = jnp.where(row_o < ATT_DH, o[:, :tq], o[:, tq:]).astype(BF16)

    yb = _dot_tn(ot_ref[...], wbr_ref[...])
    mixed = jax.nn.sigmoid(gb_ref[...].astype(F32)) * yb + ya_ref[...].astype(F32)
    out_ref[...] = mixed.astype(BF16)


def _dsa_mixer(p, ya, rel_bias, w_br_att, batch, seq):
    n = batch * seq
    d = w_br_att.shape[1]
    tq = Q_TILE
    nq = seq // tq
    nb = n // tq
    k_top = min(TOPK_MAX, seq // 4)
    qit, wit, qbt, vbt = p["qi"], p["wi"], p["qb"], p["vb"]
    kb3 = p["kb"].reshape(batch, seq, ATT_W)
    ki3 = p["ki"].reshape(batch, seq, IDX_DH)
    blk = lambda b, j: (b * nq + j, 0, 0)
    row = lambda b, j: (b * nq + j, 0)
    per_b = lambda b, j: (b, 0, 0)
    const = lambda b, j: (0, 0)
    once = pl.Buffered(1)
    return pl.pallas_call(
        functools.partial(_dsa_kernel, k_top=k_top),
        grid=(batch, nq),
        in_specs=[pl.BlockSpec(memory_space=pltpu.SMEM),
                  pl.BlockSpec((1, IDX_DH, IDX_HEADS * tq), blk),
                  pl.BlockSpec((1, IDX_HEADS, tq), blk),
                  pl.BlockSpec((1, ATT_W, tq), blk),
                  pl.BlockSpec((tq, d), row), pl.BlockSpec((tq, d), row),
                  pl.BlockSpec((1, seq, ATT_W), per_b, pipeline_mode=once),
                  pl.BlockSpec((1, ATT_W, seq), per_b, pipeline_mode=once),
                  pl.BlockSpec((1, seq, IDX_DH), per_b, pipeline_mode=once),
                  pl.BlockSpec((ATT_W, d), const)],
        out_specs=pl.BlockSpec((tq, d), row),
        out_shape=jax.ShapeDtypeStruct((n, d), BF16),
        scratch_shapes=[pltpu.VMEM((seq, tq), F32),
                        pltpu.VMEM((seq, tq), F32),
                        pltpu.VMEM((seq, tq), F32),
                        pltpu.VMEM((2, ATT_HEADS // 2, 2 * tq, 2 * tq), F32),
                        pltpu.VMEM((ATT_HEADS // 2, LANES, 2 * tq), BF16),
                        pltpu.VMEM((ATT_HEADS // 2, seq, 2 * tq), F32),
                        pltpu.VMEM((ATT_HEADS // 2, 8, 2 * tq), F32),
                        pltpu.VMEM((ATT_HEADS // 2, LANES + 16, 2 * tq), F32),
                        pltpu.VMEM((ATT_W, tq), BF16)],
        compiler_params=pltpu.CompilerParams(dimension_semantics=("arbitrary", "arbitrary"),
                                             vmem_limit_bytes=VMEM_LIMIT),
        name="dsa_mixer",
    )(rel_bias.astype(F32).reshape(-1), qit, wit, qbt, p["gb"], ya, kb3, vbt, ki3,
      w_br_att.astype(BF16))


RT_G0 = 0
RT_E0 = N_GROUPS


def _moe_kernel(x_ref, mx_ref, wo_ref, gf_ref, wrh_ref, wrl_ref, br_ref, wg_ref, wu_ref,
                wd_ref, out_ref, h2_ref):
    tm = x_ref.shape[0]
    x1 = x_ref[...] + _dot(mx_ref[...], wo_ref[...])
    out_ref[...] = x1
    ms = jnp.mean(x1 * x1, axis=-1, keepdims=True)
    h = x1 * lax.rsqrt(ms + EPS) * gf_ref[...]
    h_hi, h_lo = _split_bf16(h)
    h2_ref[...] = h_hi
    logits = (_dot(h_hi, wrh_ref[...]) + _dot(h_lo, wrh_ref[...]) + _dot(h_hi, wrl_ref[...])
              + br_ref[...])
    lane = lax.broadcasted_iota(I32, (tm, LANES), 1)
    big = jnp.int32(LANES)
    is_g = lane < N_GROUPS
    gl = jnp.where(is_g, logits, -jnp.inf)
    gmax = jnp.max(gl, axis=-1, keepdims=True)
    gidx = jnp.min(jnp.where(gl == gmax, lane, big), axis=-1, keepdims=True)
    gsum = jnp.sum(jnp.where(is_g, jnp.exp(gl - gmax), 0.0), axis=-1, keepdims=True)
    g_w = 1.0 / gsum
    in_grp = (lane >= RT_E0) & (lane < RT_E0 + N_EXPERTS) & (((lane - RT_E0) >> 3) == gidx)
    ev = jnp.where(in_grp, logits, -jnp.inf)
    t1 = jnp.max(ev, axis=-1, keepdims=True)
    i1 = jnp.min(jnp.where(ev == t1, lane, big), axis=-1, keepdims=True)
    ev2 = jnp.where(lane == i1, -jnp.inf, ev)
    t2 = jnp.max(ev2, axis=-1, keepdims=True)
    i2 = jnp.min(jnp.where(ev2 == t2, lane, big), axis=-1, keepdims=True)
    e2 = jnp.exp(t2 - t1)
    w1 = 1.0 / (1.0 + e2)
    w2 = e2 * w1
    comb = g_w * (jnp.where(lane == i1, w1, 0.0) + jnp.where(lane == i2, w2, 0.0))

    h2 = h2_ref[...]
    per_chunk = MOE_COLS // D_EXPERT
    for grp in range(N_GROUPS):
        for c in range(EXPERTS_PER_GROUP // per_chunk):
            cols = slice(c * MOE_COLS, (c + 1) * MOE_COLS)
            hg = _dot(h2, wg_ref[grp, :, cols])
            hu = _dot(h2, wu_ref[grp, :, cols])
            e0 = RT_E0 + grp * EXPERTS_PER_GROUP + c * per_chunk
            scale = jnp.concatenate(
                [jnp.broadcast_to(comb[:, e:e + 1], (tm, D_EXPERT)) for e in range(e0, e0 + per_chunk)],
                axis=1)
            hid = (hg * jax.nn.sigmoid(hg) * hu * scale).astype(BF16)
            out_ref[...] += _dot(hid, wd_ref[grp, cols, :])


def _out_proj_moe(x2, mixed, w_out, g_ffn, w_rg, b_rg, w_re, b_re, w_gate, w_up, w_down):
    n, d = x2.shape
    tm = ROW_TILE
    gw = EXPERTS_PER_GROUP * D_EXPERT
    w_r = jnp.zeros((d, LANES), F32).at[:, RT_G0:RT_G0 + N_GROUPS].set(w_rg)
    w_r = w_r.at[:, RT_E0:RT_E0 + N_EXPERTS].set(w_re)
    wr_hi = w_r.astype(BF16)
    wr_lo = (w_r - wr_hi.astype(F32)).astype(BF16)
    b_r = jnp.zeros((1, LANES), F32).at[0, RT_G0:RT_G0 + N_GROUPS].set(b_rg)
    b_r = b_r.at[0, RT_E0:RT_E0 + N_EXPERTS].set(b_re)

    def by_group(w):
        return (w.reshape(N_GROUPS, EXPERTS_PER_GROUP, d, D_EXPERT).transpose(0, 2, 1, 3)
                .reshape(N_GROUPS, d, gw).astype(BF16))
    wg = by_group(w_gate)
    wu = by_group(w_up)
    wd = w_down.reshape(N_GROUPS, gw, d).astype(BF16)

    row = lambda i: (i, 0)
    const = lambda i: (0, 0)
    const3 = lambda i: (0, 0, 0)
    once = pl.Buffered(1)
    return pl.pallas_call(
        _moe_kernel,
        grid=(n // tm,),
        in_specs=[pl.BlockSpec((tm, d), row), pl.BlockSpec((tm, d), row),
                  pl.BlockSpec((d, d), const, pipeline_mode=once), pl.BlockSpec((1, d), const),
                  pl.BlockSpec((d, LANES), const), pl.BlockSpec((d, LANES), const),
                  pl.BlockSpec((1, LANES), const),
                  pl.BlockSpec((N_GROUPS, d, gw), const3, pipeline_mode=once),
                  pl.BlockSpec((N_GROUPS, d, gw), const3, pipeline_mode=once),
                  pl.BlockSpec((N_GROUPS, gw, d), const3, pipeline_mode=once)],
        out_specs=pl.BlockSpec((tm, d), row),
        out_shape=jax.ShapeDtypeStruct((n, d), F32),
        scratch_shapes=[pltpu.VMEM((tm, d), BF16)],
        compiler_params=pltpu.CompilerParams(dimension_semantics=("arbitrary",),
                                             vmem_limit_bytes=VMEM_LIMIT),
        name="out_proj_moe",
    )(x2, mixed, w_out.astype(BF16), g_ffn[None, :].astype(F32), wr_hi, wr_lo, b_r,
      wg, wu, wd)


def kernel(x, g_mix, w_in, w_alpha2, b_alpha, g_gla, w_br_gla, g_q, g_k, rel_bias, w_br_att, w_out, g_ffn, w_rg, b_rg, w_re, b_re, w_gate, w_up, w_down):
    batch, seq, d = x.shape
    assert seq % ROW_TILE == 0 and seq % K_TILE == 0 and (batch * seq) % ROW_TILE == 0
    x2 = x.reshape(batch * seq, d)
    for l in range(g_mix.shape[0]):
        p = _in_projection(x2, g_mix[l], w_in[l], g_q[l], g_k[l], seq)
        ya = _gla_mixer(p, w_alpha2[l], b_alpha[l], g_gla[l], w_br_gla[l], batch, seq)
        mixed = _dsa_mixer(p, ya, rel_bias, w_br_att[l], batch, seq)
        x2 = _out_proj_moe(x2, mixed, w_out[l], g_ffn[l], w_rg[l], b_rg[l], w_re[l], b_re[l],
                           w_gate[l], w_up[l], w_down[l])
    return x2.reshape(batch, seq, d)
```

```python
import functools
import math

import jax
import jax.numpy as jnp
from jax import lax
from jax.experimental import pallas as pl
from jax.experimental.pallas import tpu as pltpu

F32 = jnp.float32
BF16 = jnp.bfloat16
I32 = jnp.int32

CHUNK = 64
GLA_HEADS = 4
GLA_DK = 64
GLA_DV = 128
GLA_GATE_RANK = 16
GLA_TAU = 16.0
ATT_HEADS = 8
ATT_DH = 64
IDX_HEADS = 8
IDX_DH = 32
TOPK_MAX = 256
REL_BUCKETS = 32
REL_MAX_DIST = 128
N_GROUPS = 4
EXPERTS_PER_GROUP = 8
N_EXPERTS = N_GROUPS * EXPERTS_PER_GROUP
D_EXPERT = 128
EPS = 1e-6

GLA_QK_W = GLA_HEADS * GLA_DK
GLA_V_W = GLA_HEADS * GLA_DV
ATT_W = ATT_HEADS * ATT_DH
IDX_Q_W = IDX_HEADS * IDX_DH

LANES = 128
VMEM_LIMIT = 56 * 1024 * 1024

SM_KI = 0
SM_ALR = IDX_DH
SM_WI = IDX_DH + GLA_GATE_RANK

NEG = -1e30
LOG2_E = math.log2(math.e)

ROW_TILE = 512
Q_TILE = 128
K_TILE = 512
MOE_COLS = 256
S_TILE = 256
C_TILE = 2 * S_TILE
BISECT_STEPS = 2


def _dot(a, b):
    return jnp.dot(a, b, preferred_element_type=F32)


def _dot_nt(a, b):
    return lax.dot_general(a, b, (((1,), (1,)), ((), ())), preferred_element_type=F32)


def _dot_tn(a, b):
    return lax.dot_general(a, b, (((0,), (0,)), ((), ())), preferred_element_type=F32)


def _split_bf16(a):
    hi = a.astype(BF16)
    lo = (a - hi.astype(F32)).astype(BF16)
    return hi, lo


_PROJ = (("qa", GLA_QK_W), ("ka", GLA_QK_W), ("va", GLA_V_W), ("ra", GLA_V_W),
         ("qb", ATT_W), ("kb", ATT_W), ("vb", ATT_W), ("qi", IDX_Q_W),
         ("ga", None), ("gb", None), ("sm", LANES))


def _inproj_kernel(x_ref, g_ref, w_ref, gq_ref, gk_ref, bd_ref, *out_refs, offs):
    x = x_ref[...]
    ms = jnp.mean(x * x, axis=-1, keepdims=True)
    hn = (x * lax.rsqrt(ms + EPS) * g_ref[...]).astype(BF16)

    def proj(name):
        c0, c1 = offs[name]
        return _dot(hn, w_ref[:, c0:c1])

    def head_norm(y, gain):
        ss = _dot((y * y).astype(BF16), bd_ref[...])
        return y * lax.rsqrt(ss * (1.0 / ATT_DH) + EPS) * gain

    (qa_ref, ka_ref, va_ref, ra_ref, qbt_ref, kb_ref, vbt_ref, qit_ref, ga_ref, gb_ref, sm_ref,
     ki_ref, wit_ref) = out_refs
    tq = Q_TILE
    n_blk = x.shape[0] // tq

    def blocks_t(y):
        return [y[r * tq:(r + 1) * tq, :].T for r in range(n_blk)]

    qa_ref[...] = (proj("qa") * (GLA_DK ** -0.5)).astype(BF16)
    ka_ref[...] = proj("ka").astype(BF16)
    va_ref[...] = proj("va").astype(BF16)
    ra_ref[...] = proj("ra").astype(BF16)
    qb = head_norm(proj("qb"), gq_ref[...]) * (ATT_DH ** -0.5 * LOG2_E)
    for r, yt in enumerate(blocks_t(qb)):
        qbt_ref[r] = yt.astype(BF16)
    kb_ref[...] = head_norm(proj("kb"), gk_ref[...]).astype(BF16)
    vbt_ref[0] = jnp.concatenate(blocks_t(proj("vb")), axis=1).astype(BF16)
    for r, yt in enumerate(blocks_t(proj("qi"))):
        qit_ref[r] = jnp.concatenate(
            [yt[h * IDX_DH:(h + 1) * IDX_DH, :] for h in range(IDX_HEADS)], axis=1).astype(BF16)
    ga_ref[...] = proj("ga").astype(BF16)
    gb_ref[...] = proj("gb").astype(BF16)
    small = proj("sm")
    sm_ref[...] = small
    ki_ref[...] = small[:, SM_KI:SM_KI + IDX_DH].astype(BF16)
    for r, yt in enumerate(blocks_t(small)):
        wit_ref[r] = yt[SM_WI:SM_WI + IDX_HEADS, :]


def _in_projection(x2, g_mix, w_in, g_q, g_k, seq):
    n, d = x2.shape
    widths = {name: (d if w is None else w) for name, w in _PROJ}
    ref_order = (("qa", GLA_QK_W), ("ka", GLA_QK_W), ("va", GLA_V_W), ("ra", GLA_V_W),
                 ("alr", GLA_GATE_RANK), ("qb", ATT_W), ("kb", ATT_W), ("vb", ATT_W),
                 ("qi", IDX_Q_W), ("ki", IDX_DH), ("wi", IDX_HEADS), ("ga", d), ("gb", d))
    cols, c = {}, 0
    for name, w in ref_order:
        cols[name] = w_in[:, c:c + w]
        c += w
    small = jnp.concatenate(
        [cols["ki"], cols["alr"], cols["wi"],
         jnp.zeros((d, LANES - GLA_GATE_RANK - IDX_DH - IDX_HEADS), w_in.dtype)], axis=1)
    cols["sm"] = small
    w_cat = jnp.concatenate([cols[name] for name, _ in _PROJ], axis=1).astype(BF16)
    offs, c = {}, 0
    for name, _ in _PROJ:
        offs[name] = (c, c + widths[name])
        c += widths[name]
    d_cat = c

    head = jnp.arange(ATT_W) // ATT_DH
    blockdiag = (head[:, None] == head[None, :]).astype(BF16)
    gq = jnp.tile(g_q, ATT_HEADS)[None, :].astype(F32)
    gk = jnp.tile(g_k, ATT_HEADS)[None, :].astype(F32)

    tm = ROW_TILE
    tq = Q_TILE
    nt = seq // tm
    nb = n // tq
    const = lambda i: (0, 0)
    row = lambda i: (i, 0)
    blk = lambda i: (i, 0, 0)
    shapes = {name: ((n, widths[name]), F32 if name == "sm" else BF16, (tm, widths[name]), row)
              for name, _ in _PROJ}
    shapes["qb"] = ((nb, ATT_W, tq), BF16, (tm // tq, ATT_W, tq), blk)
    shapes["vb"] = ((n // seq, ATT_W, seq), BF16, (1, ATT_W, tm), lambda i: (i // nt, 0, i % nt))
    shapes["qi"] = ((nb, IDX_DH, IDX_HEADS * tq), BF16, (tm // tq, IDX_DH, IDX_HEADS * tq), blk)
    names = [name for name, _ in _PROJ] + ["ki", "wi"]
    shapes["ki"] = ((n, IDX_DH), BF16, (tm, IDX_DH), row)
    shapes["wi"] = ((nb, IDX_HEADS, tq), F32, (tm // tq, IDX_HEADS, tq), blk)
    out_shape = [jax.ShapeDtypeStruct(shapes[k][0], shapes[k][1]) for k in names]
    out_specs = [pl.BlockSpec(shapes[k][2], shapes[k][3]) for k in names]
    outs = pl.pallas_call(
        functools.partial(_inproj_kernel, offs=offs),
        grid=(n // tm,),
        in_specs=[pl.BlockSpec((tm, d), lambda i: (i, 0)),
                  pl.BlockSpec((1, d), const),
                  pl.BlockSpec((d, d_cat), const),
                  pl.BlockSpec((1, ATT_W), const),
                  pl.BlockSpec((1, ATT_W), const),
                  pl.BlockSpec((ATT_W, ATT_W), const)],
        out_specs=out_specs,
        out_shape=out_shape,
        compiler_params=pltpu.CompilerParams(dimension_semantics=("arbitrary",),
                                             vmem_limit_bytes=VMEM_LIMIT),
        name="in_projection",
    )(x2, g_mix[None, :].astype(F32), w_cat, gq, gk, blockdiag)
    return dict(zip(names, outs))


def _gla_kernel(qa_ref, ka_ref, va_ref, ra_ref, sm_ref, ga_ref, wa2_ref, ba_ref, ltri_ref,
                lall_ref, gg_ref, wbr_ref, out_ref, st_ref, o_ref):
    @pl.when(pl.program_id(1) == 0)
    def _():
        st_ref[...] = jnp.zeros_like(st_ref)

    tb = qa_ref.shape[0]
    z = _dot(sm_ref[...].astype(BF16), wa2_ref[...]) + ba_ref[...]
    log_a = (jnp.minimum(z, 0.0) - jnp.log1p(jnp.exp(-jnp.abs(z)))) * (1.0 / GLA_TAU)
    la_hi, la_lo = _split_bf16(log_a)
    ltri = ltri_ref[...]
    lall = lall_ref[...]
    cum = _dot(ltri, la_hi) + _dot(ltri, la_lo)
    tot = _dot(lall, la_hi) + _dot(lall, la_lo)
    k_dec = (ka_ref[...].astype(F32) * jnp.exp(tot - cum)).astype(BF16)
    dec = jnp.exp(tot)

    states = [st_ref[h] for h in range(GLA_HEADS)]
    for c in range(tb // CHUNK):
        rows = slice(c * CHUNK, (c + 1) * CHUNK)
        for h in range(GLA_HEADS):
            kl = slice(h * GLA_DK, (h + 1) * GLA_DK)
            vl = slice(h * GLA_DV, (h + 1) * GLA_DV)
            u_t = _dot_tn(va_ref[rows, vl], k_dec[rows, kl])
            s = states[h] * dec[c * CHUNK:c * CHUNK + 1, kl] + u_t
            states[h] = s
            o_ref[rows, vl] = _dot_nt(qa_ref[rows, kl], s.astype(BF16))
    for h in range(GLA_HEADS):
        st_ref[h] = states[h]

    r = ra_ref[...].astype(F32)
    gated = []
    for h in range(GLA_HEADS):
        vl = slice(h * GLA_DV, (h + 1) * GLA_DV)
        oh = o_ref[:, vl]
        ms = jnp.mean(oh * oh, axis=-1, keepdims=True)
        oh = oh * lax.rsqrt(ms + EPS) * gg_ref[...]
        rh = r[:, vl]
        gated.append((oh * (rh * jax.nn.sigmoid(rh))).astype(BF16))
    og = jnp.concatenate(gated, axis=1)
    ya = _dot(og, wbr_ref[...])
    out_ref[...] = (jax.nn.sigmoid(ga_ref[...].astype(F32)) * ya).astype(BF16)


def _gla_mixer(p, w_alpha2, b_alpha, g_gla, w_br_gla, batch, seq):
    n = batch * seq
    d = w_br_gla.shape[1]
    tb = ROW_TILE
    nt = seq // tb
    wa2 = jnp.zeros((LANES, GLA_QK_W), F32).at[SM_ALR:SM_ALR + GLA_GATE_RANK].set(w_alpha2).astype(BF16)
    r = jnp.arange(tb)
    same = (r[:, None] // CHUNK) == (r[None, :] // CHUNK)
    ltri = (same & (r[None, :] <= r[:, None])).astype(BF16)
    lall = same.astype(BF16)
    row = lambda b, i: (b * nt + i, 0)
    const = lambda b, i: (0, 0)
    return pl.pallas_call(
        _gla_kernel,
        grid=(batch, nt),
        in_specs=[pl.BlockSpec((tb, GLA_QK_W), row), pl.BlockSpec((tb, GLA_QK_W), row),
                  pl.BlockSpec((tb, GLA_V_W), row), pl.BlockSpec((tb, GLA_V_W), row),
                  pl.BlockSpec((tb, LANES), row), pl.BlockSpec((tb, d), row),
                  pl.BlockSpec((LANES, GLA_QK_W), const), pl.BlockSpec((1, GLA_QK_W), const),
                  pl.BlockSpec((tb, tb), const), pl.BlockSpec((tb, tb), const),
                  pl.BlockSpec((1, GLA_DV), const), pl.BlockSpec((GLA_V_W, d), const)],
        out_specs=pl.BlockSpec((tb, d), row),
        out_shape=jax.ShapeDtypeStruct((n, d), BF16),
        scratch_shapes=[pltpu.VMEM((GLA_HEADS, GLA_DV, GLA_DK), F32),
                        pltpu.VMEM((tb, GLA_V_W), F32)],
        compiler_params=pltpu.CompilerParams(dimension_semantics=("arbitrary", "arbitrary"),
                                             vmem_limit_bytes=VMEM_LIMIT),
        name="gla_mixer",
    )(p["qa"], p["ka"], p["va"], p["ra"], p["sm"], p["ga"], wa2, b_alpha[None, :].astype(F32),
      ltri, lall, g_gla[None, :].astype(F32), w_br_gla.astype(BF16))


def _t5_bucket_int(rel):
    half = REL_BUCKETS // 2
    exact = half // 2
    n = jnp.abs(rel)
    large = jnp.full(rel.shape, exact, I32)
    for j in range(1, half - exact):
        thr = math.ceil(exact * (REL_MAX_DIST / exact) ** (j / (half - exact)) - 1e-9)
        large = large + jnp.where(n >= thr, 1, 0)
    return jnp.where(rel > 0, half, 0) + jnp.where(n < exact, n, large)


def _dsa_kernel(relb_ref, qit_ref, wit_ref, qbt_ref, gb_ref, ya_ref, kb_ref, vbt_ref, ki_ref,
                wbr_ref, out_ref, sc_ref, mb_ref, mbf_ref, bias_ref, qp_ref, s_ref, mrow_ref, alpha_ref,
                acc_ref, ot_ref, *, k_top):
    b = pl.program_id(0)
    j = pl.program_id(1)
    tq = Q_TILE
    t0 = j * tq
    n_pairs = ATT_HEADS // 2
    wwin = 2 * tq
    sub = 8

    @pl.when((b == 0) & (j == 0))
    def _():
        rk = lax.broadcasted_iota(I32, (wwin, tq), 0)
        rq = lax.broadcasted_iota(I32, (wwin, tq), 1)
        bucket = _t5_bucket_int(rk - tq - rq)
        far = REL_BUCKETS // 2 - 1
        for h in range(ATT_HEADS):
            a = jnp.zeros((wwin, tq), F32)
            for bk in range(REL_BUCKETS):
                a = jnp.where(bucket == bk, relb_ref[bk * ATT_HEADS + h], a)
            a = (a - relb_ref[far * ATT_HEADS + h]) * LOG2_E
            cols = slice((h % 2) * tq, (h % 2 + 1) * tq)
            bias_ref[0, h // 2, :, cols] = a
            bias_ref[1, h // 2, :, cols] = jnp.concatenate([a[tq:], jnp.zeros((tq, tq), F32)], axis=0)

    row_q = lax.broadcasted_iota(I32, (LANES, tq), 0)
    for p in range(n_pairs):
        qpair = qbt_ref[0, p * LANES:(p + 1) * LANES, :]
        zero = jnp.zeros_like(qpair)
        qp_ref[p] = jnp.concatenate([jnp.where(row_q < ATT_DH, qpair, zero),
                                     jnp.where(row_q >= ATT_DH, qpair, zero)], axis=1)
    wi = wit_ref[0] * (IDX_HEADS ** -0.5 * IDX_DH ** -0.5)

    n_sel = (j + 2) // 2
    row_s = lax.broadcasted_iota(I32, (S_TILE, tq), 0)
    lane_s = lax.broadcasted_iota(I32, (S_TILE, tq), 1)
    limit = t0 + ((lane_s >> 6) + 1) * CHUNK

    def score_body(kt, _):
        ks = pl.multiple_of(kt * S_TILE, S_TILE)
        s_all = _dot(ki_ref[0, pl.ds(ks, S_TILE), :], qit_ref[0])
        score = jnp.zeros((S_TILE, tq), F32)
        for h in range(IDX_HEADS):
            score = score + jnp.maximum(s_all[:, h * tq:(h + 1) * tq], 0.0) * wi[h:h + 1, :]
        sc_ref[pl.ds(ks, S_TILE), :] = jnp.where(ks + row_s < limit, score, -jnp.inf)
        return 0
    lax.fori_loop(0, n_sel, score_body, 0)

    n_cnt = (n_sel + 1) // 2

    @pl.when(n_sel % 2 == 1)
    def _():
        sc_ref[pl.ds(pl.multiple_of(n_sel * S_TILE, S_TILE), S_TILE), :] = jnp.full((S_TILE, tq), -jnp.inf, F32)

    kf = float(k_top)
    grp_rows = 64
    c_groups = C_TILE // grp_rows

    def score_group(kt, g):
        return sc_ref[pl.ds(pl.multiple_of(kt * C_TILE + g * grp_rows, grp_rows), grp_rows), :]

    def key_sum(c):
        return jnp.sum(c, axis=0, keepdims=True)

    def count(cand, strict):
        def body(kt, c):
            for g in range(c_groups):
                sc = score_group(kt, g)
                c = c + jnp.where((sc > cand) if strict else (sc >= cand), 1.0, 0.0)
            return c
        return key_sum(lax.fori_loop(0, n_cnt, body, jnp.zeros((grp_rows, tq), F32)))

    def stats_body(kt, carry):
        lo, hi, n_fin, n_ge0, n_gt0 = carry
        for g in range(c_groups):
            sc = score_group(kt, g)
            fin = sc > -jnp.inf
            hi = jnp.maximum(hi, sc)
            lo = jnp.minimum(lo, jnp.where(fin, sc, jnp.inf))
            n_fin = n_fin + jnp.where(fin, 1.0, 0.0)
            n_ge0 = n_ge0 + jnp.where(sc >= 0.0, 1.0, 0.0)
            n_gt0 = n_gt0 + jnp.where(sc > 0.0, 1.0, 0.0)
        return lo, hi, n_fin, n_ge0, n_gt0
    zeros_g = jnp.zeros((grp_rows, tq), F32)
    lo_g, hi_g, fin_g, ge0_g, gt0_g = lax.fori_loop(
        0, n_cnt, stats_body,
        (jnp.full((grp_rows, tq), jnp.inf, F32), jnp.full((grp_rows, tq), -jnp.inf, F32),
         zeros_g, zeros_g, zeros_g))
    row_min = jnp.min(lo_g, axis=0, keepdims=True)
    row_max = jnp.max(hi_g, axis=0, keepdims=True)
    n_adm = key_sum(fin_g)
    c_ge0 = key_sum(ge0_g)
    c_gt0 = key_sum(gt0_g)
    has_thr = n_adm >= kf
    c_max = count(row_max, False)
    at_max = c_max >= kf
    at_zero = (c_gt0 < kf) & (c_ge0 >= kf)
    above_zero = c_gt0 >= kf
    lo0 = jnp.where(at_max, row_max, jnp.where(at_zero | above_zero, 0.0, row_min))
    cnt0 = jnp.where(at_max, c_max, jnp.where(at_zero | above_zero, c_ge0, n_adm))
    hi0 = jnp.where(at_zero | above_zero, row_max, 0.0)
    done0 = jnp.logical_not(has_thr) | at_max | at_zero | (cnt0 == kf)
    pre_done = jnp.where(done0, 1.0, 0.0)

    def bisect_step(lo, hi, cnt_lo):
        mid = 0.5 * lo + 0.5 * hi
        open_ = (mid > lo) & (mid < hi)
        c = count(mid, False)
        ge = c >= kf
        lo = jnp.where(ge, mid, lo)
        cnt_lo = jnp.where(ge, c, cnt_lo)
        hi = jnp.where(ge, hi, mid)
        return lo, hi, cnt_lo, jnp.where(open_ & (cnt_lo != kf), pre_done, 1.0)

    def bisect_cond(carry):
        return carry[3] < 0.5

    def bisect_body(carry):
        lo, hi, cnt_lo, _ = carry
        for _ in range(BISECT_STEPS):
            lo, hi, cnt_lo, conv = bisect_step(lo, hi, cnt_lo)
        return lo, hi, cnt_lo, jnp.min(conv)
    thr, _, cnt_thr, _ = lax.while_loop(
        bisect_cond, bisect_body, (lo0, hi0, cnt0, jnp.min(pre_done)))

    tie = has_thr & (cnt_thr > kf)
    any_tie = jnp.max(jnp.where(tie, 1.0, 0.0)) > 0.0
    far_end = t0 - tq
    f32_min = float(jnp.finfo(F32).min)

    def write_mask(ks, width, sel):
        pos = ks + lax.broadcasted_iota(I32, (width, tq), 0)
        mb_ref[pl.ds(ks, width), :] = jnp.where(sel, 0.0, NEG)
        mbf_ref[pl.ds(ks, width), :] = jnp.where(sel & (pos < far_end), 0.0, NEG)

    @pl.when(jnp.logical_not(any_tie))
    def _():
        lo = jnp.where(has_thr, thr, f32_min)

        def body(kt, _):
            ks = pl.multiple_of(kt * S_TILE, S_TILE)
            write_mask(ks, S_TILE, sc_ref[pl.ds(ks, S_TILE), :] >= lo)
            return 0
        lax.fori_loop(0, n_sel, body, 0)

    @pl.when(any_tie)
    def _():
        need = kf - count(thr, True)
        lo = jnp.where(has_thr, thr, -jnp.inf)
        ri = lax.broadcasted_iota(I32, (C_TILE, C_TILE), 0)
        ci = lax.broadcasted_iota(I32, (C_TILE, C_TILE), 1)
        tri = jnp.where(ci <= ri, 1.0, 0.0).astype(BF16)

        def body(kt, seen):
            ks = pl.multiple_of(kt * C_TILE, C_TILE)
            sc = sc_ref[pl.ds(ks, C_TILE), :]
            eq = (sc == thr) & has_thr
            eq_f = jnp.where(eq, 1.0, 0.0)
            prefix = seen + _dot(tri, eq_f.astype(BF16))
            write_mask(ks, C_TILE, (sc > lo) | (eq & (prefix <= need)))
            return seen + key_sum(eq_f)
        lax.fori_loop(0, n_cnt, body, jnp.zeros((1, tq), F32))

    n_far = (j + 2) // 4
    ws = pl.multiple_of(jnp.maximum(t0 - tq, 0), tq)
    first = (j == 0).astype(I32)
    ones_rows = 16

    def pair_lanes(p):
        return slice(p * LANES, (p + 1) * LANES)

    def key_max(s):
        part = jnp.max(s.reshape(s.shape[0] // sub, sub, s.shape[1]), axis=0)
        return jnp.max(part, axis=0, keepdims=True)

    def values_t(p, ks, width):
        return jnp.concatenate([vbt_ref[0, pair_lanes(p), pl.ds(ks, width)],
                                jnp.ones((ones_rows, width), BF16)], axis=0)

    mrow_ref[...] = jnp.full(mrow_ref.shape, NEG, F32)
    acc_ref[...] = jnp.zeros_like(acc_ref)

    def logits_stage(slot, ks, width, mask_ref, bias):
        mbt = mask_ref[pl.ds(ks, width), :]
        mb2 = jnp.concatenate([mbt, mbt], axis=1)
        for p in range(n_pairs):
            s = _dot(kb_ref[0, pl.ds(ks, width), pair_lanes(p)], qp_ref[p]) + mb2
            if bias is not None:
                s = s + bias(p)
            s_ref[slot, p, 0:width, :] = s
            m_old = mrow_ref[p]
            m_new = jnp.maximum(m_old, key_max(s))
            alpha_ref[p] = jnp.exp2(m_old - m_new)
            mrow_ref[p] = m_new

    def values_stage(slot, ks, width):
        for p in range(n_pairs):
            pexp = jnp.exp2(s_ref[slot, p, 0:width, :] - mrow_ref[p]).astype(BF16)
            acc_ref[p] = acc_ref[p] * alpha_ref[p] + _dot(values_t(p, ks, width), pexp)

    def far_start(kt):
        return pl.multiple_of(kt * K_TILE, K_TILE)

    def window_logits():
        logits_stage(n_far % 2, ws, wwin, mb_ref, lambda p: bias_ref[first, p])

    @pl.when(n_far > 0)
    def _():
        logits_stage(0, far_start(0), K_TILE, mbf_ref, None)

        def body(kt, _):
            values_stage((kt - 1) % 2, far_start(kt - 1), K_TILE)
            logits_stage(kt % 2, far_start(kt), K_TILE, mbf_ref, None)
            return 0
        lax.fori_loop(1, n_far, body, 0)
        values_stage((n_far - 1) % 2, far_start(n_far - 1), K_TILE)
        window_logits()

    @pl.when(n_far == 0)
    def _():
        window_logits()
    values_stage(n_far % 2, ws, wwin)

    row_o = lax.broadcasted_iota(I32, (LANES, tq), 0)
    for p in range(n_pairs):
        a = acc_ref[p]
        o = a[:LANES, :] / a[LANES:LANES + 1, :]
        ot_ref[pair_lanes(p), :] = jnp.where(row_o < ATT_DH, o[:, :tq], o[:, tq:]).astype(BF16)

    yb = _dot_tn(ot_ref[...], wbr_ref[...])
    mixed = jax.nn.sigmoid(gb_ref[...].astype(F32)) * yb + ya_ref[...].astype(F32)
    out_ref[...] = mixed.astype(BF16)


def _dsa_mixer(p, ya, rel_bias, w_br_att, batch, seq):
    n = batch * seq
    d = w_br_att.shape[1]
    tq = Q_TILE
    nq = seq // tq
    nb = n // tq
    k_top = min(TOPK_MAX, seq // 4)
    qit, wit, qbt, vbt = p["qi"], p["wi"], p["qb"], p["vb"]
    kb3 = p["kb"].reshape(batch, seq, ATT_W)
    ki3 = p["ki"].reshape(batch, seq, IDX_DH)
    blk = lambda b, j: (b * nq + j, 0, 0)
    row = lambda b, j: (b * nq + j, 0)
    per_b = lambda b, j: (b, 0, 0)
    const = lambda b, j: (0, 0)
    once = pl.Buffered(1)
    return pl.pallas_call(
        functools.partial(_dsa_kernel, k_top=k_top),
        grid=(batch, nq),
        in_specs=[pl.BlockSpec(memory_space=pltpu.SMEM),
                  pl.BlockSpec((1, IDX_DH, IDX_HEADS * tq), blk),
                  pl.BlockSpec((1, IDX_HEADS, tq), blk),
                  pl.BlockSpec((1, ATT_W, tq), blk),
                  pl.BlockSpec((tq, d), row), pl.BlockSpec((tq, d), row),
                  pl.BlockSpec((1, seq, ATT_W), per_b, pipeline_mode=once),
                  pl.BlockSpec((1, ATT_W, seq), per_b, pipeline_mode=once),
                  pl.BlockSpec((1, seq, IDX_DH), per_b, pipeline_mode=once),
                  pl.BlockSpec((ATT_W, d), const)],
        out_specs=pl.BlockSpec((tq, d), row),
        out_shape=jax.ShapeDtypeStruct((n, d), BF16),
        scratch_shapes=[pltpu.VMEM((seq, tq), F32),
                        pltpu.VMEM((seq, tq), F32),
                        pltpu.VMEM((seq, tq), F32),
                        pltpu.VMEM((2, ATT_HEADS // 2, 2 * tq, 2 * tq), F32),
                        pltpu.VMEM((ATT_HEADS // 2, LANES, 2 * tq), BF16),
                        pltpu.VMEM((2, ATT_HEADS // 2, K_TILE, 2 * tq), F32),
                        pltpu.VMEM((ATT_HEADS // 2, 1, 2 * tq), F32),
                        pltpu.VMEM((ATT_HEADS // 2, 1, 2 * tq), F32),
                        pltpu.VMEM((ATT_HEADS // 2, LANES + 16, 2 * tq), F32),
                        pltpu.VMEM((ATT_W, tq), BF16)],
        compiler_params=pltpu.CompilerParams(dimension_semantics=("arbitrary", "arbitrary"),
                                             vmem_limit_bytes=VMEM_LIMIT),
        name="dsa_mixer",
    )(rel_bias.astype(F32).reshape(-1), qit, wit, qbt, p["gb"], ya, kb3, vbt, ki3,
      w_br_att.astype(BF16))


RT_G0 = 0
RT_E0 = N_GROUPS


def _moe_kernel(x_ref, mx_ref, wo_ref, gf_ref, wrh_ref, wrl_ref, br_ref, wg_ref, wu_ref,
                wd_ref, out_ref, h2_ref):
    tm = x_ref.shape[0]
    x1 = x_ref[...] + _dot(mx_ref[...], wo_ref[...])
    out_ref[...] = x1
    ms = jnp.mean(x1 * x1, axis=-1, keepdims=True)
    h = x1 * lax.rsqrt(ms + EPS) * gf_ref[...]
    h_hi, h_lo = _split_bf16(h)
    h2_ref[...] = h_hi
    logits = (_dot(h_hi, wrh_ref[...]) + _dot(h_lo, wrh_ref[...]) + _dot(h_hi, wrl_ref[...])
              + br_ref[...])
    lane = lax.broadcasted_iota(I32, (tm, LANES), 1)
    big = jnp.int32(LANES)
    is_g = lane < N_GROUPS
    gl = jnp.where(is_g, logits, -jnp.inf)
    gmax = jnp.max(gl, axis=-1, keepdims=True)
    gidx = jnp.min(jnp.where(gl == gmax, lane, big), axis=-1, keepdims=True)
    gsum = jnp.sum(jnp.where(is_g, jnp.exp(gl - gmax), 0.0), axis=-1, keepdims=True)
    g_w = 1.0 / gsum
    in_grp = (lane >= RT_E0) & (lane < RT_E0 + N_EXPERTS) & (((lane - RT_E0) >> 3) == gidx)
    ev = jnp.where(in_grp, logits, -jnp.inf)
    t1 = jnp.max(ev, axis=-1, keepdims=True)
    i1 = jnp.min(jnp.where(ev == t1, lane, big), axis=-1, keepdims=True)
    ev2 = jnp.where(lane == i1, -jnp.inf, ev)
    t2 = jnp.max(ev2, axis=-1, keepdims=True)
    i2 = jnp.min(jnp.where(ev2 == t2, lane, big), axis=-1, keepdims=True)
    e2 = jnp.exp(t2 - t1)
    w1 = 1.0 / (1.0 + e2)
    w2 = e2 * w1
    comb = g_w * (jnp.where(lane == i1, w1, 0.0) + jnp.where(lane == i2, w2, 0.0))

    h2 = h2_ref[...]
    per_chunk = MOE_COLS // D_EXPERT
    for grp in range(N_GROUPS):
        for c in range(EXPERTS_PER_GROUP // per_chunk):
            cols = slice(c * MOE_COLS, (c + 1) * MOE_COLS)
            hg = _dot(h2, wg_ref[grp, :, cols])
            hu = _dot(h2, wu_ref[grp, :, cols])
            e0 = RT_E0 + grp * EXPERTS_PER_GROUP + c * per_chunk
            scale = jnp.concatenate(
                [jnp.broadcast_to(comb[:, e:e + 1], (tm, D_EXPERT)) for e in range(e0, e0 + per_chunk)],
                axis=1)
            hid = (hg * jax.nn.sigmoid(hg) * hu * scale).astype(BF16)
            out_ref[...] += _dot(hid, wd_ref[grp, cols, :])


def _out_proj_moe(x2, mixed, w_out, g_ffn, w_rg, b_rg, w_re, b_re, w_gate, w_up, w_down):
    n, d = x2.shape
    tm = ROW_TILE
    gw = EXPERTS_PER_GROUP * D_EXPERT
    w_r = jnp.zeros((d, LANES), F32).at[:, RT_G0:RT_G0 + N_GROUPS].set(w_rg)
    w_r = w_r.at[:, RT_E0:RT_E0 + N_EXPERTS].set(w_re)
    wr_hi = w_r.astype(BF16)
    wr_lo = (w_r - wr_hi.astype(F32)).astype(BF16)
    b_r = jnp.zeros((1, LANES), F32).at[0, RT_G0:RT_G0 + N_GROUPS].set(b_rg)
    b_r = b_r.at[0, RT_E0:RT_E0 + N_EXPERTS].set(b_re)

    def by_group(w):
        return (w.reshape(N_GROUPS, EXPERTS_PER_GROUP, d, D_EXPERT).transpose(0, 2, 1, 3)
                .reshape(N_GROUPS, d, gw).astype(BF16))
    wg = by_group(w_gate)
    wu = by_group(w_up)
    wd = w_down.reshape(N_GROUPS, gw, d).astype(BF16)

    row = lambda i: (i, 0)
    const = lambda i: (0, 0)
    const3 = lambda i: (0, 0, 0)
    once = pl.Buffered(1)
    return pl.pallas_call(
        _moe_kernel,
        grid=(n // tm,),
        in_specs=[pl.BlockSpec((tm, d), row), pl.BlockSpec((tm, d), row),
                  pl.BlockSpec((d, d), const, pipeline_mode=once), pl.BlockSpec((1, d), const),
                  pl.BlockSpec((d, LANES), const), pl.BlockSpec((d, LANES), const),
                  pl.BlockSpec((1, LANES), const),
                  pl.BlockSpec((N_GROUPS, d, gw), const3, pipeline_mode=once),
                  pl.BlockSpec((N_GROUPS, d, gw), const3, pipeline_mode=once),
                  pl.BlockSpec((N_GROUPS, gw, d), const3, pipeline_mode=once)],
        out_specs=pl.BlockSpec((tm, d), row),
        out_shape=jax.ShapeDtypeStruct((n, d), F32),
        scratch_shapes=[pltpu.VMEM((tm, d), BF16)],
        compiler_params=pltpu.CompilerParams(dimension_semantics=("arbitrary",),
                                             vmem_limit_bytes=VMEM_LIMIT),
        name="out_proj_moe",
    )(x2, mixed, w_out.astype(BF16), g_ffn[None, :].astype(F32), wr_hi, wr_lo, b_r,
      wg, wu, wd)


def kernel(x, g_mix, w_in, w_alpha2, b_alpha, g_gla, w_br_gla, g_q, g_k, rel_bias, w_br_att, w_out, g_ffn, w_rg, b_rg, w_re, b_re, w_gate, w_up, w_down):
    batch, seq, d = x.shape
    assert seq % ROW_TILE == 0 and seq % K_TILE == 0 and (batch * seq) % ROW_TILE == 0
    x2 = x.reshape(batch * seq, d)
    for l in range(g_mix.shape[0]):
        p = _in_projection(x2, g_mix[l], w_in[l], g_q[l], g_k[l], seq)
        ya = _gla_mixer(p, w_alpha2[l], b_alpha[l], g_gla[l], w_br_gla[l], batch, seq)
        mixed = _dsa_mixer(p, ya, rel_bias, w_br_att[l], batch, seq)
        x2 = _out_proj_moe(x2, mixed, w_out[l], g_ffn[l], w_rg[l], b_rg[l], w_re[l], b_re[l],
                           w_gate[l], w_up[l], w_down[l])
    return x2.reshape(batch, seq, d)
```

```python
import functools
import math

import jax
import jax.numpy as jnp
from jax import lax
from jax.experimental import pallas as pl
from jax.experimental.pallas import tpu as pltpu

F32 = jnp.float32
BF16 = jnp.bfloat16
I32 = jnp.int32

CHUNK = 64
GLA_HEADS = 4
GLA_DK = 64
GLA_DV = 128
GLA_GATE_RANK = 16
GLA_TAU = 16.0
ATT_HEADS = 8
ATT_DH = 64
IDX_HEADS = 8
IDX_DH = 32
TOPK_MAX = 256
REL_BUCKETS = 32
REL_MAX_DIST = 128
N_GROUPS = 4
EXPERTS_PER_GROUP = 8
N_EXPERTS = N_GROUPS * EXPERTS_PER_GROUP
D_EXPERT = 128
EPS = 1e-6

GLA_QK_W = GLA_HEADS * GLA_DK
GLA_V_W = GLA_HEADS * GLA_DV
ATT_W = ATT_HEADS * ATT_DH
IDX_Q_W = IDX_HEADS * IDX_DH

LANES = 128
VMEM_LIMIT = 56 * 1024 * 1024

SM_KI = 0
SM_ALR = IDX_DH
SM_WI = IDX_DH + GLA_GATE_RANK

NEG = -1e30
LOG2_E = math.log2(math.e)

ROW_TILE = 512
Q_TILE = 256
WIN_BACK = 128
K_TILE = 512
MOE_COLS = 256
S_TILE = 256
C_TILE = 2 * S_TILE
BISECT_STEPS = 2


def _dot(a, b):
    return jnp.dot(a, b, preferred_element_type=F32)


def _dot_nt(a, b):
    return lax.dot_general(a, b, (((1,), (1,)), ((), ())), preferred_element_type=F32)


def _dot_tn(a, b):
    return lax.dot_general(a, b, (((0,), (0,)), ((), ())), preferred_element_type=F32)


def _split_bf16(a):
    hi = a.astype(BF16)
    lo = (a - hi.astype(F32)).astype(BF16)
    return hi, lo


_PROJ = (("qa", GLA_QK_W), ("ka", GLA_QK_W), ("va", GLA_V_W), ("ra", GLA_V_W),
         ("qb", ATT_W), ("kb", ATT_W), ("vb", ATT_W), ("qi", IDX_Q_W),
         ("ga", None), ("gb", None), ("sm", LANES))


def _inproj_kernel(x_ref, g_ref, w_ref, gq_ref, gk_ref, bd_ref, *out_refs, offs):
    x = x_ref[...]
    ms = jnp.mean(x * x, axis=-1, keepdims=True)
    hn = (x * lax.rsqrt(ms + EPS) * g_ref[...]).astype(BF16)

    def proj(name):
        c0, c1 = offs[name]
        return _dot(hn, w_ref[:, c0:c1])

    def head_norm(y, gain):
        ss = _dot((y * y).astype(BF16), bd_ref[...])
        return y * lax.rsqrt(ss * (1.0 / ATT_DH) + EPS) * gain

    (qa_ref, ka_ref, va_ref, ra_ref, qbt_ref, kb_ref, vbt_ref, qit_ref, ga_ref, gb_ref, sm_ref,
     ki_ref, wit_ref) = out_refs
    tq = Q_TILE
    n_blk = x.shape[0] // tq

    def blocks_t(y):
        return [y[r * tq:(r + 1) * tq, :].T for r in range(n_blk)]

    qa_ref[...] = (proj("qa") * (GLA_DK ** -0.5)).astype(BF16)
    ka_ref[...] = proj("ka").astype(BF16)
    va_ref[...] = proj("va").astype(BF16)
    ra_ref[...] = proj("ra").astype(BF16)
    qb = head_norm(proj("qb"), gq_ref[...]) * (ATT_DH ** -0.5 * LOG2_E)
    for r, yt in enumerate(blocks_t(qb)):
        qbt_ref[r] = yt.astype(BF16)
    kb_ref[...] = head_norm(proj("kb"), gk_ref[...]).astype(BF16)
    vbt_ref[0] = jnp.concatenate(blocks_t(proj("vb")), axis=1).astype(BF16)
    for r, yt in enumerate(blocks_t(proj("qi"))):
        qit_ref[r] = jnp.concatenate(
            [yt[h * IDX_DH:(h + 1) * IDX_DH, :] for h in range(IDX_HEADS)], axis=1).astype(BF16)
    ga_ref[...] = proj("ga").astype(BF16)
    gb_ref[...] = proj("gb").astype(BF16)
    small = proj("sm")
    sm_ref[...] = small
    ki_ref[...] = small[:, SM_KI:SM_KI + IDX_DH].astype(BF16)
    for r, yt in enumerate(blocks_t(small)):
        wit_ref[r] = yt[SM_WI:SM_WI + IDX_HEADS, :]


def _in_projection(x2, g_mix, w_in, g_q, g_k, seq):
    n, d = x2.shape
    widths = {name: (d if w is None else w) for name, w in _PROJ}
    ref_order = (("qa", GLA_QK_W), ("ka", GLA_QK_W), ("va", GLA_V_W), ("ra", GLA_V_W),
                 ("alr", GLA_GATE_RANK), ("qb", ATT_W), ("kb", ATT_W), ("vb", ATT_W),
                 ("qi", IDX_Q_W), ("ki", IDX_DH), ("wi", IDX_HEADS), ("ga", d), ("gb", d))
    cols, c = {}, 0
    for name, w in ref_order:
        cols[name] = w_in[:, c:c + w]
        c += w
    small = jnp.concatenate(
        [cols["ki"], cols["alr"], cols["wi"],
         jnp.zeros((d, LANES - GLA_GATE_RANK - IDX_DH - IDX_HEADS), w_in.dtype)], axis=1)
    cols["sm"] = small
    w_cat = jnp.concatenate([cols[name] for name, _ in _PROJ], axis=1).astype(BF16)
    offs, c = {}, 0
    for name, _ in _PROJ:
        offs[name] = (c, c + widths[name])
        c += widths[name]
    d_cat = c

    head = jnp.arange(ATT_W) // ATT_DH
    blockdiag = (head[:, None] == head[None, :]).astype(BF16)
    gq = jnp.tile(g_q, ATT_HEADS)[None, :].astype(F32)
    gk = jnp.tile(g_k, ATT_HEADS)[None, :].astype(F32)

    tm = ROW_TILE
    tq = Q_TILE
    nt = seq // tm
    nb = n // tq
    const = lambda i: (0, 0)
    row = lambda i: (i, 0)
    blk = lambda i: (i, 0, 0)
    shapes = {name: ((n, widths[name]), F32 if name == "sm" else BF16, (tm, widths[name]), row)
              for name, _ in _PROJ}
    shapes["qb"] = ((nb, ATT_W, tq), BF16, (tm // tq, ATT_W, tq), blk)
    shapes["vb"] = ((n // seq, ATT_W, seq), BF16, (1, ATT_W, tm), lambda i: (i // nt, 0, i % nt))
    shapes["qi"] = ((nb, IDX_DH, IDX_HEADS * tq), BF16, (tm // tq, IDX_DH, IDX_HEADS * tq), blk)
    names = [name for name, _ in _PROJ] + ["ki", "wi"]
    shapes["ki"] = ((n, IDX_DH), BF16, (tm, IDX_DH), row)
    shapes["wi"] = ((nb, IDX_HEADS, tq), F32, (tm // tq, IDX_HEADS, tq), blk)
    out_shape = [jax.ShapeDtypeStruct(shapes[k][0], shapes[k][1]) for k in names]
    out_specs = [pl.BlockSpec(shapes[k][2], shapes[k][3]) for k in names]
    outs = pl.pallas_call(
        functools.partial(_inproj_kernel, offs=offs),
        grid=(n // tm,),
        in_specs=[pl.BlockSpec((tm, d), lambda i: (i, 0)),
                  pl.BlockSpec((1, d), const),
                  pl.BlockSpec((d, d_cat), const),
                  pl.BlockSpec((1, ATT_W), const),
                  pl.BlockSpec((1, ATT_W), const),
                  pl.BlockSpec((ATT_W, ATT_W), const)],
        out_specs=out_specs,
        out_shape=out_shape,
        compiler_params=pltpu.CompilerParams(dimension_semantics=("arbitrary",),
                                             vmem_limit_bytes=VMEM_LIMIT),
        name="in_projection",
    )(x2, g_mix[None, :].astype(F32), w_cat, gq, gk, blockdiag)
    return dict(zip(names, outs))


def _gla_kernel(qa_ref, ka_ref, va_ref, ra_ref, sm_ref, ga_ref, wa2_ref, ba_ref, ltri_ref,
                lall_ref, gg_ref, wbr_ref, out_ref, st_ref, o_ref):
    @pl.when(pl.program_id(1) == 0)
    def _():
        st_ref[...] = jnp.zeros_like(st_ref)

    tb = qa_ref.shape[0]
    z = _dot(sm_ref[...].astype(BF16), wa2_ref[...]) + ba_ref[...]
    log_a = (jnp.minimum(z, 0.0) - jnp.log1p(jnp.exp(-jnp.abs(z)))) * (1.0 / GLA_TAU)
    la_hi, la_lo = _split_bf16(log_a)
    ltri = ltri_ref[...]
    lall = lall_ref[...]
    cum = _dot(ltri, la_hi) + _dot(ltri, la_lo)
    tot = _dot(lall, la_hi) + _dot(lall, la_lo)
    k_dec = (ka_ref[...].astype(F32) * jnp.exp(tot - cum)).astype(BF16)
    dec = jnp.exp(tot)

    states = [st_ref[h] for h in range(GLA_HEADS)]
    for c in range(tb // CHUNK):
        rows = slice(c * CHUNK, (c + 1) * CHUNK)
        for h in range(GLA_HEADS):
            kl = slice(h * GLA_DK, (h + 1) * GLA_DK)
            vl = slice(h * GLA_DV, (h + 1) * GLA_DV)
            u_t = _dot_tn(va_ref[rows, vl], k_dec[rows, kl])
            s = states[h] * dec[c * CHUNK:c * CHUNK + 1, kl] + u_t
            states[h] = s
            o_ref[rows, vl] = _dot_nt(qa_ref[rows, kl], s.astype(BF16))
    for h in range(GLA_HEADS):
        st_ref[h] = states[h]

    r = ra_ref[...].astype(F32)
    gated = []
    for h in range(GLA_HEADS):
        vl = slice(h * GLA_DV, (h + 1) * GLA_DV)
        oh = o_ref[:, vl]
        ms = jnp.mean(oh * oh, axis=-1, keepdims=True)
        oh = oh * lax.rsqrt(ms + EPS) * gg_ref[...]
        rh = r[:, vl]
        gated.append((oh * (rh * jax.nn.sigmoid(rh))).astype(BF16))
    og = jnp.concatenate(gated, axis=1)
    ya = _dot(og, wbr_ref[...])
    out_ref[...] = (jax.nn.sigmoid(ga_ref[...].astype(F32)) * ya).astype(BF16)


def _gla_mixer(p, w_alpha2, b_alpha, g_gla, w_br_gla, batch, seq):
    n = batch * seq
    d = w_br_gla.shape[1]
    tb = ROW_TILE
    nt = seq // tb
    wa2 = jnp.zeros((LANES, GLA_QK_W), F32).at[SM_ALR:SM_ALR + GLA_GATE_RANK].set(w_alpha2).astype(BF16)
    r = jnp.arange(tb)
    same = (r[:, None] // CHUNK) == (r[None, :] // CHUNK)
    ltri = (same & (r[None, :] <= r[:, None])).astype(BF16)
    lall = same.astype(BF16)
    row = lambda b, i: (b * nt + i, 0)
    const = lambda b, i: (0, 0)
    return pl.pallas_call(
        _gla_kernel,
        grid=(batch, nt),
        in_specs=[pl.BlockSpec((tb, GLA_QK_W), row), pl.BlockSpec((tb, GLA_QK_W), row),
                  pl.BlockSpec((tb, GLA_V_W), row), pl.BlockSpec((tb, GLA_V_W), row),
                  pl.BlockSpec((tb, LANES), row), pl.BlockSpec((tb, d), row),
                  pl.BlockSpec((LANES, GLA_QK_W), const), pl.BlockSpec((1, GLA_QK_W), const),
                  pl.BlockSpec((tb, tb), const), pl.BlockSpec((tb, tb), const),
                  pl.BlockSpec((1, GLA_DV), const), pl.BlockSpec((GLA_V_W, d), const)],
        out_specs=pl.BlockSpec((tb, d), row),
        out_shape=jax.ShapeDtypeStruct((n, d), BF16),
        scratch_shapes=[pltpu.VMEM((GLA_HEADS, GLA_DV, GLA_DK), F32),
                        pltpu.VMEM((tb, GLA_V_W), F32)],
        compiler_params=pltpu.CompilerParams(dimension_semantics=("arbitrary", "arbitrary"),
                                             vmem_limit_bytes=VMEM_LIMIT),
        name="gla_mixer",
    )(p["qa"], p["ka"], p["va"], p["ra"], p["sm"], p["ga"], wa2, b_alpha[None, :].astype(F32),
      ltri, lall, g_gla[None, :].astype(F32), w_br_gla.astype(BF16))


def _t5_bucket_int(rel):
    half = REL_BUCKETS // 2
    exact = half // 2
    n = jnp.abs(rel)
    large = jnp.full(rel.shape, exact, I32)
    for j in range(1, half - exact):
        thr = math.ceil(exact * (REL_MAX_DIST / exact) ** (j / (half - exact)) - 1e-9)
        large = large + jnp.where(n >= thr, 1, 0)
    return jnp.where(rel > 0, half, 0) + jnp.where(n < exact, n, large)


def _dsa_kernel(relb_ref, qit_ref, wit_ref, qbt_ref, gb_ref, ya_ref, kb_ref, vbt_ref, ki_ref,
                wbr_ref, out_ref, sc_ref, mb_ref, mbf_ref, bias_ref, qp_ref, s_ref, mrow_ref, alpha_ref,
                acc_ref, ot_ref, *, k_top):
    b = pl.program_id(0)
    j = pl.program_id(1)
    tq = Q_TILE
    t0 = j * tq
    n_pairs = ATT_HEADS // 2
    wwin = WIN_BACK + tq
    sub = 8

    @pl.when((b == 0) & (j == 0))
    def _():
        rk = lax.broadcasted_iota(I32, (wwin, tq), 0)
        rq = lax.broadcasted_iota(I32, (wwin, tq), 1)
        bucket = _t5_bucket_int(rk - WIN_BACK - rq)
        far = REL_BUCKETS // 2 - 1
        for h in range(ATT_HEADS):
            a = jnp.zeros((wwin, tq), F32)
            for bk in range(REL_BUCKETS):
                a = jnp.where(bucket == bk, relb_ref[bk * ATT_HEADS + h], a)
            a = (a - relb_ref[far * ATT_HEADS + h]) * LOG2_E
            cols = slice((h % 2) * tq, (h % 2 + 1) * tq)
            bias_ref[0, h // 2, :, cols] = a
            bias_ref[1, h // 2, :, cols] = jnp.concatenate(
                [a[WIN_BACK:], jnp.zeros((WIN_BACK, tq), F32)], axis=0)

    row_q = lax.broadcasted_iota(I32, (LANES, tq), 0)
    for p in range(n_pairs):
        qpair = qbt_ref[0, p * LANES:(p + 1) * LANES, :]
        zero = jnp.zeros_like(qpair)
        qp_ref[p] = jnp.concatenate([jnp.where(row_q < ATT_DH, qpair, zero),
                                     jnp.where(row_q >= ATT_DH, qpair, zero)], axis=1)
    wi = wit_ref[0] * (IDX_HEADS ** -0.5 * IDX_DH ** -0.5)

    n_sel = (t0 + tq + S_TILE - 1) // S_TILE
    row_s = lax.broadcasted_iota(I32, (S_TILE, tq), 0)
    lane_s = lax.broadcasted_iota(I32, (S_TILE, tq), 1)
    limit = t0 + ((lane_s >> 6) + 1) * CHUNK

    def score_body(kt, _):
        ks = pl.multiple_of(kt * S_TILE, S_TILE)
        s_all = _dot(ki_ref[0, pl.ds(ks, S_TILE), :], qit_ref[0])
        score = jnp.zeros((S_TILE, tq), F32)
        for h in range(IDX_HEADS):
            score = score + jnp.maximum(s_all[:, h * tq:(h + 1) * tq], 0.0) * wi[h:h + 1, :]
        sc_ref[pl.ds(ks, S_TILE), :] = jnp.where(ks + row_s < limit, score, -jnp.inf)
        return 0
    lax.fori_loop(0, n_sel, score_body, 0)

    n_cnt = (n_sel + 1) // 2

    @pl.when(n_sel % 2 == 1)
    def _():
        sc_ref[pl.ds(pl.multiple_of(n_sel * S_TILE, S_TILE), S_TILE), :] = jnp.full((S_TILE, tq), -jnp.inf, F32)

    kf = float(k_top)
    grp_rows = 8 * sub * LANES // tq
    c_groups = C_TILE // grp_rows

    def score_group(kt, g):
        return sc_ref[pl.ds(pl.multiple_of(kt * C_TILE + g * grp_rows, grp_rows), grp_rows), :]

    def key_sum(c):
        return jnp.sum(c, axis=0, keepdims=True)

    def count(cand, strict):
        def body(kt, c):
            for g in range(c_groups):
                sc = score_group(kt, g)
                c = c + jnp.where((sc > cand) if strict else (sc >= cand), 1.0, 0.0)
            return c
        return key_sum(lax.fori_loop(0, n_cnt, body, jnp.zeros((grp_rows, tq), F32)))

    def stats_body(kt, carry):
        lo, hi, n_fin, n_ge0, n_gt0 = carry
        for g in range(c_groups):
            sc = score_group(kt, g)
            fin = sc > -jnp.inf
            hi = jnp.maximum(hi, sc)
            lo = jnp.minimum(lo, jnp.where(fin, sc, jnp.inf))
            n_fin = n_fin + jnp.where(fin, 1.0, 0.0)
            n_ge0 = n_ge0 + jnp.where(sc >= 0.0, 1.0, 0.0)
            n_gt0 = n_gt0 + jnp.where(sc > 0.0, 1.0, 0.0)
        return lo, hi, n_fin, n_ge0, n_gt0
    zeros_g = jnp.zeros((grp_rows, tq), F32)
    lo_g, hi_g, fin_g, ge0_g, gt0_g = lax.fori_loop(
        0, n_cnt, stats_body,
        (jnp.full((grp_rows, tq), jnp.inf, F32), jnp.full((grp_rows, tq), -jnp.inf, F32),
         zeros_g, zeros_g, zeros_g))
    row_min = jnp.min(lo_g, axis=0, keepdims=True)
    row_max = jnp.max(hi_g, axis=0, keepdims=True)
    n_adm = key_sum(fin_g)
    c_ge0 = key_sum(ge0_g)
    c_gt0 = key_sum(gt0_g)
    has_thr = n_adm >= kf
    c_max = count(row_max, False)
    at_max = c_max >= kf
    at_zero = (c_gt0 < kf) & (c_ge0 >= kf)
    above_zero = c_gt0 >= kf
    lo0 = jnp.where(at_max, row_max, jnp.where(at_zero | above_zero, 0.0, row_min))
    cnt0 = jnp.where(at_max, c_max, jnp.where(at_zero | above_zero, c_ge0, n_adm))
    hi0 = jnp.where(at_zero | above_zero, row_max, 0.0)
    done0 = jnp.logical_not(has_thr) | at_max | at_zero | (cnt0 == kf)
    pre_done = jnp.where(done0, 1.0, 0.0)

    def bisect_step(lo, hi, cnt_lo):
        mid = 0.5 * lo + 0.5 * hi
        open_ = (mid > lo) & (mid < hi)
        c = count(mid, False)
        ge = c >= kf
        lo = jnp.where(ge, mid, lo)
        cnt_lo = jnp.where(ge, c, cnt_lo)
        hi = jnp.where(ge, hi, mid)
        return lo, hi, cnt_lo, jnp.where(open_ & (cnt_lo != kf), pre_done, 1.0)

    def bisect_cond(carry):
        return carry[3] < 0.5

    def bisect_body(carry):
        lo, hi, cnt_lo, _ = carry
        for _ in range(BISECT_STEPS):
            lo, hi, cnt_lo, conv = bisect_step(lo, hi, cnt_lo)
        return lo, hi, cnt_lo, jnp.min(conv)
    thr, _, cnt_thr, _ = lax.while_loop(
        bisect_cond, bisect_body, (lo0, hi0, cnt0, jnp.min(pre_done)))

    tie = has_thr & (cnt_thr > kf)
    any_tie = jnp.max(jnp.where(tie, 1.0, 0.0)) > 0.0
    far_end = t0 - WIN_BACK
    f32_min = float(jnp.finfo(F32).min)

    def write_mask(ks, width, sel):
        pos = ks + lax.broadcasted_iota(I32, (width, tq), 0)
        mb_ref[pl.ds(ks, width), :] = jnp.where(sel, 0.0, NEG)
        mbf_ref[pl.ds(ks, width), :] = jnp.where(sel & (pos < far_end), 0.0, NEG)

    @pl.when(jnp.logical_not(any_tie))
    def _():
        lo = jnp.where(has_thr, thr, f32_min)

        def body(kt, _):
            ks = pl.multiple_of(kt * C_TILE, C_TILE)
            write_mask(ks, C_TILE, sc_ref[pl.ds(ks, C_TILE), :] >= lo)
            return 0
        lax.fori_loop(0, n_cnt, body, 0)

    @pl.when(any_tie)
    def _():
        need = kf - count(thr, True)
        lo = jnp.where(has_thr, thr, -jnp.inf)
        ri = lax.broadcasted_iota(I32, (C_TILE, C_TILE), 0)
        ci = lax.broadcasted_iota(I32, (C_TILE, C_TILE), 1)
        tri = jnp.where(ci <= ri, 1.0, 0.0).astype(BF16)

        def body(kt, seen):
            ks = pl.multiple_of(kt * C_TILE, C_TILE)
            sc = sc_ref[pl.ds(ks, C_TILE), :]
            eq = (sc == thr) & has_thr
            eq_f = jnp.where(eq, 1.0, 0.0)
            prefix = seen + _dot(tri, eq_f.astype(BF16))
            write_mask(ks, C_TILE, (sc > lo) | (eq & (prefix <= need)))
            return seen + key_sum(eq_f)
        lax.fori_loop(0, n_cnt, body, jnp.zeros((1, tq), F32))

    ws = pl.multiple_of(jnp.maximum(far_end, 0), WIN_BACK)
    n_far = (ws + K_TILE - 1) // K_TILE
    first = (j == 0).astype(I32)
    ones_rows = 16

    def pair_lanes(p):
        return slice(p * LANES, (p + 1) * LANES)

    def key_max(s):
        part = jnp.max(s.reshape(s.shape[0] // sub, sub, s.shape[1]), axis=0)
        return jnp.max(part, axis=0, keepdims=True)

    def values_t(p, ks, width):
        return jnp.concatenate([vbt_ref[0, pair_lanes(p), pl.ds(ks, width)],
                                jnp.ones((ones_rows, width), BF16)], axis=0)

    mrow_ref[...] = jnp.full(mrow_ref.shape, NEG, F32)
    acc_ref[...] = jnp.zeros_like(acc_ref)

    def logits_stage(slot, ks, width, mask_ref, bias):
        mbt = mask_ref[pl.ds(ks, width), :]
        mb2 = jnp.concatenate([mbt, mbt], axis=1)
        for p in range(n_pairs):
            s = _dot(kb_ref[0, pl.ds(ks, width), pair_lanes(p)], qp_ref[p]) + mb2
            if bias is not None:
                s = s + bias(p)
            s_ref[slot, p, 0:width, :] = s
            m_old = mrow_ref[p]
            m_new = jnp.maximum(m_old, key_max(s))
            alpha_ref[p] = jnp.exp2(m_old - m_new)
            mrow_ref[p] = m_new

    def values_stage(slot, ks, width):
        for p in range(n_pairs):
            pexp = jnp.exp2(s_ref[slot, p, 0:width, :] - mrow_ref[p]).astype(BF16)
            acc_ref[p] = acc_ref[p] * alpha_ref[p] + _dot(values_t(p, ks, width), pexp)

    def far_start(kt):
        return pl.multiple_of(kt * K_TILE, K_TILE)

    def window_logits():
        logits_stage(n_far % 2, ws, wwin, mb_ref, lambda p: bias_ref[first, p])

    @pl.when(n_far > 0)
    def _():
        logits_stage(0, far_start(0), K_TILE, mbf_ref, None)

        def body(kt, _):
            values_stage((kt - 1) % 2, far_start(kt - 1), K_TILE)
            logits_stage(kt % 2, far_start(kt), K_TILE, mbf_ref, None)
            return 0
        lax.fori_loop(1, n_far, body, 0)
        values_stage((n_far - 1) % 2, far_start(n_far - 1), K_TILE)
        window_logits()

    @pl.when(n_far == 0)
    def _():
        window_logits()
    values_stage(n_far % 2, ws, wwin)

    row_o = lax.broadcasted_iota(I32, (LANES, tq), 0)
    for p in range(n_pairs):
        a = acc_ref[p]
        o = a[:LANES, :] / a[LANES:LANES + 1, :]
        ot_ref[pair_lanes(p), :] = jnp.where(row_o < ATT_DH, o[:, :tq], o[:, tq:]).astype(BF16)

    yb = _dot_tn(ot_ref[...], wbr_ref[...])
    mixed = jax.nn.sigmoid(gb_ref[...].astype(F32)) * yb + ya_ref[...].astype(F32)
    out_ref[...] = mixed.astype(BF16)


def _dsa_mixer(p, ya, rel_bias, w_br_att, batch, seq):
    n = batch * seq
    d = w_br_att.shape[1]
    tq = Q_TILE
    nq = seq // tq
    nb = n // tq
    k_top = min(TOPK_MAX, seq // 4)
    qit, wit, qbt, vbt = p["qi"], p["wi"], p["qb"], p["vb"]
    kb3 = p["kb"].reshape(batch, seq, ATT_W)
    ki3 = p["ki"].reshape(batch, seq, IDX_DH)
    blk = lambda b, j: (b * nq + j, 0, 0)
    row = lambda b, j: (b * nq + j, 0)
    per_b = lambda b, j: (b, 0, 0)
    const = lambda b, j: (0, 0)
    once = pl.Buffered(1)
    return pl.pallas_call(
        functools.partial(_dsa_kernel, k_top=k_top),
        grid=(batch, nq),
        in_specs=[pl.BlockSpec(memory_space=pltpu.SMEM),
                  pl.BlockSpec((1, IDX_DH, IDX_HEADS * tq), blk),
                  pl.BlockSpec((1, IDX_HEADS, tq), blk),
                  pl.BlockSpec((1, ATT_W, tq), blk),
                  pl.BlockSpec((tq, d), row), pl.BlockSpec((tq, d), row),
                  pl.BlockSpec((1, seq, ATT_W), per_b, pipeline_mode=once),
                  pl.BlockSpec((1, ATT_W, seq), per_b, pipeline_mode=once),
                  pl.BlockSpec((1, seq, IDX_DH), per_b, pipeline_mode=once),
                  pl.BlockSpec((ATT_W, d), const)],
        out_specs=pl.BlockSpec((tq, d), row),
        out_shape=jax.ShapeDtypeStruct((n, d), BF16),
        scratch_shapes=[pltpu.VMEM((seq, tq), F32),
                        pltpu.VMEM((seq, tq), F32),
                        pltpu.VMEM((seq, tq), F32),
                        pltpu.VMEM((2, ATT_HEADS // 2, WIN_BACK + tq, 2 * tq), F32),
                        pltpu.VMEM((ATT_HEADS // 2, LANES, 2 * tq), BF16),
                        pltpu.VMEM((2, ATT_HEADS // 2, K_TILE, 2 * tq), F32),
                        pltpu.VMEM((ATT_HEADS // 2, 1, 2 * tq), F32),
                        pltpu.VMEM((ATT_HEADS // 2, 1, 2 * tq), F32),
                        pltpu.VMEM((ATT_HEADS // 2, LANES + 16, 2 * tq), F32),
                        pltpu.VMEM((ATT_W, tq), BF16)],
        compiler_params=pltpu.CompilerParams(dimension_semantics=("arbitrary", "arbitrary"),
                                             vmem_limit_bytes=VMEM_LIMIT),
        name="dsa_mixer",
    )(rel_bias.astype(F32).reshape(-1), qit, wit, qbt, p["gb"], ya, kb3, vbt, ki3,
      w_br_att.astype(BF16))


RT_G0 = 0
RT_E0 = N_GROUPS


def _moe_kernel(x_ref, mx_ref, wo_ref, gf_ref, wrh_ref, wrl_ref, br_ref, wg_ref, wu_ref,
                wd_ref, out_ref, h2_ref):
    tm = x_ref.shape[0]
    x1 = x_ref[...] + _dot(mx_ref[...], wo_ref[...])
    out_ref[...] = x1
    ms = jnp.mean(x1 * x1, axis=-1, keepdims=True)
    h = x1 * lax.rsqrt(ms + EPS) * gf_ref[...]
    h_hi, h_lo = _split_bf16(h)
    h2_ref[...] = h_hi
    logits = (_dot(h_hi, wrh_ref[...]) + _dot(h_lo, wrh_ref[...]) + _dot(h_hi, wrl_ref[...])
              + br_ref[...])
    lane = lax.broadcasted_iota(I32, (tm, LANES), 1)
    big = jnp.int32(LANES)
    is_g = lane < N_GROUPS
    gl = jnp.where(is_g, logits, -jnp.inf)
    gmax = jnp.max(gl, axis=-1, keepdims=True)
    gidx = jnp.min(jnp.where(gl == gmax, lane, big), axis=-1, keepdims=True)
    gsum = jnp.sum(jnp.where(is_g, jnp.exp(gl - gmax), 0.0), axis=-1, keepdims=True)
    g_w = 1.0 / gsum
    in_grp = (lane >= RT_E0) & (lane < RT_E0 + N_EXPERTS) & (((lane - RT_E0) >> 3) == gidx)
    ev = jnp.where(in_grp, logits, -jnp.inf)
    t1 = jnp.max(ev, axis=-1, keepdims=True)
    i1 = jnp.min(jnp.where(ev == t1, lane, big), axis=-1, keepdims=True)
    ev2 = jnp.where(lane == i1, -jnp.inf, ev)
    t2 = jnp.max(ev2, axis=-1, keepdims=True)
    i2 = jnp.min(jnp.where(ev2 == t2, lane, big), axis=-1, keepdims=True)
    e2 = jnp.exp(t2 - t1)
    w1 = 1.0 / (1.0 + e2)
    w2 = e2 * w1
    comb = g_w * (jnp.where(lane == i1, w1, 0.0) + jnp.where(lane == i2, w2, 0.0))

    h2 = h2_ref[...]
    per_chunk = MOE_COLS // D_EXPERT
    for grp in range(N_GROUPS):
        for c in range(EXPERTS_PER_GROUP // per_chunk):
            cols = slice(c * MOE_COLS, (c + 1) * MOE_COLS)
            hg = _dot(h2, wg_ref[grp, :, cols])
            hu = _dot(h2, wu_ref[grp, :, cols])
            e0 = RT_E0 + grp * EXPERTS_PER_GROUP + c * per_chunk
            scale = jnp.concatenate(
                [jnp.broadcast_to(comb[:, e:e + 1], (tm, D_EXPERT)) for e in range(e0, e0 + per_chunk)],
                axis=1)
            hid = (hg * jax.nn.sigmoid(hg) * hu * scale).astype(BF16)
            out_ref[...] += _dot(hid, wd_ref[grp, cols, :])


def _out_proj_moe(x2, mixed, w_out, g_ffn, w_rg, b_rg, w_re, b_re, w_gate, w_up, w_down):
    n, d = x2.shape
    tm = ROW_TILE
    gw = EXPERTS_PER_GROUP * D_EXPERT
    w_r = jnp.zeros((d, LANES), F32).at[:, RT_G0:RT_G0 + N_GROUPS].set(w_rg)
    w_r = w_r.at[:, RT_E0:RT_E0 + N_EXPERTS].set(w_re)
    wr_hi = w_r.astype(BF16)
    wr_lo = (w_r - wr_hi.astype(F32)).astype(BF16)
    b_r = jnp.zeros((1, LANES), F32).at[0, RT_G0:RT_G0 + N_GROUPS].set(b_rg)
    b_r = b_r.at[0, RT_E0:RT_E0 + N_EXPERTS].set(b_re)

    def by_group(w):
        return (w.reshape(N_GROUPS, EXPERTS_PER_GROUP, d, D_EXPERT).transpose(0, 2, 1, 3)
                .reshape(N_GROUPS, d, gw).astype(BF16))
    wg = by_group(w_gate)
    wu = by_group(w_up)
    wd = w_down.reshape(N_GROUPS, gw, d).astype(BF16)

    row = lambda i: (i, 0)
    const = lambda i: (0, 0)
    const3 = lambda i: (0, 0, 0)
    once = pl.Buffered(1)
    return pl.pallas_call(
        _moe_kernel,
        grid=(n // tm,),
        in_specs=[pl.BlockSpec((tm, d), row), pl.BlockSpec((tm, d), row),
                  pl.BlockSpec((d, d), const, pipeline_mode=once), pl.BlockSpec((1, d), const),
                  pl.BlockSpec((d, LANES), const), pl.BlockSpec((d, LANES), const),
                  pl.BlockSpec((1, LANES), const),
                  pl.BlockSpec((N_GROUPS, d, gw), const3, pipeline_mode=once),
                  pl.BlockSpec((N_GROUPS, d, gw), const3, pipeline_mode=once),
                  pl.BlockSpec((N_GROUPS, gw, d), const3, pipeline_mode=once)],
        out_specs=pl.BlockSpec((tm, d), row),
        out_shape=jax.ShapeDtypeStruct((n, d), F32),
        scratch_shapes=[pltpu.VMEM((tm, d), BF16)],
        compiler_params=pltpu.CompilerParams(dimension_semantics=("arbitrary",),
                                             vmem_limit_bytes=VMEM_LIMIT),
        name="out_proj_moe",
    )(x2, mixed, w_out.astype(BF16), g_ffn[None, :].astype(F32), wr_hi, wr_lo, b_r,
      wg, wu, wd)


def kernel(x, g_mix, w_in, w_alpha2, b_alpha, g_gla, w_br_gla, g_q, g_k, rel_bias, w_br_att, w_out, g_ffn, w_rg, b_rg, w_re, b_re, w_gate, w_up, w_down):
    batch, seq, d = x.shape
    assert seq % ROW_TILE == 0 and seq % K_TILE == 0 and ROW_TILE % Q_TILE == 0
    x2 = x.reshape(batch * seq, d)
    for l in range(g_mix.shape[0]):
        p = _in_projection(x2, g_mix[l], w_in[l], g_q[l], g_k[l], seq)
        ya = _gla_mixer(p, w_alpha2[l], b_alpha[l], g_gla[l], w_br_gla[l], batch, seq)
        mixed = _dsa_mixer(p, ya, rel_bias, w_br_att[l], batch, seq)
        x2 = _out_proj_moe(x2, mixed, w_out[l], g_ffn[l], w_rg[l], b_rg[l], w_re[l], b_re[l],
                           w_gate[l], w_up[l], w_down[l])
    return x2.reshape(batch, seq, d)
```

```python
import functools
import math

import jax
import jax.numpy as jnp
from jax import lax
from jax.experimental import pallas as pl
from jax.experimental.pallas import tpu as pltpu

F32 = jnp.float32
BF16 = jnp.bfloat16
I32 = jnp.int32

CHUNK = 64
GLA_HEADS = 4
GLA_DK = 64
GLA_DV = 128
GLA_GATE_RANK = 16
GLA_TAU = 16.0
ATT_HEADS = 8
ATT_DH = 64
IDX_HEADS = 8
IDX_DH = 32
TOPK_MAX = 256
REL_BUCKETS = 32
REL_MAX_DIST = 128
N_GROUPS = 4
EXPERTS_PER_GROUP = 8
N_EXPERTS = N_GROUPS * EXPERTS_PER_GROUP
D_EXPERT = 128
EPS = 1e-6

GLA_QK_W = GLA_HEADS * GLA_DK
GLA_V_W = GLA_HEADS * GLA_DV
ATT_W = ATT_HEADS * ATT_DH
IDX_Q_W = IDX_HEADS * IDX_DH

LANES = 128
VMEM_LIMIT = 56 * 1024 * 1024

SM_KI = 0
SM_ALR = IDX_DH
SM_WI = IDX_DH + GLA_GATE_RANK

NEG = -1e30
LOG2_E = math.log2(math.e)

ROW_TILE = 512
Q_TILE = 256
WIN_BACK = 128
K_TILE = 512
MOE_COLS = 256
S_TILE = 256
C_TILE = 2 * S_TILE
BISECT_STEPS = 2


def _dot(a, b):
    return jnp.dot(a, b, preferred_element_type=F32)


def _dot_nt(a, b):
    return lax.dot_general(a, b, (((1,), (1,)), ((), ())), preferred_element_type=F32)


def _dot_tn(a, b):
    return lax.dot_general(a, b, (((0,), (0,)), ((), ())), preferred_element_type=F32)


def _split_bf16(a):
    hi = a.astype(BF16)
    lo = (a - hi.astype(F32)).astype(BF16)
    return hi, lo


_PROJ = (("qa", GLA_QK_W), ("ka", GLA_QK_W), ("va", GLA_V_W), ("ra", GLA_V_W),
         ("qb", ATT_W), ("kb", ATT_W), ("vb", ATT_W), ("qi", IDX_Q_W),
         ("ga", None), ("gb", None), ("sm", LANES))


def _inproj_kernel(x_ref, g_ref, w_ref, gq_ref, gk_ref, bd_ref, *out_refs, offs):
    x = x_ref[...]
    ms = jnp.mean(x * x, axis=-1, keepdims=True)
    hn = (x * lax.rsqrt(ms + EPS) * g_ref[...]).astype(BF16)

    def proj(name):
        c0, c1 = offs[name]
        return _dot(hn, w_ref[:, c0:c1])

    def head_norm(y, gain):
        ss = _dot((y * y).astype(BF16), bd_ref[...])
        return y * lax.rsqrt(ss * (1.0 / ATT_DH) + EPS) * gain

    (qa_ref, ka_ref, va_ref, ra_ref, qbt_ref, kb_ref, vbt_ref, qit_ref, ga_ref, gb_ref, sm_ref,
     ki_ref, wit_ref) = out_refs
    tq = Q_TILE
    n_blk = x.shape[0] // tq

    def blocks_t(y):
        return [y[r * tq:(r + 1) * tq, :].T for r in range(n_blk)]

    qa_ref[...] = (proj("qa") * (GLA_DK ** -0.5)).astype(BF16)
    ka_ref[...] = proj("ka").astype(BF16)
    va_ref[...] = proj("va").astype(BF16)
    ra_ref[...] = proj("ra").astype(BF16)
    qb = head_norm(proj("qb"), gq_ref[...]) * (ATT_DH ** -0.5 * LOG2_E)
    for r, yt in enumerate(blocks_t(qb)):
        qbt_ref[r] = yt.astype(BF16)
    kb_ref[...] = head_norm(proj("kb"), gk_ref[...]).astype(BF16)
    vbt_ref[0] = jnp.concatenate(blocks_t(proj("vb")), axis=1).astype(BF16)
    for r, yt in enumerate(blocks_t(proj("qi"))):
        qit_ref[r] = jnp.concatenate(
            [yt[h * IDX_DH:(h + 1) * IDX_DH, :] for h in range(IDX_HEADS)], axis=1).astype(BF16)
    ga_ref[...] = proj("ga").astype(BF16)
    gb_ref[...] = proj("gb").astype(BF16)
    small = proj("sm")
    sm_ref[...] = small
    ki_ref[...] = small[:, SM_KI:SM_KI + IDX_DH].astype(BF16)
    for r, yt in enumerate(blocks_t(small)):
        wit_ref[r] = yt[SM_WI:SM_WI + IDX_HEADS, :]


def _in_projection(x2, g_mix, w_in, g_q, g_k, seq):
    n, d = x2.shape
    widths = {name: (d if w is None else w) for name, w in _PROJ}
    ref_order = (("qa", GLA_QK_W), ("ka", GLA_QK_W), ("va", GLA_V_W), ("ra", GLA_V_W),
                 ("alr", GLA_GATE_RANK), ("qb", ATT_W), ("kb", ATT_W), ("vb", ATT_W),
                 ("qi", IDX_Q_W), ("ki", IDX_DH), ("wi", IDX_HEADS), ("ga", d), ("gb", d))
    cols, c = {}, 0
    for name, w in ref_order:
        cols[name] = w_in[:, c:c + w]
        c += w
    small = jnp.concatenate(
        [cols["ki"], cols["alr"], cols["wi"],
         jnp.zeros((d, LANES - GLA_GATE_RANK - IDX_DH - IDX_HEADS), w_in.dtype)], axis=1)
    cols["sm"] = small
    w_cat = jnp.concatenate([cols[name] for name, _ in _PROJ], axis=1).astype(BF16)
    offs, c = {}, 0
    for name, _ in _PROJ:
        offs[name] = (c, c + widths[name])
        c += widths[name]
    d_cat = c

    head = jnp.arange(ATT_W) // ATT_DH
    blockdiag = (head[:, None] == head[None, :]).astype(BF16)
    gq = jnp.tile(g_q, ATT_HEADS)[None, :].astype(F32)
    gk = jnp.tile(g_k, ATT_HEADS)[None, :].astype(F32)

    tm = ROW_TILE
    tq = Q_TILE
    nt = seq // tm
    nb = n // tq
    const = lambda i: (0, 0)
    row = lambda i: (i, 0)
    blk = lambda i: (i, 0, 0)
    shapes = {name: ((n, widths[name]), F32 if name == "sm" else BF16, (tm, widths[name]), row)
              for name, _ in _PROJ}
    shapes["qb"] = ((nb, ATT_W, tq), BF16, (tm // tq, ATT_W, tq), blk)
    shapes["vb"] = ((n // seq, ATT_W, seq), BF16, (1, ATT_W, tm), lambda i: (i // nt, 0, i % nt))
    shapes["qi"] = ((nb, IDX_DH, IDX_HEADS * tq), BF16, (tm // tq, IDX_DH, IDX_HEADS * tq), blk)
    names = [name for name, _ in _PROJ] + ["ki", "wi"]
    shapes["ki"] = ((n, IDX_DH), BF16, (tm, IDX_DH), row)
    shapes["wi"] = ((nb, IDX_HEADS, tq), F32, (tm // tq, IDX_HEADS, tq), blk)
    out_shape = [jax.ShapeDtypeStruct(shapes[k][0], shapes[k][1]) for k in names]
    out_specs = [pl.BlockSpec(shapes[k][2], shapes[k][3]) for k in names]
    outs = pl.pallas_call(
        functools.partial(_inproj_kernel, offs=offs),
        grid=(n // tm,),
        in_specs=[pl.BlockSpec((tm, d), lambda i: (i, 0)),
                  pl.BlockSpec((1, d), const),
                  pl.BlockSpec((d, d_cat), const),
                  pl.BlockSpec((1, ATT_W), const),
                  pl.BlockSpec((1, ATT_W), const),
                  pl.BlockSpec((ATT_W, ATT_W), const)],
        out_specs=out_specs,
        out_shape=out_shape,
        compiler_params=pltpu.CompilerParams(dimension_semantics=("arbitrary",),
                                             vmem_limit_bytes=VMEM_LIMIT),
        name="in_projection",
    )(x2, g_mix[None, :].astype(F32), w_cat, gq, gk, blockdiag)
    return dict(zip(names, outs))


def _gla_kernel(qa_ref, ka_ref, va_ref, ra_ref, sm_ref, ga_ref, wa2_ref, ba_ref, ltri_ref,
                lall_ref, gg_ref, wbr_ref, out_ref, st_ref, o_ref):
    @pl.when(pl.program_id(1) == 0)
    def _():
        st_ref[...] = jnp.zeros_like(st_ref)

    tb = qa_ref.shape[0]
    z = _dot(sm_ref[...].astype(BF16), wa2_ref[...]) + ba_ref[...]
    log_a = (jnp.minimum(z, 0.0) - jnp.log1p(jnp.exp(-jnp.abs(z)))) * (1.0 / GLA_TAU)
    la_hi, la_lo = _split_bf16(log_a)
    ltri = ltri_ref[...]
    lall = lall_ref[...]
    cum = _dot(ltri, la_hi) + _dot(ltri, la_lo)
    tot = _dot(lall, la_hi) + _dot(lall, la_lo)
    k_dec = (ka_ref[...].astype(F32) * jnp.exp(tot - cum)).astype(BF16)
    dec = jnp.exp(tot)

    states = [st_ref[h] for h in range(GLA_HEADS)]
    for c in range(tb // CHUNK):
        rows = slice(c * CHUNK, (c + 1) * CHUNK)
        for h in range(GLA_HEADS):
            kl = slice(h * GLA_DK, (h + 1) * GLA_DK)
            vl = slice(h * GLA_DV, (h + 1) * GLA_DV)
            u_t = _dot_tn(va_ref[rows, vl], k_dec[rows, kl])
            s = states[h] * dec[c * CHUNK:c * CHUNK + 1, kl] + u_t
            states[h] = s
            o_ref[rows, vl] = _dot_nt(qa_ref[rows, kl], s.astype(BF16))
    for h in range(GLA_HEADS):
        st_ref[h] = states[h]

    r = ra_ref[...].astype(F32)
    gated = []
    for h in range(GLA_HEADS):
        vl = slice(h * GLA_DV, (h + 1) * GLA_DV)
        oh = o_ref[:, vl]
        ms = jnp.mean(oh * oh, axis=-1, keepdims=True)
        oh = oh * lax.rsqrt(ms + EPS) * gg_ref[...]
        rh = r[:, vl]
        gated.append((oh * (rh * jax.nn.sigmoid(rh))).astype(BF16))
    og = jnp.concatenate(gated, axis=1)
    ya = _dot(og, wbr_ref[...])
    out_ref[...] = (jax.nn.sigmoid(ga_ref[...].astype(F32)) * ya).astype(BF16)


def _gla_mixer(p, w_alpha2, b_alpha, g_gla, w_br_gla, batch, seq):
    n = batch * seq
    d = w_br_gla.shape[1]
    tb = ROW_TILE
    nt = seq // tb
    wa2 = jnp.zeros((LANES, GLA_QK_W), F32).at[SM_ALR:SM_ALR + GLA_GATE_RANK].set(w_alpha2).astype(BF16)
    r = jnp.arange(tb)
    same = (r[:, None] // CHUNK) == (r[None, :] // CHUNK)
    ltri = (same & (r[None, :] <= r[:, None])).astype(BF16)
    lall = same.astype(BF16)
    row = lambda b, i: (b * nt + i, 0)
    const = lambda b, i: (0, 0)
    return pl.pallas_call(
        _gla_kernel,
        grid=(batch, nt),
        in_specs=[pl.BlockSpec((tb, GLA_QK_W), row), pl.BlockSpec((tb, GLA_QK_W), row),
                  pl.BlockSpec((tb, GLA_V_W), row), pl.BlockSpec((tb, GLA_V_W), row),
                  pl.BlockSpec((tb, LANES), row), pl.BlockSpec((tb, d), row),
                  pl.BlockSpec((LANES, GLA_QK_W), const), pl.BlockSpec((1, GLA_QK_W), const),
                  pl.BlockSpec((tb, tb), const), pl.BlockSpec((tb, tb), const),
                  pl.BlockSpec((1, GLA_DV), const), pl.BlockSpec((GLA_V_W, d), const)],
        out_specs=pl.BlockSpec((tb, d), row),
        out_shape=jax.ShapeDtypeStruct((n, d), BF16),
        scratch_shapes=[pltpu.VMEM((GLA_HEADS, GLA_DV, GLA_DK), F32),
                        pltpu.VMEM((tb, GLA_V_W), F32)],
        compiler_params=pltpu.CompilerParams(dimension_semantics=("arbitrary", "arbitrary"),
                                             vmem_limit_bytes=VMEM_LIMIT),
        name="gla_mixer",
    )(p["qa"], p["ka"], p["va"], p["ra"], p["sm"], p["ga"], wa2, b_alpha[None, :].astype(F32),
      ltri, lall, g_gla[None, :].astype(F32), w_br_gla.astype(BF16))


def _t5_bucket_int(rel):
    half = REL_BUCKETS // 2
    exact = half // 2
    n = jnp.abs(rel)
    large = jnp.full(rel.shape, exact, I32)
    for j in range(1, half - exact):
        thr = math.ceil(exact * (REL_MAX_DIST / exact) ** (j / (half - exact)) - 1e-9)
        large = large + jnp.where(n >= thr, 1, 0)
    return jnp.where(rel > 0, half, 0) + jnp.where(n < exact, n, large)


def _dsa_kernel(relb_ref, qit_ref, wit_ref, qbt_ref, gb_ref, ya_ref, kb_ref, vbt_ref, ki_ref,
                wbr_ref, out_ref, sc_ref, scb_ref, mb_ref, mbf_ref, bias_ref, qp_ref, s_ref, mrow_ref,
                alpha_ref,
                acc_ref, ot_ref, *, k_top):
    b = pl.program_id(0)
    j = pl.program_id(1)
    tq = Q_TILE
    t0 = j * tq
    n_pairs = ATT_HEADS // 2
    wwin = WIN_BACK + tq
    sub = 8

    @pl.when((b == 0) & (j == 0))
    def _():
        rk = lax.broadcasted_iota(I32, (wwin, tq), 0)
        rq = lax.broadcasted_iota(I32, (wwin, tq), 1)
        bucket = _t5_bucket_int(rk - WIN_BACK - rq)
        far = REL_BUCKETS // 2 - 1
        for h in range(ATT_HEADS):
            a = jnp.zeros((wwin, tq), F32)
            for bk in range(REL_BUCKETS):
                a = jnp.where(bucket == bk, relb_ref[bk * ATT_HEADS + h], a)
            a = (a - relb_ref[far * ATT_HEADS + h]) * LOG2_E
            cols = slice((h % 2) * tq, (h % 2 + 1) * tq)
            bias_ref[0, h // 2, :, cols] = a
            bias_ref[1, h // 2, :, cols] = jnp.concatenate(
                [a[WIN_BACK:], jnp.zeros((WIN_BACK, tq), F32)], axis=0)

    row_q = lax.broadcasted_iota(I32, (LANES, tq), 0)
    for p in range(n_pairs):
        qpair = qbt_ref[0, p * LANES:(p + 1) * LANES, :]
        zero = jnp.zeros_like(qpair)
        qp_ref[p] = jnp.concatenate([jnp.where(row_q < ATT_DH, qpair, zero),
                                     jnp.where(row_q >= ATT_DH, qpair, zero)], axis=1)
    wi = wit_ref[0] * (IDX_HEADS ** -0.5 * IDX_DH ** -0.5)

    n_sel = (t0 + tq + S_TILE - 1) // S_TILE
    row_s = lax.broadcasted_iota(I32, (S_TILE, tq), 0)
    lane_s = lax.broadcasted_iota(I32, (S_TILE, tq), 1)
    limit = t0 + ((lane_s >> 6) + 1) * CHUNK

    def floor_bf16(v):
        bits = lax.bitcast_convert_type(v, I32)
        keep = jnp.int32(-65536)
        return lax.bitcast_convert_type(
            jnp.where(bits >= 0, bits & keep, (bits + 65535) & keep), F32)

    def score_body(kt, _):
        ks = pl.multiple_of(kt * S_TILE, S_TILE)
        s_all = _dot(ki_ref[0, pl.ds(ks, S_TILE), :], qit_ref[0])
        score = jnp.zeros((S_TILE, tq), F32)
        for h in range(IDX_HEADS):
            score = score + jnp.maximum(s_all[:, h * tq:(h + 1) * tq], 0.0) * wi[h:h + 1, :]
        score = jnp.where(ks + row_s < limit, score, -jnp.inf)
        sc_ref[pl.ds(ks, S_TILE), :] = score
        scb_ref[pl.ds(ks, S_TILE), :] = floor_bf16(score).astype(BF16)
        return 0
    lax.fori_loop(0, n_sel, score_body, 0)

    n_cnt = (n_sel + 1) // 2

    @pl.when(n_sel % 2 == 1)
    def _():
        pad = pl.ds(pl.multiple_of(n_sel * S_TILE, S_TILE), S_TILE)
        sc_ref[pad, :] = jnp.full((S_TILE, tq), -jnp.inf, F32)
        scb_ref[pad, :] = jnp.full((S_TILE, tq), -jnp.inf, BF16)

    kf = float(k_top)
    grp_rows = 8 * sub * LANES // tq
    c_groups = C_TILE // grp_rows

    def score_group(kt, g):
        return sc_ref[pl.ds(pl.multiple_of(kt * C_TILE + g * grp_rows, grp_rows), grp_rows), :]

    def key_sum(c):
        return jnp.sum(c, axis=0, keepdims=True)

    def count(cand, strict):
        def body(kt, c):
            for g in range(c_groups):
                sc = score_group(kt, g)
                c = c + jnp.where((sc > cand) if strict else (sc >= cand), 1.0, 0.0)
            return c
        return key_sum(lax.fori_loop(0, n_cnt, body, jnp.zeros((grp_rows, tq), F32)))

    def stats_body(kt, carry):
        lo, hi, n_fin, n_ge0, n_gt0 = carry
        for g in range(c_groups):
            sc = score_group(kt, g)
            fin = sc > -jnp.inf
            hi = jnp.maximum(hi, sc)
            lo = jnp.minimum(lo, jnp.where(fin, sc, jnp.inf))
            n_fin = n_fin + jnp.where(fin, 1.0, 0.0)
            n_ge0 = n_ge0 + jnp.where(sc >= 0.0, 1.0, 0.0)
            n_gt0 = n_gt0 + jnp.where(sc > 0.0, 1.0, 0.0)
        return lo, hi, n_fin, n_ge0, n_gt0
    zeros_g = jnp.zeros((grp_rows, tq), F32)
    lo_g, hi_g, fin_g, ge0_g, gt0_g = lax.fori_loop(
        0, n_cnt, stats_body,
        (jnp.full((grp_rows, tq), jnp.inf, F32), jnp.full((grp_rows, tq), -jnp.inf, F32),
         zeros_g, zeros_g, zeros_g))
    row_min = jnp.min(lo_g, axis=0, keepdims=True)
    row_max = jnp.max(hi_g, axis=0, keepdims=True)
    n_adm = key_sum(fin_g)
    c_ge0 = key_sum(ge0_g)
    c_gt0 = key_sum(gt0_g)
    has_thr = n_adm >= kf
    c_max = count(row_max, False)
    at_max = c_max >= kf
    at_zero = (c_gt0 < kf) & (c_ge0 >= kf)
    above_zero = c_gt0 >= kf
    lo0 = jnp.where(at_max, row_max, jnp.where(at_zero | above_zero, 0.0, row_min))
    cnt0 = jnp.where(at_max, c_max, jnp.where(at_zero | above_zero, c_ge0, n_adm))
    hi0 = jnp.where(at_zero | above_zero, row_max, 0.0)
    done0 = jnp.logical_not(has_thr) | at_max | at_zero | (cnt0 == kf)
    pre_done = jnp.where(done0, 1.0, 0.0)

    def count_coarse(cand):
        cb = cand.astype(BF16)
        rows = 2 * grp_rows
        one = jnp.ones((rows, tq), BF16)
        zero = jnp.zeros((rows, tq), BF16)

        def body(kt, c):
            for g in range(C_TILE // rows):
                sb = scb_ref[pl.ds(pl.multiple_of(kt * C_TILE + g * rows, rows), rows), :]
                c = c + jnp.where(sb >= cb, one, zero)
            return c
        return key_sum(lax.fori_loop(0, n_cnt, body, zero).astype(F32))

    def bisect_step(lo, hi, cnt_lo, coarse):
        mid = 0.5 * lo + 0.5 * hi
        if coarse:
            mid = floor_bf16(mid)
        open_ = (mid > lo) & (mid < hi)
        mid = jnp.where(open_, mid, lo)
        c = count_coarse(mid) if coarse else count(mid, False)
        ge = (c >= kf) & open_
        lt = (c < kf) & open_
        lo = jnp.where(ge, mid, lo)
        cnt_lo = jnp.where(ge, c, cnt_lo)
        hi = jnp.where(lt, mid, hi)
        return lo, hi, cnt_lo, jnp.where(open_ & (cnt_lo != kf), pre_done, 1.0)

    def bisect_cond(carry):
        return carry[3] < 0.5

    def bisect_loop(coarse, lo, hi, cnt_lo):
        def body(carry):
            lo, hi, cnt_lo, _ = carry
            for _ in range(BISECT_STEPS):
                lo, hi, cnt_lo, conv = bisect_step(lo, hi, cnt_lo, coarse)
            return lo, hi, cnt_lo, jnp.min(conv)
        return lax.while_loop(bisect_cond, body, (lo, hi, cnt_lo, jnp.min(pre_done)))[:3]

    assert sc_ref.shape[0] // (2 * grp_rows) <= 256
    lo1, hi1, cnt1 = bisect_loop(True, lo0, hi0, cnt0)
    thr, _, cnt_thr = bisect_loop(False, lo1, hi1, cnt1)

    tie = has_thr & (cnt_thr > kf)
    any_tie = jnp.max(jnp.where(tie, 1.0, 0.0)) > 0.0
    far_end = t0 - WIN_BACK
    f32_min = float(jnp.finfo(F32).min)

    def write_mask(ks, width, sel):
        pos = ks + lax.broadcasted_iota(I32, (width, tq), 0)
        mb_ref[pl.ds(ks, width), :] = jnp.where(sel, 0.0, NEG)
        mbf_ref[pl.ds(ks, width), :] = jnp.where(sel & (pos < far_end), 0.0, NEG)

    @pl.when(jnp.logical_not(any_tie))
    def _():
        lo = jnp.where(has_thr, thr, f32_min)

        def body(kt, _):
            ks = pl.multiple_of(kt * C_TILE, C_TILE)
            write_mask(ks, C_TILE, sc_ref[pl.ds(ks, C_TILE), :] >= lo)
            return 0
        lax.fori_loop(0, n_cnt, body, 0)

    @pl.when(any_tie)
    def _():
        need = kf - count(thr, True)
        lo = jnp.where(has_thr, thr, -jnp.inf)
        ri = lax.broadcasted_iota(I32, (C_TILE, C_TILE), 0)
        ci = lax.broadcasted_iota(I32, (C_TILE, C_TILE), 1)
        tri = jnp.where(ci <= ri, 1.0, 0.0).astype(BF16)

        def body(kt, seen):
            ks = pl.multiple_of(kt * C_TILE, C_TILE)
            sc = sc_ref[pl.ds(ks, C_TILE), :]
            eq = (sc == thr) & has_thr
            eq_f = jnp.where(eq, 1.0, 0.0)
            prefix = seen + _dot(tri, eq_f.astype(BF16))
            write_mask(ks, C_TILE, (sc > lo) | (eq & (prefix <= need)))
            return seen + key_sum(eq_f)
        lax.fori_loop(0, n_cnt, body, jnp.zeros((1, tq), F32))

    ws = pl.multiple_of(jnp.maximum(far_end, 0), WIN_BACK)
    n_far = (ws + K_TILE - 1) // K_TILE
    first = (j == 0).astype(I32)
    ones_rows = 16

    def pair_lanes(p):
        return slice(p * LANES, (p + 1) * LANES)

    def key_max(s):
        part = jnp.max(s.reshape(s.shape[0] // sub, sub, s.shape[1]), axis=0)
        return jnp.max(part, axis=0, keepdims=True)

    def values_t(p, ks, width):
        return jnp.concatenate([vbt_ref[0, pair_lanes(p), pl.ds(ks, width)],
                                jnp.ones((ones_rows, width), BF16)], axis=0)

    mrow_ref[...] = jnp.full(mrow_ref.shape, NEG, F32)
    acc_ref[...] = jnp.zeros_like(acc_ref)

    def logits_stage(slot, ks, width, mask_ref, bias):
        mbt = mask_ref[pl.ds(ks, width), :]
        mb2 = jnp.concatenate([mbt, mbt], axis=1)
        for p in range(n_pairs):
            s = _dot(kb_ref[0, pl.ds(ks, width), pair_lanes(p)], qp_ref[p]) + mb2
            if bias is not None:
                s = s + bias(p)
            s_ref[slot, p, 0:width, :] = s
            m_old = mrow_ref[p]
            m_new = jnp.maximum(m_old, key_max(s))
            alpha_ref[p] = jnp.exp2(m_old - m_new)
            mrow_ref[p] = m_new

    def values_stage(slot, ks, width):
        for p in range(n_pairs):
            pexp = jnp.exp2(s_ref[slot, p, 0:width, :] - mrow_ref[p]).astype(BF16)
            acc_ref[p] = acc_ref[p] * alpha_ref[p] + _dot(values_t(p, ks, width), pexp)

    def far_start(kt):
        return pl.multiple_of(kt * K_TILE, K_TILE)

    def window_logits():
        logits_stage(n_far % 2, ws, wwin, mb_ref, lambda p: bias_ref[first, p])

    @pl.when(n_far > 0)
    def _():
        logits_stage(0, far_start(0), K_TILE, mbf_ref, None)

        def body(kt, _):
            values_stage((kt - 1) % 2, far_start(kt - 1), K_TILE)
            logits_stage(kt % 2, far_start(kt), K_TILE, mbf_ref, None)
            return 0
        lax.fori_loop(1, n_far, body, 0)
        values_stage((n_far - 1) % 2, far_start(n_far - 1), K_TILE)
        window_logits()

    @pl.when(n_far == 0)
    def _():
        window_logits()
    values_stage(n_far % 2, ws, wwin)

    row_o = lax.broadcasted_iota(I32, (LANES, tq), 0)
    for p in range(n_pairs):
        a = acc_ref[p]
        o = a[:LANES, :] / a[LANES:LANES + 1, :]
        ot_ref[pair_lanes(p), :] = jnp.where(row_o < ATT_DH, o[:, :tq], o[:, tq:]).astype(BF16)

    yb = _dot_tn(ot_ref[...], wbr_ref[...])
    mixed = jax.nn.sigmoid(gb_ref[...].astype(F32)) * yb + ya_ref[...].astype(F32)
    out_ref[...] = mixed.astype(BF16)


def _dsa_mixer(p, ya, rel_bias, w_br_att, batch, seq):
    n = batch * seq
    d = w_br_att.shape[1]
    tq = Q_TILE
    nq = seq // tq
    nb = n // tq
    k_top = min(TOPK_MAX, seq // 4)
    qit, wit, qbt, vbt = p["qi"], p["wi"], p["qb"], p["vb"]
    kb3 = p["kb"].reshape(batch, seq, ATT_W)
    ki3 = p["ki"].reshape(batch, seq, IDX_DH)
    blk = lambda b, j: (b * nq + j, 0, 0)
    row = lambda b, j: (b * nq + j, 0)
    per_b = lambda b, j: (b, 0, 0)
    const = lambda b, j: (0, 0)
    once = pl.Buffered(1)
    return pl.pallas_call(
        functools.partial(_dsa_kernel, k_top=k_top),
        grid=(batch, nq),
        in_specs=[pl.BlockSpec(memory_space=pltpu.SMEM),
                  pl.BlockSpec((1, IDX_DH, IDX_HEADS * tq), blk),
                  pl.BlockSpec((1, IDX_HEADS, tq), blk),
                  pl.BlockSpec((1, ATT_W, tq), blk),
                  pl.BlockSpec((tq, d), row), pl.BlockSpec((tq, d), row),
                  pl.BlockSpec((1, seq, ATT_W), per_b, pipeline_mode=once),
                  pl.BlockSpec((1, ATT_W, seq), per_b, pipeline_mode=once),
                  pl.BlockSpec((1, seq, IDX_DH), per_b, pipeline_mode=once),
                  pl.BlockSpec((ATT_W, d), const)],
        out_specs=pl.BlockSpec((tq, d), row),
        out_shape=jax.ShapeDtypeStruct((n, d), BF16),
        scratch_shapes=[pltpu.VMEM((seq, tq), F32),
                        pltpu.VMEM((seq, tq), BF16),
                        pltpu.VMEM((seq, tq), F32),
                        pltpu.VMEM((seq, tq), F32),
                        pltpu.VMEM((2, ATT_HEADS // 2, WIN_BACK + tq, 2 * tq), F32),
                        pltpu.VMEM((ATT_HEADS // 2, LANES, 2 * tq), BF16),
                        pltpu.VMEM((2, ATT_HEADS // 2, K_TILE, 2 * tq), F32),
                        pltpu.VMEM((ATT_HEADS // 2, 1, 2 * tq), F32),
                        pltpu.VMEM((ATT_HEADS // 2, 1, 2 * tq), F32),
                        pltpu.VMEM((ATT_HEADS // 2, LANES + 16, 2 * tq), F32),
                        pltpu.VMEM((ATT_W, tq), BF16)],
        compiler_params=pltpu.CompilerParams(dimension_semantics=("arbitrary", "arbitrary"),
                                             vmem_limit_bytes=VMEM_LIMIT),
        name="dsa_mixer",
    )(rel_bias.astype(F32).reshape(-1), qit, wit, qbt, p["gb"], ya, kb3, vbt, ki3,
      w_br_att.astype(BF16))


RT_G0 = 0
RT_E0 = N_GROUPS


def _moe_kernel(x_ref, mx_ref, wo_ref, gf_ref, wrh_ref, wrl_ref, br_ref, wg_ref, wu_ref,
                wd_ref, out_ref, h2_ref):
    tm = x_ref.shape[0]
    x1 = x_ref[...] + _dot(mx_ref[...], wo_ref[...])
    out_ref[...] = x1
    ms = jnp.mean(x1 * x1, axis=-1, keepdims=True)
    h = x1 * lax.rsqrt(ms + EPS) * gf_ref[...]
    h_hi, h_lo = _split_bf16(h)
    h2_ref[...] = h_hi
    logits = (_dot(h_hi, wrh_ref[...]) + _dot(h_lo, wrh_ref[...]) + _dot(h_hi, wrl_ref[...])
              + br_ref[...])
    lane = lax.broadcasted_iota(I32, (tm, LANES), 1)
    big = jnp.int32(LANES)
    is_g = lane < N_GROUPS
    gl = jnp.where(is_g, logits, -jnp.inf)
    gmax = jnp.max(gl, axis=-1, keepdims=True)
    gidx = jnp.min(jnp.where(gl == gmax, lane, big), axis=-1, keepdims=True)
    gsum = jnp.sum(jnp.where(is_g, jnp.exp(gl - gmax), 0.0), axis=-1, keepdims=True)
    g_w = 1.0 / gsum
    in_grp = (lane >= RT_E0) & (lane < RT_E0 + N_EXPERTS) & (((lane - RT_E0) >> 3) == gidx)
    ev = jnp.where(in_grp, logits, -jnp.inf)
    t1 = jnp.max(ev, axis=-1, keepdims=True)
    i1 = jnp.min(jnp.where(ev == t1, lane, big), axis=-1, keepdims=True)
    ev2 = jnp.where(lane == i1, -jnp.inf, ev)
    t2 = jnp.max(ev2, axis=-1, keepdims=True)
    i2 = jnp.min(jnp.where(ev2 == t2, lane, big), axis=-1, keepdims=True)
    e2 = jnp.exp(t2 - t1)
    w1 = 1.0 / (1.0 + e2)
    w2 = e2 * w1
    comb = g_w * (jnp.where(lane == i1, w1, 0.0) + jnp.where(lane == i2, w2, 0.0))

    h2 = h2_ref[...]
    per_chunk = MOE_COLS // D_EXPERT
    for grp in range(N_GROUPS):
        for c in range(EXPERTS_PER_GROUP // per_chunk):
            cols = slice(c * MOE_COLS, (c + 1) * MOE_COLS)
            hg = _dot(h2, wg_ref[grp, :, cols])
            hu = _dot(h2, wu_ref[grp, :, cols])
            e0 = RT_E0 + grp * EXPERTS_PER_GROUP + c * per_chunk
            scale = jnp.concatenate(
                [jnp.broadcast_to(comb[:, e:e + 1], (tm, D_EXPERT)) for e in range(e0, e0 + per_chunk)],
                axis=1)
            hid = (hg * jax.nn.sigmoid(hg) * hu * scale).astype(BF16)
            out_ref[...] += _dot(hid, wd_ref[grp, cols, :])


def _out_proj_moe(x2, mixed, w_out, g_ffn, w_rg, b_rg, w_re, b_re, w_gate, w_up, w_down):
    n, d = x2.shape
    tm = ROW_TILE
    gw = EXPERTS_PER_GROUP * D_EXPERT
    w_r = jnp.zeros((d, LANES), F32).at[:, RT_G0:RT_G0 + N_GROUPS].set(w_rg)
    w_r = w_r.at[:, RT_E0:RT_E0 + N_EXPERTS].set(w_re)
    wr_hi = w_r.astype(BF16)
    wr_lo = (w_r - wr_hi.astype(F32)).astype(BF16)
    b_r = jnp.zeros((1, LANES), F32).at[0, RT_G0:RT_G0 + N_GROUPS].set(b_rg)
    b_r = b_r.at[0, RT_E0:RT_E0 + N_EXPERTS].set(b_re)

    def by_group(w):
        return (w.reshape(N_GROUPS, EXPERTS_PER_GROUP, d, D_EXPERT).transpose(0, 2, 1, 3)
                .reshape(N_GROUPS, d, gw).astype(BF16))
    wg = by_group(w_gate)
    wu = by_group(w_up)
    wd = w_down.reshape(N_GROUPS, gw, d).astype(BF16)

    row = lambda i: (i, 0)
    const = lambda i: (0, 0)
    const3 = lambda i: (0, 0, 0)
    once = pl.Buffered(1)
    return pl.pallas_call(
        _moe_kernel,
        grid=(n // tm,),
        in_specs=[pl.BlockSpec((tm, d), row), pl.BlockSpec((tm, d), row),
                  pl.BlockSpec((d, d), const, pipeline_mode=once), pl.BlockSpec((1, d), const),
                  pl.BlockSpec((d, LANES), const), pl.BlockSpec((d, LANES), const),
                  pl.BlockSpec((1, LANES), const),
                  pl.BlockSpec((N_GROUPS, d, gw), const3, pipeline_mode=once),
                  pl.BlockSpec((N_GROUPS, d, gw), const3, pipeline_mode=once),
                  pl.BlockSpec((N_GROUPS, gw, d), const3, pipeline_mode=once)],
        out_specs=pl.BlockSpec((tm, d), row),
        out_shape=jax.ShapeDtypeStruct((n, d), F32),
        scratch_shapes=[pltpu.VMEM((tm, d), BF16)],
        compiler_params=pltpu.CompilerParams(dimension_semantics=("arbitrary",),
                                             vmem_limit_bytes=VMEM_LIMIT),
        name="out_proj_moe",
    )(x2, mixed, w_out.astype(BF16), g_ffn[None, :].astype(F32), wr_hi, wr_lo, b_r,
      wg, wu, wd)


def kernel(x, g_mix, w_in, w_alpha2, b_alpha, g_gla, w_br_gla, g_q, g_k, rel_bias, w_br_att, w_out, g_ffn, w_rg, b_rg, w_re, b_re, w_gate, w_up, w_down):
    batch, seq, d = x.shape
    assert seq % ROW_TILE == 0 and seq % K_TILE == 0 and ROW_TILE % Q_TILE == 0
    x2 = x.reshape(batch * seq, d)
    for l in range(g_mix.shape[0]):
        p = _in_projection(x2, g_mix[l], w_in[l], g_q[l], g_k[l], seq)
        ya = _gla_mixer(p, w_alpha2[l], b_alpha[l], g_gla[l], w_br_gla[l], batch, seq)
        mixed = _dsa_mixer(p, ya, rel_bias, w_br_att[l], batch, seq)
        x2 = _out_proj_moe(x2, mixed, w_out[l], g_ffn[l], w_rg[l], b_rg[l], w_re[l], b_re[l],
                           w_gate[l], w_up[l], w_down[l])
    return x2.reshape(batch, seq, d)
```

```python
import functools
import math

import jax
import jax.numpy as jnp
from jax import lax
from jax.experimental import pallas as pl
from jax.experimental.pallas import tpu as pltpu

F32 = jnp.float32
BF16 = jnp.bfloat16
I32 = jnp.int32

CHUNK = 64
GLA_HEADS = 4
GLA_DK = 64
GLA_DV = 128
GLA_GATE_RANK = 16
GLA_TAU = 16.0
ATT_HEADS = 8
ATT_DH = 64
IDX_HEADS = 8
IDX_DH = 32
TOPK_MAX = 256
REL_BUCKETS = 32
REL_MAX_DIST = 128
N_GROUPS = 4
EXPERTS_PER_GROUP = 8
N_EXPERTS = N_GROUPS * EXPERTS_PER_GROUP
D_EXPERT = 128
EPS = 1e-6

GLA_QK_W = GLA_HEADS * GLA_DK
GLA_V_W = GLA_HEADS * GLA_DV
ATT_W = ATT_HEADS * ATT_DH
IDX_Q_W = IDX_HEADS * IDX_DH

LANES = 128
VMEM_LIMIT = 56 * 1024 * 1024

SM_KI = 0
SM_ALR = IDX_DH
SM_WI = IDX_DH + GLA_GATE_RANK

NEG = -1e30
LOG2_E = math.log2(math.e)

ROW_TILE = 512
Q_TILE = 256
WIN_BACK = 128
K_TILE = 512
MOE_COLS = 256
S_TILE = 256
C_TILE = 2 * S_TILE
BISECT_STEPS = 2


def _dot(a, b):
    return jnp.dot(a, b, preferred_element_type=F32)


def _dot_nt(a, b):
    return lax.dot_general(a, b, (((1,), (1,)), ((), ())), preferred_element_type=F32)


def _dot_tn(a, b):
    return lax.dot_general(a, b, (((0,), (0,)), ((), ())), preferred_element_type=F32)


def _split_bf16(a):
    hi = a.astype(BF16)
    lo = (a - hi.astype(F32)).astype(BF16)
    return hi, lo


_PROJ = (("qa", GLA_QK_W), ("ka", GLA_QK_W), ("va", GLA_V_W), ("ra", GLA_V_W),
         ("qb", ATT_W), ("kb", ATT_W), ("vb", ATT_W), ("qi", IDX_Q_W),
         ("ga", None), ("gb", None), ("sm", LANES))


def _inproj_kernel(x_ref, g_ref, w_ref, gq_ref, gk_ref, bd_ref, *out_refs, offs):
    x = x_ref[...]
    ms = jnp.mean(x * x, axis=-1, keepdims=True)
    hn = (x * lax.rsqrt(ms + EPS) * g_ref[...]).astype(BF16)

    def proj(name):
        c0, c1 = offs[name]
        return _dot(hn, w_ref[:, c0:c1])

    def head_norm(y, gain):
        ss = _dot((y * y).astype(BF16), bd_ref[...])
        return y * lax.rsqrt(ss * (1.0 / ATT_DH) + EPS) * gain

    (qa_ref, ka_ref, va_ref, ra_ref, qbt_ref, kb_ref, vbt_ref, qit_ref, ga_ref, gb_ref, sm_ref,
     ki_ref, wit_ref) = out_refs
    tq = Q_TILE
    n_blk = x.shape[0] // tq

    def blocks_t(y):
        return [y[r * tq:(r + 1) * tq, :].T for r in range(n_blk)]

    qa_ref[...] = (proj("qa") * (GLA_DK ** -0.5)).astype(BF16)
    ka_ref[...] = proj("ka").astype(BF16)
    va_ref[...] = proj("va").astype(BF16)
    ra_ref[...] = proj("ra").astype(BF16)
    qb = head_norm(proj("qb"), gq_ref[...]) * (ATT_DH ** -0.5 * LOG2_E)
    for r, yt in enumerate(blocks_t(qb)):
        qbt_ref[r] = yt.astype(BF16)
    kb_ref[...] = head_norm(proj("kb"), gk_ref[...]).astype(BF16)
    vbt_ref[0] = jnp.concatenate(blocks_t(proj("vb")), axis=1).astype(BF16)
    for r, yt in enumerate(blocks_t(proj("qi"))):
        qit_ref[r] = jnp.concatenate(
            [yt[h * IDX_DH:(h + 1) * IDX_DH, :] for h in range(IDX_HEADS)], axis=1).astype(BF16)
    ga_ref[...] = proj("ga").astype(BF16)
    gb_ref[...] = proj("gb").astype(BF16)
    small = proj("sm")
    sm_ref[...] = small
    ki_ref[...] = small[:, SM_KI:SM_KI + IDX_DH].astype(BF16)
    for r, yt in enumerate(blocks_t(small)):
        wit_ref[r] = yt[SM_WI:SM_WI + IDX_HEADS, :]


def _in_projection(x2, g_mix, w_in, g_q, g_k, seq):
    n, d = x2.shape
    widths = {name: (d if w is None else w) for name, w in _PROJ}
    ref_order = (("qa", GLA_QK_W), ("ka", GLA_QK_W), ("va", GLA_V_W), ("ra", GLA_V_W),
                 ("alr", GLA_GATE_RANK), ("qb", ATT_W), ("kb", ATT_W), ("vb", ATT_W),
                 ("qi", IDX_Q_W), ("ki", IDX_DH), ("wi", IDX_HEADS), ("ga", d), ("gb", d))
    cols, c = {}, 0
    for name, w in ref_order:
        cols[name] = w_in[:, c:c + w]
        c += w
    small = jnp.concatenate(
        [cols["ki"], cols["alr"], cols["wi"],
         jnp.zeros((d, LANES - GLA_GATE_RANK - IDX_DH - IDX_HEADS), w_in.dtype)], axis=1)
    cols["sm"] = small
    w_cat = jnp.concatenate([cols[name] for name, _ in _PROJ], axis=1).astype(BF16)
    offs, c = {}, 0
    for name, _ in _PROJ:
        offs[name] = (c, c + widths[name])
        c += widths[name]
    d_cat = c

    head = jnp.arange(ATT_W) // ATT_DH
    blockdiag = (head[:, None] == head[None, :]).astype(BF16)
    gq = jnp.tile(g_q, ATT_HEADS)[None, :].astype(F32)
    gk = jnp.tile(g_k, ATT_HEADS)[None, :].astype(F32)

    tm = ROW_TILE
    tq = Q_TILE
    nt = seq // tm
    nb = n // tq
    const = lambda i: (0, 0)
    row = lambda i: (i, 0)
    blk = lambda i: (i, 0, 0)
    shapes = {name: ((n, widths[name]), F32 if name == "sm" else BF16, (tm, widths[name]), row)
              for name, _ in _PROJ}
    shapes["qb"] = ((nb, ATT_W, tq), BF16, (tm // tq, ATT_W, tq), blk)
    shapes["vb"] = ((n // seq, ATT_W, seq), BF16, (1, ATT_W, tm), lambda i: (i // nt, 0, i % nt))
    shapes["qi"] = ((nb, IDX_DH, IDX_HEADS * tq), BF16, (tm // tq, IDX_DH, IDX_HEADS * tq), blk)
    names = [name for name, _ in _PROJ] + ["ki", "wi"]
    shapes["ki"] = ((n, IDX_DH), BF16, (tm, IDX_DH), row)
    shapes["wi"] = ((nb, IDX_HEADS, tq), F32, (tm // tq, IDX_HEADS, tq), blk)
    out_shape = [jax.ShapeDtypeStruct(shapes[k][0], shapes[k][1]) for k in names]
    out_specs = [pl.BlockSpec(shapes[k][2], shapes[k][3]) for k in names]
    outs = pl.pallas_call(
        functools.partial(_inproj_kernel, offs=offs),
        grid=(n // tm,),
        in_specs=[pl.BlockSpec((tm, d), lambda i: (i, 0)),
                  pl.BlockSpec((1, d), const),
                  pl.BlockSpec((d, d_cat), const),
                  pl.BlockSpec((1, ATT_W), const),
                  pl.BlockSpec((1, ATT_W), const),
                  pl.BlockSpec((ATT_W, ATT_W), const)],
        out_specs=out_specs,
        out_shape=out_shape,
        compiler_params=pltpu.CompilerParams(dimension_semantics=("arbitrary",),
                                             vmem_limit_bytes=VMEM_LIMIT),
        name="in_projection",
    )(x2, g_mix[None, :].astype(F32), w_cat, gq, gk, blockdiag)
    return dict(zip(names, outs))


def _gla_kernel(qa_ref, ka_ref, va_ref, ra_ref, sm_ref, ga_ref, wa2_ref, ba_ref, ltri_ref,
                lall_ref, gg_ref, wbr_ref, out_ref, st_ref, o_ref):
    @pl.when(pl.program_id(1) == 0)
    def _():
        st_ref[...] = jnp.zeros_like(st_ref)

    tb = qa_ref.shape[0]
    z = _dot(sm_ref[...].astype(BF16), wa2_ref[...]) + ba_ref[...]
    log_a = (jnp.minimum(z, 0.0) - jnp.log1p(jnp.exp(-jnp.abs(z)))) * (1.0 / GLA_TAU)
    la_hi, la_lo = _split_bf16(log_a)
    ltri = ltri_ref[...]
    lall = lall_ref[...]
    cum = _dot(ltri, la_hi) + _dot(ltri, la_lo)
    tot = _dot(lall, la_hi) + _dot(lall, la_lo)
    k_dec = (ka_ref[...].astype(F32) * jnp.exp(tot - cum)).astype(BF16)
    dec = jnp.exp(tot)

    rv = lax.broadcasted_iota(I32, (GLA_V_W, GLA_QK_W), 0)
    ck = lax.broadcasted_iota(I32, (GLA_V_W, GLA_QK_W), 1)
    same_head = (rv // GLA_DV) == (ck // GLA_DK)
    state = st_ref[...]
    for c in range(tb // CHUNK):
        rows = slice(c * CHUNK, (c + 1) * CHUNK)
        u_t = _dot_tn(va_ref[rows, :], k_dec[rows, :])
        state = state * dec[c * CHUNK:c * CHUNK + 1, :] + jnp.where(same_head, u_t, 0.0)
        o_ref[rows, :] = _dot_nt(qa_ref[rows, :], state.astype(BF16))
    st_ref[...] = state

    r = ra_ref[...].astype(F32)
    gated = []
    for h in range(GLA_HEADS):
        vl = slice(h * GLA_DV, (h + 1) * GLA_DV)
        oh = o_ref[:, vl]
        ms = jnp.mean(oh * oh, axis=-1, keepdims=True)
        oh = oh * lax.rsqrt(ms + EPS) * gg_ref[...]
        rh = r[:, vl]
        gated.append((oh * (rh * jax.nn.sigmoid(rh))).astype(BF16))
    og = jnp.concatenate(gated, axis=1)
    ya = _dot(og, wbr_ref[...])
    out_ref[...] = (jax.nn.sigmoid(ga_ref[...].astype(F32)) * ya).astype(BF16)


def _gla_mixer(p, w_alpha2, b_alpha, g_gla, w_br_gla, batch, seq):
    n = batch * seq
    d = w_br_gla.shape[1]
    tb = ROW_TILE
    nt = seq // tb
    wa2 = jnp.zeros((LANES, GLA_QK_W), F32).at[SM_ALR:SM_ALR + GLA_GATE_RANK].set(w_alpha2).astype(BF16)
    r = jnp.arange(tb)
    same = (r[:, None] // CHUNK) == (r[None, :] // CHUNK)
    ltri = (same & (r[None, :] <= r[:, None])).astype(BF16)
    lall = same.astype(BF16)
    row = lambda b, i: (b * nt + i, 0)
    const = lambda b, i: (0, 0)
    return pl.pallas_call(
        _gla_kernel,
        grid=(batch, nt),
        in_specs=[pl.BlockSpec((tb, GLA_QK_W), row), pl.BlockSpec((tb, GLA_QK_W), row),
                  pl.BlockSpec((tb, GLA_V_W), row), pl.BlockSpec((tb, GLA_V_W), row),
                  pl.BlockSpec((tb, LANES), row), pl.BlockSpec((tb, d), row),
                  pl.BlockSpec((LANES, GLA_QK_W), const), pl.BlockSpec((1, GLA_QK_W), const),
                  pl.BlockSpec((tb, tb), const), pl.BlockSpec((tb, tb), const),
                  pl.BlockSpec((1, GLA_DV), const), pl.BlockSpec((GLA_V_W, d), const)],
        out_specs=pl.BlockSpec((tb, d), row),
        out_shape=jax.ShapeDtypeStruct((n, d), BF16),
        scratch_shapes=[pltpu.VMEM((GLA_V_W, GLA_QK_W), F32),
                        pltpu.VMEM((tb, GLA_V_W), F32)],
        compiler_params=pltpu.CompilerParams(dimension_semantics=("arbitrary", "arbitrary"),
                                             vmem_limit_bytes=VMEM_LIMIT),
        name="gla_mixer",
    )(p["qa"], p["ka"], p["va"], p["ra"], p["sm"], p["ga"], wa2, b_alpha[None, :].astype(F32),
      ltri, lall, g_gla[None, :].astype(F32), w_br_gla.astype(BF16))


def _t5_bucket_int(rel):
    half = REL_BUCKETS // 2
    exact = half // 2
    n = jnp.abs(rel)
    large = jnp.full(rel.shape, exact, I32)
    for j in range(1, half - exact):
        thr = math.ceil(exact * (REL_MAX_DIST / exact) ** (j / (half - exact)) - 1e-9)
        large = large + jnp.where(n >= thr, 1, 0)
    return jnp.where(rel > 0, half, 0) + jnp.where(n < exact, n, large)


def _dsa_kernel(relb_ref, qit_ref, wit_ref, qbt_ref, gb_ref, ya_ref, kb_ref, vbt_ref, ki_ref,
                wbr_ref, out_ref, sc_ref, mb_ref, mbf_ref, bias_ref, qp_ref, s_ref, mrow_ref, alpha_ref,
                acc_ref, ot_ref, *, k_top):
    b = pl.program_id(0)
    j = pl.program_id(1)
    tq = Q_TILE
    t0 = j * tq
    n_pairs = ATT_HEADS // 2
    wwin = WIN_BACK + tq
    sub = 8

    @pl.when((b == 0) & (j == 0))
    def _():
        rk = lax.broadcasted_iota(I32, (wwin, tq), 0)
        rq = lax.broadcasted_iota(I32, (wwin, tq), 1)
        bucket = _t5_bucket_int(rk - WIN_BACK - rq)
        far = REL_BUCKETS // 2 - 1
        for h in range(ATT_HEADS):
            a = jnp.zeros((wwin, tq), F32)
            for bk in range(REL_BUCKETS):
                a = jnp.where(bucket == bk, relb_ref[bk * ATT_HEADS + h], a)
            a = (a - relb_ref[far * ATT_HEADS + h]) * LOG2_E
            cols = slice((h % 2) * tq, (h % 2 + 1) * tq)
            bias_ref[0, h // 2, :, cols] = a
            bias_ref[1, h // 2, :, cols] = jnp.concatenate(
                [a[WIN_BACK:], jnp.zeros((WIN_BACK, tq), F32)], axis=0)

    row_q = lax.broadcasted_iota(I32, (LANES, tq), 0)
    for p in range(n_pairs):
        qpair = qbt_ref[0, p * LANES:(p + 1) * LANES, :]
        zero = jnp.zeros_like(qpair)
        qp_ref[p] = jnp.concatenate([jnp.where(row_q < ATT_DH, qpair, zero),
                                     jnp.where(row_q >= ATT_DH, qpair, zero)], axis=1)
    wi = wit_ref[0] * (IDX_HEADS ** -0.5 * IDX_DH ** -0.5)

    n_sel = (t0 + tq + S_TILE - 1) // S_TILE
    row_s = lax.broadcasted_iota(I32, (S_TILE, tq), 0)
    lane_s = lax.broadcasted_iota(I32, (S_TILE, tq), 1)
    limit = t0 + ((lane_s >> 6) + 1) * CHUNK

    stat_rows = 2 * sub

    def score_body(kt, carry):
        lo, hi, n_ge0, n_gt0 = carry
        ks = pl.multiple_of(kt * S_TILE, S_TILE)
        s_all = _dot(ki_ref[0, pl.ds(ks, S_TILE), :], qit_ref[0])
        score = jnp.zeros((S_TILE, tq), F32)
        for h in range(IDX_HEADS):
            score = score + jnp.maximum(s_all[:, h * tq:(h + 1) * tq], 0.0) * wi[h:h + 1, :]
        adm = ks + row_s < limit
        masked = jnp.where(adm, score, -jnp.inf)
        sc_ref[pl.ds(ks, S_TILE), :] = masked
        for g in range(S_TILE // stat_rows):
            rows = slice(g * stat_rows, (g + 1) * stat_rows)
            mg = masked[rows]
            hi = jnp.maximum(hi, mg)
            lo = jnp.minimum(lo, jnp.where(adm[rows], score[rows], jnp.inf))
            n_ge0 = n_ge0 + jnp.where(mg >= 0.0, 1.0, 0.0)
            n_gt0 = n_gt0 + jnp.where(mg > 0.0, 1.0, 0.0)
        return lo, hi, n_ge0, n_gt0
    zeros_s = jnp.zeros((stat_rows, tq), F32)
    lo_g, hi_g, ge0_g, gt0_g = lax.fori_loop(
        0, n_sel, score_body,
        (jnp.full((stat_rows, tq), jnp.inf, F32), jnp.full((stat_rows, tq), -jnp.inf, F32),
         zeros_s, zeros_s))

    n_cnt = (n_sel + 1) // 2

    @pl.when(n_sel % 2 == 1)
    def _():
        sc_ref[pl.ds(pl.multiple_of(n_sel * S_TILE, S_TILE), S_TILE), :] = jnp.full((S_TILE, tq), -jnp.inf, F32)

    kf = float(k_top)
    grp_rows = 8 * sub * LANES // tq
    c_groups = C_TILE // grp_rows

    def score_group(kt, g):
        return sc_ref[pl.ds(pl.multiple_of(kt * C_TILE + g * grp_rows, grp_rows), grp_rows), :]

    def key_sum(c):
        return jnp.sum(c, axis=0, keepdims=True)

    def count(cand, strict):
        def body(kt, c):
            for g in range(c_groups):
                sc = score_group(kt, g)
                c = c + jnp.where((sc > cand) if strict else (sc >= cand), 1.0, 0.0)
            return c
        return key_sum(lax.fori_loop(0, n_cnt, body, jnp.zeros((grp_rows, tq), F32)))

    row_min = jnp.min(lo_g, axis=0, keepdims=True)
    row_max = jnp.max(hi_g, axis=0, keepdims=True)
    n_adm = limit[0:1, :].astype(F32)
    c_ge0 = key_sum(ge0_g)
    c_gt0 = key_sum(gt0_g)
    has_thr = n_adm >= kf
    c_max = count(row_max, False)
    at_max = c_max >= kf
    at_zero = (c_gt0 < kf) & (c_ge0 >= kf)
    above_zero = c_gt0 >= kf
    lo0 = jnp.where(at_max, row_max, jnp.where(at_zero | above_zero, 0.0, row_min))
    cnt0 = jnp.where(at_max, c_max, jnp.where(at_zero | above_zero, c_ge0, n_adm))
    hi0 = jnp.where(at_zero | above_zero, row_max, 0.0)
    done0 = jnp.logical_not(has_thr) | at_max | at_zero | (cnt0 == kf)
    pre_done = jnp.where(done0, 1.0, 0.0)

    def bisect_step(lo, hi, cnt_lo):
        mid = 0.5 * lo + 0.5 * hi
        open_ = (mid > lo) & (mid < hi)
        c = count(mid, False)
        ge = c >= kf
        lo = jnp.where(ge, mid, lo)
        cnt_lo = jnp.where(ge, c, cnt_lo)
        hi = jnp.where(ge, hi, mid)
        return lo, hi, cnt_lo, jnp.where(open_ & (cnt_lo != kf), pre_done, 1.0)

    def bisect_cond(carry):
        return carry[3] < 0.5

    def bisect_body(carry):
        lo, hi, cnt_lo, _ = carry
        for _ in range(BISECT_STEPS):
            lo, hi, cnt_lo, conv = bisect_step(lo, hi, cnt_lo)
        return lo, hi, cnt_lo, jnp.min(conv)
    thr, _, cnt_thr, _ = lax.while_loop(
        bisect_cond, bisect_body, (lo0, hi0, cnt0, jnp.min(pre_done)))

    tie = has_thr & (cnt_thr > kf)
    any_tie = jnp.max(jnp.where(tie, 1.0, 0.0)) > 0.0
    far_end = t0 - WIN_BACK
    f32_min = float(jnp.finfo(F32).min)

    def write_mask(ks, width, sel):
        pos = ks + lax.broadcasted_iota(I32, (width, tq), 0)
        mb_ref[pl.ds(ks, width), :] = jnp.where(sel, 0.0, NEG)
        mbf_ref[pl.ds(ks, width), :] = jnp.where(sel & (pos < far_end), 0.0, NEG)

    @pl.when(jnp.logical_not(any_tie))
    def _():
        lo = jnp.where(has_thr, thr, f32_min)

        def body(kt, _):
            ks = pl.multiple_of(kt * C_TILE, C_TILE)
            write_mask(ks, C_TILE, sc_ref[pl.ds(ks, C_TILE), :] >= lo)
            return 0
        lax.fori_loop(0, n_cnt, body, 0)

    @pl.when(any_tie)
    def _():
        need = kf - count(thr, True)
        lo = jnp.where(has_thr, thr, -jnp.inf)
        ri = lax.broadcasted_iota(I32, (C_TILE, C_TILE), 0)
        ci = lax.broadcasted_iota(I32, (C_TILE, C_TILE), 1)
        tri = jnp.where(ci <= ri, 1.0, 0.0).astype(BF16)

        def body(kt, seen):
            ks = pl.multiple_of(kt * C_TILE, C_TILE)
            sc = sc_ref[pl.ds(ks, C_TILE), :]
            eq = (sc == thr) & has_thr
            eq_f = jnp.where(eq, 1.0, 0.0)
            prefix = seen + _dot(tri, eq_f.astype(BF16))
            write_mask(ks, C_TILE, (sc > lo) | (eq & (prefix <= need)))
            return seen + key_sum(eq_f)
        lax.fori_loop(0, n_cnt, body, jnp.zeros((1, tq), F32))

    ws = pl.multiple_of(jnp.maximum(far_end, 0), WIN_BACK)
    n_far = (ws + K_TILE - 1) // K_TILE
    first = (j == 0).astype(I32)
    ones_rows = 16

    def pair_lanes(p):
        return slice(p * LANES, (p + 1) * LANES)

    def key_max(s):
        part = jnp.max(s.reshape(s.shape[0] // sub, sub, s.shape[1]), axis=0)
        return jnp.max(part, axis=0, keepdims=True)

    def values_t(p, ks, width):
        return jnp.concatenate([vbt_ref[0, pair_lanes(p), pl.ds(ks, width)],
                                jnp.ones((ones_rows, width), BF16)], axis=0)

    mrow_ref[...] = jnp.full(mrow_ref.shape, NEG, F32)
    acc_ref[...] = jnp.zeros_like(acc_ref)

    def logits_stage(slot, ks, width, mask_ref, bias):
        mbt = mask_ref[pl.ds(ks, width), :]
        mb2 = jnp.concatenate([mbt, mbt], axis=1)
        for p in range(n_pairs):
            s = _dot(kb_ref[0, pl.ds(ks, width), pair_lanes(p)], qp_ref[p]) + mb2
            if bias is not None:
                s = s + bias(p)
            s_ref[slot, p, 0:width, :] = s
            m_old = mrow_ref[p]
            m_new = jnp.maximum(m_old, key_max(s))
            alpha_ref[p] = jnp.exp2(m_old - m_new)
            mrow_ref[p] = m_new

    def values_stage(slot, ks, width):
        for p in range(n_pairs):
            pexp = jnp.exp2(s_ref[slot, p, 0:width, :] - mrow_ref[p]).astype(BF16)
            acc_ref[p] = acc_ref[p] * alpha_ref[p] + _dot(values_t(p, ks, width), pexp)

    def far_start(kt):
        return pl.multiple_of(kt * K_TILE, K_TILE)

    def window_logits():
        logits_stage(n_far % 2, ws, wwin, mb_ref, lambda p: bias_ref[first, p])

    @pl.when(n_far > 0)
    def _():
        logits_stage(0, far_start(0), K_TILE, mbf_ref, None)

        def body(kt, _):
            values_stage((kt - 1) % 2, far_start(kt - 1), K_TILE)
            logits_stage(kt % 2, far_start(kt), K_TILE, mbf_ref, None)
            return 0
        lax.fori_loop(1, n_far, body, 0)
        values_stage((n_far - 1) % 2, far_start(n_far - 1), K_TILE)
        window_logits()

    @pl.when(n_far == 0)
    def _():
        window_logits()
    values_stage(n_far % 2, ws, wwin)

    row_o = lax.broadcasted_iota(I32, (LANES, tq), 0)
    for p in range(n_pairs):
        a = acc_ref[p]
        o = a[:LANES, :] / a[LANES:LANES + 1, :]
        ot_ref[pair_lanes(p), :] = jnp.where(row_o < ATT_DH, o[:, :tq], o[:, tq:]).astype(BF16)

    yb = _dot_tn(ot_ref[...], wbr_ref[...])
    mixed = jax.nn.sigmoid(gb_ref[...].astype(F32)) * yb + ya_ref[...].astype(F32)
    out_ref[...] = mixed.astype(BF16)


def _dsa_mixer(p, ya, rel_bias, w_br_att, batch, seq):
    n = batch * seq
    d = w_br_att.shape[1]
    tq = Q_TILE
    nq = seq // tq
    nb = n // tq
    k_top = min(TOPK_MAX, seq // 4)
    qit, wit, qbt, vbt = p["qi"], p["wi"], p["qb"], p["vb"]
    kb3 = p["kb"].reshape(batch, seq, ATT_W)
    ki3 = p["ki"].reshape(batch, seq, IDX_DH)
    blk = lambda b, j: (b * nq + j, 0, 0)
    row = lambda b, j: (b * nq + j, 0)
    per_b = lambda b, j: (b, 0, 0)
    const = lambda b, j: (0, 0)
    once = pl.Buffered(1)
    return pl.pallas_call(
        functools.partial(_dsa_kernel, k_top=k_top),
        grid=(batch, nq),
        in_specs=[pl.BlockSpec(memory_space=pltpu.SMEM),
                  pl.BlockSpec((1, IDX_DH, IDX_HEADS * tq), blk),
                  pl.BlockSpec((1, IDX_HEADS, tq), blk),
                  pl.BlockSpec((1, ATT_W, tq), blk),
                  pl.BlockSpec((tq, d), row), pl.BlockSpec((tq, d), row),
                  pl.BlockSpec((1, seq, ATT_W), per_b, pipeline_mode=once),
                  pl.BlockSpec((1, ATT_W, seq), per_b, pipeline_mode=once),
                  pl.BlockSpec((1, seq, IDX_DH), per_b, pipeline_mode=once),
                  pl.BlockSpec((ATT_W, d), const)],
        out_specs=pl.BlockSpec((tq, d), row),
        out_shape=jax.ShapeDtypeStruct((n, d), BF16),
        scratch_shapes=[pltpu.VMEM((seq, tq), F32),
                        pltpu.VMEM((seq, tq), F32),
                        pltpu.VMEM((seq, tq), F32),
                        pltpu.VMEM((2, ATT_HEADS // 2, WIN_BACK + tq, 2 * tq), F32),
                        pltpu.VMEM((ATT_HEADS // 2, LANES, 2 * tq), BF16),
                        pltpu.VMEM((2, ATT_HEADS // 2, K_TILE, 2 * tq), F32),
                        pltpu.VMEM((ATT_HEADS // 2, 1, 2 * tq), F32),
                        pltpu.VMEM((ATT_HEADS // 2, 1, 2 * tq), F32),
                        pltpu.VMEM((ATT_HEADS // 2, LANES + 16, 2 * tq), F32),
                        pltpu.VMEM((ATT_W, tq), BF16)],
        compiler_params=pltpu.CompilerParams(dimension_semantics=("arbitrary", "arbitrary"),
                                             vmem_limit_bytes=VMEM_LIMIT),
        name="dsa_mixer",
    )(rel_bias.astype(F32).reshape(-1), qit, wit, qbt, p["gb"], ya, kb3, vbt, ki3,
      w_br_att.astype(BF16))


RT_G0 = 0
RT_E0 = 8


def _moe_kernel(x_ref, mx_ref, wo_ref, gf_ref, wrh_ref, wrl_ref, br_ref, wg_ref, wu_ref,
                wd_ref, out_ref, h2_ref):
    tm = x_ref.shape[0]
    x1 = x_ref[...] + _dot(mx_ref[...], wo_ref[...])
    out_ref[...] = x1
    ms = jnp.mean(x1 * x1, axis=-1, keepdims=True)
    h = x1 * lax.rsqrt(ms + EPS) * gf_ref[...]
    h_hi, h_lo = _split_bf16(h)
    h2_ref[...] = h_hi
    logits = (_dot(h_hi, wrh_ref[...]) + _dot(h_lo, wrh_ref[...]) + _dot(h_hi, wrl_ref[...])
              + br_ref[...])
    lt = logits.T
    gl = lt[RT_G0:RT_G0 + N_GROUPS]
    row_g = lax.broadcasted_iota(I32, gl.shape, 0)
    gmax = jnp.max(gl, axis=0, keepdims=True)
    gidx = jnp.min(jnp.where(gl == gmax, row_g, N_GROUPS), axis=0, keepdims=True)
    g_w = 1.0 / jnp.sum(jnp.exp(gl - gmax), axis=0, keepdims=True)
    el = lt[RT_E0:RT_E0 + N_EXPERTS]
    row_e = lax.broadcasted_iota(I32, el.shape, 0)
    ev = jnp.where((row_e >> 3) == gidx, el, -jnp.inf)
    t1 = jnp.max(ev, axis=0, keepdims=True)
    i1 = jnp.min(jnp.where(ev == t1, row_e, N_EXPERTS), axis=0, keepdims=True)
    ev2 = jnp.where(row_e == i1, -jnp.inf, ev)
    t2 = jnp.max(ev2, axis=0, keepdims=True)
    i2 = jnp.min(jnp.where(ev2 == t2, row_e, N_EXPERTS), axis=0, keepdims=True)
    e2 = jnp.exp(t2 - t1)
    w1 = 1.0 / (1.0 + e2)
    w2 = e2 * w1
    comb_t = g_w * (jnp.where(row_e == i1, w1, 0.0) + jnp.where(row_e == i2, w2, 0.0))
    comb = jnp.concatenate([comb_t, jnp.zeros((LANES - N_EXPERTS, tm), F32)], axis=0).T

    h2 = h2_ref[...]
    per_chunk = MOE_COLS // D_EXPERT
    for grp in range(N_GROUPS):
        for c in range(EXPERTS_PER_GROUP // per_chunk):
            cols = slice(c * MOE_COLS, (c + 1) * MOE_COLS)
            hg = _dot(h2, wg_ref[grp, :, cols])
            hu = _dot(h2, wu_ref[grp, :, cols])
            e0 = grp * EXPERTS_PER_GROUP + c * per_chunk
            scale = jnp.concatenate(
                [jnp.broadcast_to(comb[:, e:e + 1], (tm, D_EXPERT)) for e in range(e0, e0 + per_chunk)],
                axis=1)
            hid = (hg * jax.nn.sigmoid(hg) * hu * scale).astype(BF16)
            out_ref[...] += _dot(hid, wd_ref[grp, cols, :])


def _out_proj_moe(x2, mixed, w_out, g_ffn, w_rg, b_rg, w_re, b_re, w_gate, w_up, w_down):
    n, d = x2.shape
    tm = ROW_TILE
    gw = EXPERTS_PER_GROUP * D_EXPERT
    w_r = jnp.zeros((d, LANES), F32).at[:, RT_G0:RT_G0 + N_GROUPS].set(w_rg)
    w_r = w_r.at[:, RT_E0:RT_E0 + N_EXPERTS].set(w_re)
    wr_hi = w_r.astype(BF16)
    wr_lo = (w_r - wr_hi.astype(F32)).astype(BF16)
    b_r = jnp.zeros((1, LANES), F32).at[0, RT_G0:RT_G0 + N_GROUPS].set(b_rg)
    b_r = b_r.at[0, RT_E0:RT_E0 + N_EXPERTS].set(b_re)

    def by_group(w):
        return (w.reshape(N_GROUPS, EXPERTS_PER_GROUP, d, D_EXPERT).transpose(0, 2, 1, 3)
                .reshape(N_GROUPS, d, gw).astype(BF16))
    wg = by_group(w_gate)
    wu = by_group(w_up)
    wd = w_down.reshape(N_GROUPS, gw, d).astype(BF16)

    row = lambda i: (i, 0)
    const = lambda i: (0, 0)
    const3 = lambda i: (0, 0, 0)
    once = pl.Buffered(1)
    return pl.pallas_call(
        _moe_kernel,
        grid=(n // tm,),
        in_specs=[pl.BlockSpec((tm, d), row), pl.BlockSpec((tm, d), row),
                  pl.BlockSpec((d, d), const, pipeline_mode=once), pl.BlockSpec((1, d), const),
                  pl.BlockSpec((d, LANES), const), pl.BlockSpec((d, LANES), const),
                  pl.BlockSpec((1, LANES), const),
                  pl.BlockSpec((N_GROUPS, d, gw), const3, pipeline_mode=once),
                  pl.BlockSpec((N_GROUPS, d, gw), const3, pipeline_mode=once),
                  pl.BlockSpec((N_GROUPS, gw, d), const3, pipeline_mode=once)],
        out_specs=pl.BlockSpec((tm, d), row),
        out_shape=jax.ShapeDtypeStruct((n, d), F32),
        scratch_shapes=[pltpu.VMEM((tm, d), BF16)],
        compiler_params=pltpu.CompilerParams(dimension_semantics=("arbitrary",),
                                             vmem_limit_bytes=VMEM_LIMIT),
        name="out_proj_moe",
    )(x2, mixed, w_out.astype(BF16), g_ffn[None, :].astype(F32), wr_hi, wr_lo, b_r,
      wg, wu, wd)


def kernel(x, g_mix, w_in, w_alpha2, b_alpha, g_gla, w_br_gla, g_q, g_k, rel_bias, w_br_att, w_out, g_ffn, w_rg, b_rg, w_re, b_re, w_gate, w_up, w_down):
    batch, seq, d = x.shape
    assert seq % ROW_TILE == 0 and seq % K_TILE == 0 and ROW_TILE % Q_TILE == 0
    x2 = x.reshape(batch * seq, d)
    for l in range(g_mix.shape[0]):
        p = _in_projection(x2, g_mix[l], w_in[l], g_q[l], g_k[l], seq)
        ya = _gla_mixer(p, w_alpha2[l], b_alpha[l], g_gla[l], w_br_gla[l], batch, seq)
        mixed = _dsa_mixer(p, ya, rel_bias, w_br_att[l], batch, seq)
        x2 = _out_proj_moe(x2, mixed, w_out[l], g_ffn[l], w_rg[l], b_rg[l], w_re[l], b_re[l],
                           w_gate[l], w_up[l], w_down[l])
    return x2.reshape(batch, seq, d)
```

```python
import functools
import math

import jax
import jax.numpy as jnp
from jax import lax
from jax.experimental import pallas as pl
from jax.experimental.pallas import tpu as pltpu

F32 = jnp.float32
BF16 = jnp.bfloat16
I32 = jnp.int32

CHUNK = 64
GLA_HEADS = 4
GLA_DK = 64
GLA_DV = 128
GLA_GATE_RANK = 16
GLA_TAU = 16.0
ATT_HEADS = 8
ATT_DH = 64
IDX_HEADS = 8
IDX_DH = 32
TOPK_MAX = 256
REL_BUCKETS = 32
REL_MAX_DIST = 128
N_GROUPS = 4
EXPERTS_PER_GROUP = 8
N_EXPERTS = N_GROUPS * EXPERTS_PER_GROUP
D_EXPERT = 128
EPS = 1e-6

GLA_QK_W = GLA_HEADS * GLA_DK
GLA_V_W = GLA_HEADS * GLA_DV
ATT_W = ATT_HEADS * ATT_DH
IDX_Q_W = IDX_HEADS * IDX_DH

LANES = 128
VMEM_LIMIT = 56 * 1024 * 1024

SM_KI = 0
SM_ALR = IDX_DH
SM_WI = IDX_DH + GLA_GATE_RANK

NEG = -1e30
LOG2_E = math.log2(math.e)

ROW_TILE = 512
Q_TILE = 256
WIN_BACK = 128
K_TILE = 512
MOE_COLS = 256
S_TILE = 256
C_TILE = 2 * S_TILE
BISECT_STEPS = 2


def _dot(a, b):
    return jnp.dot(a, b, preferred_element_type=F32)


def _dot_nt(a, b):
    return lax.dot_general(a, b, (((1,), (1,)), ((), ())), preferred_element_type=F32)


def _dot_tn(a, b):
    return lax.dot_general(a, b, (((0,), (0,)), ((), ())), preferred_element_type=F32)


def _split_bf16(a):
    hi = a.astype(BF16)
    lo = (a - hi.astype(F32)).astype(BF16)
    return hi, lo


_PROJ = (("qa", GLA_QK_W), ("ka", GLA_QK_W), ("va", GLA_V_W), ("ra", GLA_V_W),
         ("qb", ATT_W), ("kb", ATT_W), ("vb", ATT_W), ("qi", IDX_Q_W),
         ("ga", None), ("gb", None), ("sm", LANES))


def _inproj_kernel(x_ref, g_ref, w_ref, gq_ref, gk_ref, bd_ref, *out_refs, offs):
    x = x_ref[...]
    ms = jnp.mean(x * x, axis=-1, keepdims=True)
    hn = (x * lax.rsqrt(ms + EPS) * g_ref[...]).astype(BF16)

    def proj(name):
        c0, c1 = offs[name]
        return _dot(hn, w_ref[:, c0:c1])

    def head_norm(y, gain):
        ss = _dot((y * y).astype(BF16), bd_ref[...])
        return y * lax.rsqrt(ss * (1.0 / ATT_DH) + EPS) * gain

    (qa_ref, ka_ref, va_ref, ra_ref, qbt_ref, kb_ref, vbt_ref, qit_ref, ga_ref, gb_ref, sm_ref,
     ki_ref, wit_ref) = out_refs
    tq = Q_TILE
    n_blk = x.shape[0] // tq

    def blocks_t(y):
        return [y[r * tq:(r + 1) * tq, :].T for r in range(n_blk)]

    qa_ref[...] = (proj("qa") * (GLA_DK ** -0.5)).astype(BF16)
    ka_ref[...] = proj("ka").astype(BF16)
    va_ref[...] = proj("va").astype(BF16)
    ra_ref[...] = proj("ra").astype(BF16)
    qb = head_norm(proj("qb"), gq_ref[...]) * (ATT_DH ** -0.5 * LOG2_E)
    for r, yt in enumerate(blocks_t(qb)):
        qbt_ref[r] = yt.astype(BF16)
    kb_ref[...] = head_norm(proj("kb"), gk_ref[...]).astype(BF16)
    vbt_ref[0] = jnp.concatenate(blocks_t(proj("vb")), axis=1).astype(BF16)
    for r, yt in enumerate(blocks_t(proj("qi"))):
        qit_ref[r] = jnp.concatenate(
            [yt[h * IDX_DH:(h + 1) * IDX_DH, :] for h in range(IDX_HEADS)], axis=1).astype(BF16)
    ga_ref[...] = proj("ga").astype(BF16)
    gb_ref[...] = proj("gb").astype(BF16)
    small = proj("sm")
    sm_ref[...] = small
    ki_ref[...] = small[:, SM_KI:SM_KI + IDX_DH].astype(BF16)
    for r, yt in enumerate(blocks_t(small)):
        wit_ref[r] = yt[SM_WI:SM_WI + IDX_HEADS, :]


def _in_projection(x2, g_mix, w_in, g_q, g_k, seq):
    n, d = x2.shape
    widths = {name: (d if w is None else w) for name, w in _PROJ}
    ref_order = (("qa", GLA_QK_W), ("ka", GLA_QK_W), ("va", GLA_V_W), ("ra", GLA_V_W),
                 ("alr", GLA_GATE_RANK), ("qb", ATT_W), ("kb", ATT_W), ("vb", ATT_W),
                 ("qi", IDX_Q_W), ("ki", IDX_DH), ("wi", IDX_HEADS), ("ga", d), ("gb", d))
    cols, c = {}, 0
    for name, w in ref_order:
        cols[name] = w_in[:, c:c + w]
        c += w
    small = jnp.concatenate(
        [cols["ki"], cols["alr"], cols["wi"],
         jnp.zeros((d, LANES - GLA_GATE_RANK - IDX_DH - IDX_HEADS), w_in.dtype)], axis=1)
    cols["sm"] = small
    w_cat = jnp.concatenate([cols[name] for name, _ in _PROJ], axis=1).astype(BF16)
    offs, c = {}, 0
    for name, _ in _PROJ:
        offs[name] = (c, c + widths[name])
        c += widths[name]
    d_cat = c

    head = jnp.arange(ATT_W) // ATT_DH
    blockdiag = (head[:, None] == head[None, :]).astype(BF16)
    gq = jnp.tile(g_q, ATT_HEADS)[None, :].astype(F32)
    gk = jnp.tile(g_k, ATT_HEADS)[None, :].astype(F32)

    tm = ROW_TILE
    tq = Q_TILE
    nt = seq // tm
    nb = n // tq
    const = lambda i: (0, 0)
    row = lambda i: (i, 0)
    blk = lambda i: (i, 0, 0)
    shapes = {name: ((n, widths[name]), F32 if name == "sm" else BF16, (tm, widths[name]), row)
              for name, _ in _PROJ}
    shapes["qb"] = ((nb, ATT_W, tq), BF16, (tm // tq, ATT_W, tq), blk)
    shapes["vb"] = ((n // seq, ATT_W, seq), BF16, (1, ATT_W, tm), lambda i: (i // nt, 0, i % nt))
    shapes["qi"] = ((nb, IDX_DH, IDX_HEADS * tq), BF16, (tm // tq, IDX_DH, IDX_HEADS * tq), blk)
    names = [name for name, _ in _PROJ] + ["ki", "wi"]
    shapes["ki"] = ((n, IDX_DH), BF16, (tm, IDX_DH), row)
    shapes["wi"] = ((nb, IDX_HEADS, tq), F32, (tm // tq, IDX_HEADS, tq), blk)
    out_shape = [jax.ShapeDtypeStruct(shapes[k][0], shapes[k][1]) for k in names]
    out_specs = [pl.BlockSpec(shapes[k][2], shapes[k][3]) for k in names]
    outs = pl.pallas_call(
        functools.partial(_inproj_kernel, offs=offs),
        grid=(n // tm,),
        in_specs=[pl.BlockSpec((tm, d), lambda i: (i, 0)),
                  pl.BlockSpec((1, d), const),
                  pl.BlockSpec((d, d_cat), const),
                  pl.BlockSpec((1, ATT_W), const),
                  pl.BlockSpec((1, ATT_W), const),
                  pl.BlockSpec((ATT_W, ATT_W), const)],
        out_specs=out_specs,
        out_shape=out_shape,
        compiler_params=pltpu.CompilerParams(dimension_semantics=("arbitrary",),
                                             vmem_limit_bytes=VMEM_LIMIT),
        name="in_projection",
    )(x2, g_mix[None, :].astype(F32), w_cat, gq, gk, blockdiag)
    return dict(zip(names, outs))


def _gla_kernel(qa_ref, ka_ref, va_ref, ra_ref, sm_ref, ga_ref, wa2_ref, ba_ref, ltri_ref,
                lall_ref, gg_ref, wbr_ref, out_ref, st_ref, o_ref):
    @pl.when(pl.program_id(1) == 0)
    def _():
        st_ref[...] = jnp.zeros_like(st_ref)

    tb = qa_ref.shape[0]
    z = _dot(sm_ref[...].astype(BF16), wa2_ref[...]) + ba_ref[...]
    log_a = (jnp.minimum(z, 0.0) - jnp.log1p(jnp.exp(-jnp.abs(z)))) * (1.0 / GLA_TAU)
    la_hi, la_lo = _split_bf16(log_a)
    ltri = ltri_ref[...]
    lall = lall_ref[...]
    cum = _dot(ltri, la_hi) + _dot(ltri, la_lo)
    tot = _dot(lall, la_hi) + _dot(lall, la_lo)
    k_dec = (ka_ref[...].astype(F32) * jnp.exp(tot - cum)).astype(BF16)
    dec = jnp.exp(tot)

    rv = lax.broadcasted_iota(I32, (GLA_V_W, GLA_QK_W), 0)
    ck = lax.broadcasted_iota(I32, (GLA_V_W, GLA_QK_W), 1)
    same_head = (rv // GLA_DV) == (ck // GLA_DK)
    state = st_ref[...]
    for c in range(tb // CHUNK):
        rows = slice(c * CHUNK, (c + 1) * CHUNK)
        u_t = _dot_tn(va_ref[rows, :], k_dec[rows, :])
        state = state * dec[c * CHUNK:c * CHUNK + 1, :] + jnp.where(same_head, u_t, 0.0)
        o_ref[rows, :] = _dot_nt(qa_ref[rows, :], state.astype(BF16))
    st_ref[...] = state

    r = ra_ref[...].astype(F32)
    gated = []
    for h in range(GLA_HEADS):
        vl = slice(h * GLA_DV, (h + 1) * GLA_DV)
        oh = o_ref[:, vl]
        ms = jnp.mean(oh * oh, axis=-1, keepdims=True)
        oh = oh * lax.rsqrt(ms + EPS) * gg_ref[...]
        rh = r[:, vl]
        gated.append((oh * (rh * jax.nn.sigmoid(rh))).astype(BF16))
    og = jnp.concatenate(gated, axis=1)
    ya = _dot(og, wbr_ref[...])
    out_ref[...] = (jax.nn.sigmoid(ga_ref[...].astype(F32)) * ya).astype(BF16)


def _gla_mixer(p, w_alpha2, b_alpha, g_gla, w_br_gla, batch, seq):
    n = batch * seq
    d = w_br_gla.shape[1]
    tb = ROW_TILE
    nt = seq // tb
    wa2 = jnp.zeros((LANES, GLA_QK_W), F32).at[SM_ALR:SM_ALR + GLA_GATE_RANK].set(w_alpha2).astype(BF16)
    r = jnp.arange(tb)
    same = (r[:, None] // CHUNK) == (r[None, :] // CHUNK)
    ltri = (same & (r[None, :] <= r[:, None])).astype(BF16)
    lall = same.astype(BF16)
    row = lambda b, i: (b * nt + i, 0)
    const = lambda b, i: (0, 0)
    return pl.pallas_call(
        _gla_kernel,
        grid=(batch, nt),
        in_specs=[pl.BlockSpec((tb, GLA_QK_W), row), pl.BlockSpec((tb, GLA_QK_W), row),
                  pl.BlockSpec((tb, GLA_V_W), row), pl.BlockSpec((tb, GLA_V_W), row),
                  pl.BlockSpec((tb, LANES), row), pl.BlockSpec((tb, d), row),
                  pl.BlockSpec((LANES, GLA_QK_W), const), pl.BlockSpec((1, GLA_QK_W), const),
                  pl.BlockSpec((tb, tb), const), pl.BlockSpec((tb, tb), const),
                  pl.BlockSpec((1, GLA_DV), const), pl.BlockSpec((GLA_V_W, d), const)],
        out_specs=pl.BlockSpec((tb, d), row),
        out_shape=jax.ShapeDtypeStruct((n, d), BF16),
        scratch_shapes=[pltpu.VMEM((GLA_V_W, GLA_QK_W), F32),
                        pltpu.VMEM((tb, GLA_V_W), F32)],
        compiler_params=pltpu.CompilerParams(dimension_semantics=("arbitrary", "arbitrary"),
                                             vmem_limit_bytes=VMEM_LIMIT),
        name="gla_mixer",
    )(p["qa"], p["ka"], p["va"], p["ra"], p["sm"], p["ga"], wa2, b_alpha[None, :].astype(F32),
      ltri, lall, g_gla[None, :].astype(F32), w_br_gla.astype(BF16))


def _t5_bucket_int(rel):
    half = REL_BUCKETS // 2
    exact = half // 2
    n = jnp.abs(rel)
    large = jnp.full(rel.shape, exact, I32)
    for j in range(1, half - exact):
        thr = math.ceil(exact * (REL_MAX_DIST / exact) ** (j / (half - exact)) - 1e-9)
        large = large + jnp.where(n >= thr, 1, 0)
    return jnp.where(rel > 0, half, 0) + jnp.where(n < exact, n, large)


def _dsa_kernel(relb_ref, qit_ref, wit_ref, qbt_ref, gb_ref, ya_ref, kb_ref, vbt_ref, ki_ref,
                wbr_ref, out_ref, sc_ref, mb_ref, bias_ref, qp_ref, s_ref, mrow_ref, alpha_ref,
                acc_ref, ot_ref, *, k_top):
    b = pl.program_id(0)
    j = pl.program_id(1)
    tq = Q_TILE
    t0 = j * tq
    n_pairs = ATT_HEADS // 2
    wwin = WIN_BACK + tq
    sub = 8

    @pl.when((b == 0) & (j == 0))
    def _():
        rk = lax.broadcasted_iota(I32, (wwin, tq), 0)
        rq = lax.broadcasted_iota(I32, (wwin, tq), 1)
        bucket = _t5_bucket_int(rk - WIN_BACK - rq)
        far = REL_BUCKETS // 2 - 1
        for h in range(ATT_HEADS):
            a = jnp.zeros((wwin, tq), F32)
            for bk in range(REL_BUCKETS):
                a = jnp.where(bucket == bk, relb_ref[bk * ATT_HEADS + h], a)
            a = (a - relb_ref[far * ATT_HEADS + h]) * LOG2_E
            cols = slice((h % 2) * tq, (h % 2 + 1) * tq)
            bias_ref[0, h // 2, :, cols] = a
            bias_ref[1, h // 2, :, cols] = jnp.concatenate(
                [a[WIN_BACK:], jnp.zeros((WIN_BACK, tq), F32)], axis=0)

    row_q = lax.broadcasted_iota(I32, (LANES, tq), 0)
    for p in range(n_pairs):
        qpair = qbt_ref[0, p * LANES:(p + 1) * LANES, :]
        zero = jnp.zeros_like(qpair)
        qp_ref[p] = jnp.concatenate([jnp.where(row_q < ATT_DH, qpair, zero),
                                     jnp.where(row_q >= ATT_DH, qpair, zero)], axis=1)
    wi = wit_ref[0] * (IDX_HEADS ** -0.5 * IDX_DH ** -0.5)

    n_sel = (t0 + tq + S_TILE - 1) // S_TILE
    row_s = lax.broadcasted_iota(I32, (S_TILE, tq), 0)
    lane_s = lax.broadcasted_iota(I32, (S_TILE, tq), 1)
    limit = t0 + ((lane_s >> 6) + 1) * CHUNK

    stat_rows = 2 * sub

    def score_body(kt, carry):
        lo, hi, n_ge0, n_gt0 = carry
        ks = pl.multiple_of(kt * S_TILE, S_TILE)
        s_all = _dot(ki_ref[0, pl.ds(ks, S_TILE), :], qit_ref[0])
        score = jnp.zeros((S_TILE, tq), F32)
        for h in range(IDX_HEADS):
            score = score + jnp.maximum(s_all[:, h * tq:(h + 1) * tq], 0.0) * wi[h:h + 1, :]
        adm = ks + row_s < limit
        masked = jnp.where(adm, score, -jnp.inf)
        sc_ref[pl.ds(ks, S_TILE), :] = masked
        for g in range(S_TILE // stat_rows):
            rows = slice(g * stat_rows, (g + 1) * stat_rows)
            mg = masked[rows]
            hi = jnp.maximum(hi, mg)
            lo = jnp.minimum(lo, jnp.where(adm[rows], score[rows], jnp.inf))
            n_ge0 = n_ge0 + jnp.where(mg >= 0.0, 1.0, 0.0)
            n_gt0 = n_gt0 + jnp.where(mg > 0.0, 1.0, 0.0)
        return lo, hi, n_ge0, n_gt0
    zeros_s = jnp.zeros((stat_rows, tq), F32)
    lo_g, hi_g, ge0_g, gt0_g = lax.fori_loop(
        0, n_sel, score_body,
        (jnp.full((stat_rows, tq), jnp.inf, F32), jnp.full((stat_rows, tq), -jnp.inf, F32),
         zeros_s, zeros_s))

    n_cnt = (n_sel + 1) // 2

    @pl.when(n_sel % 2 == 1)
    def _():
        sc_ref[pl.ds(pl.multiple_of(n_sel * S_TILE, S_TILE), S_TILE), :] = jnp.full((S_TILE, tq), -jnp.inf, F32)

    kf = float(k_top)
    grp_rows = 8 * sub * LANES // tq
    c_groups = C_TILE // grp_rows

    def score_group(kt, g):
        return sc_ref[pl.ds(pl.multiple_of(kt * C_TILE + g * grp_rows, grp_rows), grp_rows), :]

    def key_sum(c):
        return jnp.sum(c, axis=0, keepdims=True)

    def count(cand, strict):
        def body(kt, c):
            for g in range(c_groups):
                sc = score_group(kt, g)
                c = c + jnp.where((sc > cand) if strict else (sc >= cand), 1.0, 0.0)
            return c
        return key_sum(lax.fori_loop(0, n_cnt, body, jnp.zeros((grp_rows, tq), F32)))

    row_min = jnp.min(lo_g, axis=0, keepdims=True)
    row_max = jnp.max(hi_g, axis=0, keepdims=True)
    n_adm = limit[0:1, :].astype(F32)
    c_ge0 = key_sum(ge0_g)
    c_gt0 = key_sum(gt0_g)
    has_thr = n_adm >= kf
    c_max = count(row_max, False)
    at_max = c_max >= kf
    at_zero = (c_gt0 < kf) & (c_ge0 >= kf)
    above_zero = c_gt0 >= kf
    lo0 = jnp.where(at_max, row_max, jnp.where(at_zero | above_zero, 0.0, row_min))
    cnt0 = jnp.where(at_max, c_max, jnp.where(at_zero | above_zero, c_ge0, n_adm))
    hi0 = jnp.where(at_zero | above_zero, row_max, 0.0)
    done0 = jnp.logical_not(has_thr) | at_max | at_zero | (cnt0 == kf)
    pre_done = jnp.where(done0, 1.0, 0.0)

    def bisect_step(lo, hi, cnt_lo):
        mid = 0.5 * lo + 0.5 * hi
        open_ = (mid > lo) & (mid < hi)
        c = count(mid, False)
        ge = c >= kf
        lo = jnp.where(ge, mid, lo)
        cnt_lo = jnp.where(ge, c, cnt_lo)
        hi = jnp.where(ge, hi, mid)
        return lo, hi, cnt_lo, jnp.where(open_ & (cnt_lo != kf), pre_done, 1.0)

    def bisect_cond(carry):
        return carry[3] < 0.5

    def bisect_body(carry):
        lo, hi, cnt_lo, _ = carry
        for _ in range(BISECT_STEPS):
            lo, hi, cnt_lo, conv = bisect_step(lo, hi, cnt_lo)
        return lo, hi, cnt_lo, jnp.min(conv)
    thr, _, cnt_thr, _ = lax.while_loop(
        bisect_cond, bisect_body, (lo0, hi0, cnt0, jnp.min(pre_done)))

    tie = has_thr & (cnt_thr > kf)
    any_tie = jnp.max(jnp.where(tie, 1.0, 0.0)) > 0.0
    far_end = t0 - WIN_BACK
    f32_min = float(jnp.finfo(F32).min)

    def write_mask(ks, width, sel):
        mb_ref[pl.ds(ks, width), :] = jnp.where(sel, 0.0, NEG)

    @pl.when(jnp.logical_not(any_tie))
    def _():
        lo = jnp.where(has_thr, thr, f32_min)

        def body(kt, _):
            ks = pl.multiple_of(kt * C_TILE, C_TILE)
            write_mask(ks, C_TILE, sc_ref[pl.ds(ks, C_TILE), :] >= lo)
            return 0
        lax.fori_loop(0, n_cnt, body, 0)

    @pl.when(any_tie)
    def _():
        need = kf - count(thr, True)
        lo = jnp.where(has_thr, thr, -jnp.inf)
        ri = lax.broadcasted_iota(I32, (C_TILE, C_TILE), 0)
        ci = lax.broadcasted_iota(I32, (C_TILE, C_TILE), 1)
        tri = jnp.where(ci <= ri, 1.0, 0.0).astype(BF16)

        def body(kt, seen):
            ks = pl.multiple_of(kt * C_TILE, C_TILE)
            sc = sc_ref[pl.ds(ks, C_TILE), :]
            eq = (sc == thr) & has_thr
            eq_f = jnp.where(eq, 1.0, 0.0)
            prefix = seen + _dot(tri, eq_f.astype(BF16))
            write_mask(ks, C_TILE, (sc > lo) | (eq & (prefix <= need)))
            return seen + key_sum(eq_f)
        lax.fori_loop(0, n_cnt, body, jnp.zeros((1, tq), F32))

    ws = pl.multiple_of(jnp.maximum(far_end, 0), WIN_BACK)
    n_far = (ws + K_TILE - 1) // K_TILE
    first = (j == 0).astype(I32)
    ones_rows = 16

    def pair_lanes(p):
        return slice(p * LANES, (p + 1) * LANES)

    def key_max(s):
        part = jnp.max(s.reshape(s.shape[0] // sub, sub, s.shape[1]), axis=0)
        return jnp.max(part, axis=0, keepdims=True)

    def values_t(h, ks, width):
        return jnp.concatenate([vbt_ref[0, h * ATT_DH:(h + 1) * ATT_DH, pl.ds(ks, width)],
                                jnp.ones((ones_rows, width), BF16)], axis=0)

    mrow_ref[...] = jnp.full(mrow_ref.shape, NEG, F32)
    acc_ref[...] = jnp.zeros_like(acc_ref)

    def logits_stage(slot, ks, width, bias):
        mbt = mb_ref[pl.ds(ks, width), :]
        if bias is None:
            pos = ks + lax.broadcasted_iota(I32, (width, tq), 0)
            mbt = jnp.where(pos < far_end, mbt, NEG)
        mb2 = jnp.concatenate([mbt, mbt], axis=1)
        for p in range(n_pairs):
            s = _dot(kb_ref[0, pl.ds(ks, width), pair_lanes(p)], qp_ref[p]) + mb2
            if bias is not None:
                s = s + bias(p)
            s_ref[slot, p, 0:width, :] = s
            m_old = mrow_ref[p]
            m_new = jnp.maximum(m_old, key_max(s))
            alpha_ref[p] = jnp.exp2(m_old - m_new)
            mrow_ref[p] = m_new

    def values_stage(slot, ks, width):
        for p in range(n_pairs):
            pexp = jnp.exp2(s_ref[slot, p, 0:width, :] - mrow_ref[p]).astype(BF16)
            alpha = alpha_ref[p]
            for h in (2 * p, 2 * p + 1):
                cols = slice((h % 2) * tq, (h % 2 + 1) * tq)
                acc_ref[h] = acc_ref[h] * alpha[:, cols] + _dot(values_t(h, ks, width), pexp[:, cols])

    def far_start(kt):
        return pl.multiple_of(kt * K_TILE, K_TILE)

    def window_logits():
        logits_stage(n_far % 2, ws, wwin, lambda p: bias_ref[first, p])

    @pl.when(n_far > 0)
    def _():
        logits_stage(0, far_start(0), K_TILE, None)

        def body(kt, _):
            values_stage((kt - 1) % 2, far_start(kt - 1), K_TILE)
            logits_stage(kt % 2, far_start(kt), K_TILE, None)
            return 0
        lax.fori_loop(1, n_far, body, 0)
        values_stage((n_far - 1) % 2, far_start(n_far - 1), K_TILE)
        window_logits()

    @pl.when(n_far == 0)
    def _():
        window_logits()
    values_stage(n_far % 2, ws, wwin)

    for h in range(ATT_HEADS):
        a = acc_ref[h]
        ot_ref[h * ATT_DH:(h + 1) * ATT_DH, :] = (a[:ATT_DH, :] / a[ATT_DH:ATT_DH + 1, :]).astype(BF16)

    yb = _dot_tn(ot_ref[...], wbr_ref[...])
    mixed = jax.nn.sigmoid(gb_ref[...].astype(F32)) * yb + ya_ref[...].astype(F32)
    out_ref[...] = mixed.astype(BF16)


def _dsa_mixer(p, ya, rel_bias, w_br_att, batch, seq):
    n = batch * seq
    d = w_br_att.shape[1]
    tq = Q_TILE
    nq = seq // tq
    nb = n // tq
    k_top = min(TOPK_MAX, seq // 4)
    qit, wit, qbt, vbt = p["qi"], p["wi"], p["qb"], p["vb"]
    kb3 = p["kb"].reshape(batch, seq, ATT_W)
    ki3 = p["ki"].reshape(batch, seq, IDX_DH)
    blk = lambda b, j: (b * nq + j, 0, 0)
    row = lambda b, j: (b * nq + j, 0)
    per_b = lambda b, j: (b, 0, 0)
    const = lambda b, j: (0, 0)
    once = pl.Buffered(1)
    return pl.pallas_call(
        functools.partial(_dsa_kernel, k_top=k_top),
        grid=(batch, nq),
        in_specs=[pl.BlockSpec(memory_space=pltpu.SMEM),
                  pl.BlockSpec((1, IDX_DH, IDX_HEADS * tq), blk),
                  pl.BlockSpec((1, IDX_HEADS, tq), blk),
                  pl.BlockSpec((1, ATT_W, tq), blk),
                  pl.BlockSpec((tq, d), row), pl.BlockSpec((tq, d), row),
                  pl.BlockSpec((1, seq, ATT_W), per_b, pipeline_mode=once),
                  pl.BlockSpec((1, ATT_W, seq), per_b, pipeline_mode=once),
                  pl.BlockSpec((1, seq, IDX_DH), per_b, pipeline_mode=once),
                  pl.BlockSpec((ATT_W, d), const)],
        out_specs=pl.BlockSpec((tq, d), row),
        out_shape=jax.ShapeDtypeStruct((n, d), BF16),
        scratch_shapes=[pltpu.VMEM((seq, tq), F32),
                        pltpu.VMEM((seq, tq), F32),
                        pltpu.VMEM((2, ATT_HEADS // 2, WIN_BACK + tq, 2 * tq), F32),
                        pltpu.VMEM((ATT_HEADS // 2, LANES, 2 * tq), BF16),
                        pltpu.VMEM((2, ATT_HEADS // 2, K_TILE, 2 * tq), F32),
                        pltpu.VMEM((ATT_HEADS // 2, 1, 2 * tq), F32),
                        pltpu.VMEM((ATT_HEADS // 2, 1, 2 * tq), F32),
                        pltpu.VMEM((ATT_HEADS, ATT_DH + 16, tq), F32),
                        pltpu.VMEM((ATT_W, tq), BF16)],
        compiler_params=pltpu.CompilerParams(dimension_semantics=("arbitrary", "arbitrary"),
                                             vmem_limit_bytes=VMEM_LIMIT),
        name="dsa_mixer",
    )(rel_bias.astype(F32).reshape(-1), qit, wit, qbt, p["gb"], ya, kb3, vbt, ki3,
      w_br_att.astype(BF16))


RT_G0 = 0
RT_E0 = 8


def _moe_kernel(x_ref, mx_ref, wo_ref, gf_ref, wrh_ref, wrl_ref, br_ref, wg_ref, wu_ref,
                wd_ref, out_ref, h2_ref):
    tm = x_ref.shape[0]
    x1 = x_ref[...] + _dot(mx_ref[...], wo_ref[...])
    out_ref[...] = x1
    ms = jnp.mean(x1 * x1, axis=-1, keepdims=True)
    h = x1 * lax.rsqrt(ms + EPS) * gf_ref[...]
    h_hi, h_lo = _split_bf16(h)
    h2_ref[...] = h_hi
    logits = (_dot(h_hi, wrh_ref[...]) + _dot(h_lo, wrh_ref[...]) + _dot(h_hi, wrl_ref[...])
              + br_ref[...])
    lt = logits.T
    gl = lt[RT_G0:RT_G0 + N_GROUPS]
    row_g = lax.broadcasted_iota(I32, gl.shape, 0)
    gmax = jnp.max(gl, axis=0, keepdims=True)
    gidx = jnp.min(jnp.where(gl == gmax, row_g, N_GROUPS), axis=0, keepdims=True)
    g_w = 1.0 / jnp.sum(jnp.exp(gl - gmax), axis=0, keepdims=True)
    el = lt[RT_E0:RT_E0 + N_EXPERTS]
    row_e = lax.broadcasted_iota(I32, el.shape, 0)
    ev = jnp.where((row_e >> 3) == gidx, el, -jnp.inf)
    t1 = jnp.max(ev, axis=0, keepdims=True)
    i1 = jnp.min(jnp.where(ev == t1, row_e, N_EXPERTS), axis=0, keepdims=True)
    ev2 = jnp.where(row_e == i1, -jnp.inf, ev)
    t2 = jnp.max(ev2, axis=0, keepdims=True)
    i2 = jnp.min(jnp.where(ev2 == t2, row_e, N_EXPERTS), axis=0, keepdims=True)
    e2 = jnp.exp(t2 - t1)
    w1 = 1.0 / (1.0 + e2)
    w2 = e2 * w1
    comb_t = g_w * (jnp.where(row_e == i1, w1, 0.0) + jnp.where(row_e == i2, w2, 0.0))
    comb = jnp.concatenate([comb_t, jnp.zeros((LANES - N_EXPERTS, tm), F32)], axis=0).T

    h2 = h2_ref[...]
    per_chunk = MOE_COLS // D_EXPERT
    for grp in range(N_GROUPS):
        for c in range(EXPERTS_PER_GROUP // per_chunk):
            cols = slice(c * MOE_COLS, (c + 1) * MOE_COLS)
            hg = _dot(h2, wg_ref[grp, :, cols])
            hu = _dot(h2, wu_ref[grp, :, cols])
            e0 = grp * EXPERTS_PER_GROUP + c * per_chunk
            scale = jnp.concatenate(
                [jnp.broadcast_to(comb[:, e:e + 1], (tm, D_EXPERT)) for e in range(e0, e0 + per_chunk)],
                axis=1)
            hid = (hg * jax.nn.sigmoid(hg) * hu * scale).astype(BF16)
            out_ref[...] += _dot(hid, wd_ref[grp, cols, :])


def _out_proj_moe(x2, mixed, w_out, g_ffn, w_rg, b_rg, w_re, b_re, w_gate, w_up, w_down):
    n, d = x2.shape
    tm = ROW_TILE
    gw = EXPERTS_PER_GROUP * D_EXPERT
    w_r = jnp.zeros((d, LANES), F32).at[:, RT_G0:RT_G0 + N_GROUPS].set(w_rg)
    w_r = w_r.at[:, RT_E0:RT_E0 + N_EXPERTS].set(w_re)
    wr_hi = w_r.astype(BF16)
    wr_lo = (w_r - wr_hi.astype(F32)).astype(BF16)
    b_r = jnp.zeros((1, LANES), F32).at[0, RT_G0:RT_G0 + N_GROUPS].set(b_rg)
    b_r = b_r.at[0, RT_E0:RT_E0 + N_EXPERTS].set(b_re)

    def by_group(w):
        return (w.reshape(N_GROUPS, EXPERTS_PER_GROUP, d, D_EXPERT).transpose(0, 2, 1, 3)
                .reshape(N_GROUPS, d, gw).astype(BF16))
    wg = by_group(w_gate)
    wu = by_group(w_up)
    wd = w_down.reshape(N_GROUPS, gw, d).astype(BF16)

    row = lambda i: (i, 0)
    const = lambda i: (0, 0)
    const3 = lambda i: (0, 0, 0)
    once = pl.Buffered(1)
    return pl.pallas_call(
        _moe_kernel,
        grid=(n // tm,),
        in_specs=[pl.BlockSpec((tm, d), row), pl.BlockSpec((tm, d), row),
                  pl.BlockSpec((d, d), const, pipeline_mode=once), pl.BlockSpec((1, d), const),
                  pl.BlockSpec((d, LANES), const), pl.BlockSpec((d, LANES), const),
                  pl.BlockSpec((1, LANES), const),
                  pl.BlockSpec((N_GROUPS, d, gw), const3, pipeline_mode=once),
                  pl.BlockSpec((N_GROUPS, d, gw), const3, pipeline_mode=once),
                  pl.BlockSpec((N_GROUPS, gw, d), const3, pipeline_mode=once)],
        out_specs=pl.BlockSpec((tm, d), row),
        out_shape=jax.ShapeDtypeStruct((n, d), F32),
        scratch_shapes=[pltpu.VMEM((tm, d), BF16)],
        compiler_params=pltpu.CompilerParams(dimension_semantics=("arbitrary",),
                                             vmem_limit_bytes=VMEM_LIMIT),
        name="out_proj_moe",
    )(x2, mixed, w_out.astype(BF16), g_ffn[None, :].astype(F32), wr_hi, wr_lo, b_r,
      wg, wu, wd)


def kernel(x, g_mix, w_in, w_alpha2, b_alpha, g_gla, w_br_gla, g_q, g_k, rel_bias, w_br_att, w_out, g_ffn, w_rg, b_rg, w_re, b_re, w_gate, w_up, w_down):
    batch, seq, d = x.shape
    assert seq % ROW_TILE == 0 and seq % K_TILE == 0 and ROW_TILE % Q_TILE == 0
    x2 = x.reshape(batch * seq, d)
    for l in range(g_mix.shape[0]):
        p = _in_projection(x2, g_mix[l], w_in[l], g_q[l], g_k[l], seq)
        ya = _gla_mixer(p, w_alpha2[l], b_alpha[l], g_gla[l], w_br_gla[l], batch, seq)
        mixed = _dsa_mixer(p, ya, rel_bias, w_br_att[l], batch, seq)
        x2 = _out_proj_moe(x2, mixed, w_out[l], g_ffn[l], w_rg[l], b_rg[l], w_re[l], b_re[l],
                           w_gate[l], w_up[l], w_down[l])
    return x2.reshape(batch, seq, d)
```

```python
import functools
import math

import jax
import jax.numpy as jnp
from jax import lax
from jax.experimental import pallas as pl
from jax.experimental.pallas import tpu as pltpu

F32 = jnp.float32
BF16 = jnp.bfloat16
I32 = jnp.int32

CHUNK = 64
GLA_HEADS = 4
GLA_DK = 64
GLA_DV = 128
GLA_GATE_RANK = 16
GLA_TAU = 16.0
ATT_HEADS = 8
ATT_DH = 64
IDX_HEADS = 8
IDX_DH = 32
TOPK_MAX = 256
REL_BUCKETS = 32
REL_MAX_DIST = 128
N_GROUPS = 4
EXPERTS_PER_GROUP = 8
N_EXPERTS = N_GROUPS * EXPERTS_PER_GROUP
D_EXPERT = 128
EPS = 1e-6

GLA_QK_W = GLA_HEADS * GLA_DK
GLA_V_W = GLA_HEADS * GLA_DV
ATT_W = ATT_HEADS * ATT_DH
IDX_Q_W = IDX_HEADS * IDX_DH

LANES = 128
VMEM_LIMIT = 56 * 1024 * 1024

SM_KI = 0
SM_ALR = IDX_DH
SM_WI = IDX_DH + GLA_GATE_RANK

NEG = -1e30
LOG2_E = math.log2(math.e)

ROW_TILE = 512
Q_TILE = 256
WIN_BACK = 128
K_TILE = 512
MOE_COLS = 256
S_TILE = 256
C_TILE = 2 * S_TILE
BISECT_STEPS = 2


def _dot(a, b):
    return jnp.dot(a, b, preferred_element_type=F32)


def _dot_nt(a, b):
    return lax.dot_general(a, b, (((1,), (1,)), ((), ())), preferred_element_type=F32)


def _dot_tn(a, b):
    return lax.dot_general(a, b, (((0,), (0,)), ((), ())), preferred_element_type=F32)


def _split_bf16(a):
    hi = a.astype(BF16)
    lo = (a - hi.astype(F32)).astype(BF16)
    return hi, lo


_PROJ = (("qa", GLA_QK_W), ("ka", GLA_QK_W), ("va", GLA_V_W), ("ra", GLA_V_W),
         ("qb", ATT_W), ("kb", ATT_W), ("vb", ATT_W), ("qi", IDX_Q_W),
         ("ga", None), ("gb", None), ("sm", LANES))


def _inproj_kernel(x_ref, g_ref, w_ref, gq_ref, gk_ref, bd_ref, *out_refs, offs):
    x = x_ref[...]
    ms = jnp.mean(x * x, axis=-1, keepdims=True)
    hn = (x * lax.rsqrt(ms + EPS) * g_ref[...]).astype(BF16)

    def proj(name):
        c0, c1 = offs[name]
        return _dot(hn, w_ref[:, c0:c1])

    def head_norm(y, gain):
        ss = _dot((y * y).astype(BF16), bd_ref[...])
        return y * lax.rsqrt(ss * (1.0 / ATT_DH) + EPS) * gain

    (qa_ref, ka_ref, va_ref, ra_ref, qbt_ref, kb_ref, vbt_ref, qit_ref, ga_ref, gb_ref, sm_ref,
     ki_ref, wit_ref) = out_refs
    tq = Q_TILE
    n_blk = x.shape[0] // tq

    def blocks_t(y):
        return [y[r * tq:(r + 1) * tq, :].T for r in range(n_blk)]

    qa_ref[...] = (proj("qa") * (GLA_DK ** -0.5)).astype(BF16)
    ka_ref[...] = proj("ka").astype(BF16)
    va_ref[...] = proj("va").astype(BF16)
    ra_ref[...] = proj("ra").astype(BF16)
    qb = head_norm(proj("qb"), gq_ref[...]) * (ATT_DH ** -0.5 * LOG2_E)
    for r, yt in enumerate(blocks_t(qb)):
        qbt_ref[r] = yt.astype(BF16)
    kb_ref[...] = head_norm(proj("kb"), gk_ref[...]).astype(BF16)
    vbt_ref[0] = jnp.concatenate(blocks_t(proj("vb")), axis=1).astype(BF16)
    for r, yt in enumerate(blocks_t(proj("qi"))):
        qit_ref[r] = jnp.concatenate(
            [yt[h * IDX_DH:(h + 1) * IDX_DH, :] for h in range(IDX_HEADS)], axis=1).astype(BF16)
    ga_ref[...] = proj("ga").astype(BF16)
    gb_ref[...] = proj("gb").astype(BF16)
    small = proj("sm")
    sm_ref[...] = small
    ki_ref[...] = small[:, SM_KI:SM_KI + IDX_DH].astype(BF16)
    for r, yt in enumerate(blocks_t(small)):
        wit_ref[r] = yt[SM_WI:SM_WI + IDX_HEADS, :]


def _in_projection(x2, g_mix, w_in, g_q, g_k, seq):
    n, d = x2.shape
    widths = {name: (d if w is None else w) for name, w in _PROJ}
    ref_order = (("qa", GLA_QK_W), ("ka", GLA_QK_W), ("va", GLA_V_W), ("ra", GLA_V_W),
                 ("alr", GLA_GATE_RANK), ("qb", ATT_W), ("kb", ATT_W), ("vb", ATT_W),
                 ("qi", IDX_Q_W), ("ki", IDX_DH), ("wi", IDX_HEADS), ("ga", d), ("gb", d))
    cols, c = {}, 0
    for name, w in ref_order:
        cols[name] = w_in[:, c:c + w]
        c += w
    small = jnp.concatenate(
        [cols["ki"], cols["alr"], cols["wi"],
         jnp.zeros((d, LANES - GLA_GATE_RANK - IDX_DH - IDX_HEADS), w_in.dtype)], axis=1)
    cols["sm"] = small
    w_cat = jnp.concatenate([cols[name] for name, _ in _PROJ], axis=1).astype(BF16)
    offs, c = {}, 0
    for name, _ in _PROJ:
        offs[name] = (c, c + widths[name])
        c += widths[name]
    d_cat = c

    head = jnp.arange(ATT_W) // ATT_DH
    blockdiag = (head[:, None] == head[None, :]).astype(BF16)
    gq = jnp.tile(g_q, ATT_HEADS)[None, :].astype(F32)
    gk = jnp.tile(g_k, ATT_HEADS)[None, :].astype(F32)

    tm = ROW_TILE
    tq = Q_TILE
    nt = seq // tm
    nb = n // tq
    const = lambda i: (0, 0)
    row = lambda i: (i, 0)
    blk = lambda i: (i, 0, 0)
    shapes = {name: ((n, widths[name]), F32 if name == "sm" else BF16, (tm, widths[name]), row)
              for name, _ in _PROJ}
    shapes["qb"] = ((nb, ATT_W, tq), BF16, (tm // tq, ATT_W, tq), blk)
    shapes["vb"] = ((n // seq, ATT_W, seq), BF16, (1, ATT_W, tm), lambda i: (i // nt, 0, i % nt))
    shapes["qi"] = ((nb, IDX_DH, IDX_HEADS * tq), BF16, (tm // tq, IDX_DH, IDX_HEADS * tq), blk)
    names = [name for name, _ in _PROJ] + ["ki", "wi"]
    shapes["ki"] = ((n, IDX_DH), BF16, (tm, IDX_DH), row)
    shapes["wi"] = ((nb, IDX_HEADS, tq), F32, (tm // tq, IDX_HEADS, tq), blk)
    out_shape = [jax.ShapeDtypeStruct(shapes[k][0], shapes[k][1]) for k in names]
    out_specs = [pl.BlockSpec(shapes[k][2], shapes[k][3]) for k in names]
    outs = pl.pallas_call(
        functools.partial(_inproj_kernel, offs=offs),
        grid=(n // tm,),
        in_specs=[pl.BlockSpec((tm, d), lambda i: (i, 0)),
                  pl.BlockSpec((1, d), const),
                  pl.BlockSpec((d, d_cat), const),
                  pl.BlockSpec((1, ATT_W), const),
                  pl.BlockSpec((1, ATT_W), const),
                  pl.BlockSpec((ATT_W, ATT_W), const)],
        out_specs=out_specs,
        out_shape=out_shape,
        compiler_params=pltpu.CompilerParams(dimension_semantics=("arbitrary",),
                                             vmem_limit_bytes=VMEM_LIMIT),
        name="in_projection",
    )(x2, g_mix[None, :].astype(F32), w_cat, gq, gk, blockdiag)
    return dict(zip(names, outs))


def _gla_kernel(qa_ref, ka_ref, va_ref, ra_ref, sm_ref, ga_ref, wa2_ref, ba_ref, ltri_ref,
                lall_ref, gg_ref, wbr_ref, out_ref, st_ref, o_ref):
    @pl.when(pl.program_id(1) == 0)
    def _():
        st_ref[...] = jnp.zeros_like(st_ref)

    tb = qa_ref.shape[0]
    z = _dot(sm_ref[...].astype(BF16), wa2_ref[...]) + ba_ref[...]
    log_a = (jnp.minimum(z, 0.0) - jnp.log1p(jnp.exp(-jnp.abs(z)))) * (1.0 / GLA_TAU)
    la_hi, la_lo = _split_bf16(log_a)
    ltri = ltri_ref[...]
    lall = lall_ref[...]
    cum = _dot(ltri, la_hi) + _dot(ltri, la_lo)
    tot = _dot(lall, la_hi) + _dot(lall, la_lo)
    k_dec = (ka_ref[...].astype(F32) * jnp.exp(tot - cum)).astype(BF16)
    dec = jnp.exp(tot)

    rv = lax.broadcasted_iota(I32, (GLA_V_W, GLA_QK_W), 0)
    ck = lax.broadcasted_iota(I32, (GLA_V_W, GLA_QK_W), 1)
    same_head = (rv // GLA_DV) == (ck // GLA_DK)
    state = st_ref[...]
    for c in range(tb // CHUNK):
        rows = slice(c * CHUNK, (c + 1) * CHUNK)
        u_t = _dot_tn(va_ref[rows, :], k_dec[rows, :])
        state = state * dec[c * CHUNK:c * CHUNK + 1, :] + jnp.where(same_head, u_t, 0.0)
        o_ref[rows, :] = _dot_nt(qa_ref[rows, :], state.astype(BF16))
    st_ref[...] = state

    r = ra_ref[...].astype(F32)
    gated = []
    for h in range(GLA_HEADS):
        vl = slice(h * GLA_DV, (h + 1) * GLA_DV)
        oh = o_ref[:, vl]
        ms = jnp.mean(oh * oh, axis=-1, keepdims=True)
        oh = oh * lax.rsqrt(ms + EPS) * gg_ref[...]
        rh = r[:, vl]
        gated.append((oh * (rh * jax.nn.sigmoid(rh))).astype(BF16))
    og = jnp.concatenate(gated, axis=1)
    ya = _dot(og, wbr_ref[...])
    out_ref[...] = (jax.nn.sigmoid(ga_ref[...].astype(F32)) * ya).astype(BF16)


def _gla_mixer(p, w_alpha2, b_alpha, g_gla, w_br_gla, batch, seq):
    n = batch * seq
    d = w_br_gla.shape[1]
    tb = ROW_TILE
    nt = seq // tb
    wa2 = jnp.zeros((LANES, GLA_QK_W), F32).at[SM_ALR:SM_ALR + GLA_GATE_RANK].set(w_alpha2).astype(BF16)
    r = jnp.arange(tb)
    same = (r[:, None] // CHUNK) == (r[None, :] // CHUNK)
    ltri = (same & (r[None, :] <= r[:, None])).astype(BF16)
    lall = same.astype(BF16)
    row = lambda b, i: (b * nt + i, 0)
    const = lambda b, i: (0, 0)
    return pl.pallas_call(
        _gla_kernel,
        grid=(batch, nt),
        in_specs=[pl.BlockSpec((tb, GLA_QK_W), row), pl.BlockSpec((tb, GLA_QK_W), row),
                  pl.BlockSpec((tb, GLA_V_W), row), pl.BlockSpec((tb, GLA_V_W), row),
                  pl.BlockSpec((tb, LANES), row), pl.BlockSpec((tb, d), row),
                  pl.BlockSpec((LANES, GLA_QK_W), const), pl.BlockSpec((1, GLA_QK_W), const),
                  pl.BlockSpec((tb, tb), const), pl.BlockSpec((tb, tb), const),
                  pl.BlockSpec((1, GLA_DV), const), pl.BlockSpec((GLA_V_W, d), const)],
        out_specs=pl.BlockSpec((tb, d), row),
        out_shape=jax.ShapeDtypeStruct((n, d), BF16),
        scratch_shapes=[pltpu.VMEM((GLA_V_W, GLA_QK_W), F32),
                        pltpu.VMEM((tb, GLA_V_W), F32)],
        compiler_params=pltpu.CompilerParams(dimension_semantics=("arbitrary", "arbitrary"),
                                             vmem_limit_bytes=VMEM_LIMIT),
        name="gla_mixer",
    )(p["qa"], p["ka"], p["va"], p["ra"], p["sm"], p["ga"], wa2, b_alpha[None, :].astype(F32),
      ltri, lall, g_gla[None, :].astype(F32), w_br_gla.astype(BF16))


def _t5_bucket_int(rel):
    half = REL_BUCKETS // 2
    exact = half // 2
    n = jnp.abs(rel)
    large = jnp.full(rel.shape, exact, I32)
    for j in range(1, half - exact):
        thr = math.ceil(exact * (REL_MAX_DIST / exact) ** (j / (half - exact)) - 1e-9)
        large = large + jnp.where(n >= thr, 1, 0)
    return jnp.where(rel > 0, half, 0) + jnp.where(n < exact, n, large)


def _dsa_kernel(relb_ref, qit_ref, wit_ref, qbt_ref, gb_ref, ya_ref, kb_ref, vbt_ref, ki_ref,
                wbr_ref, out_ref, sc_ref, mb_ref, bias_ref, qp_ref, s_ref, mrow_ref, alpha_ref,
                acc_ref, ot_ref, *, k_top):
    b = pl.program_id(0)
    j = pl.program_id(1)
    tq = Q_TILE
    t0 = j * tq
    n_pairs = ATT_HEADS // 2
    wwin = WIN_BACK + tq
    sub = 8

    @pl.when((b == 0) & (j == 0))
    def _():
        rk = lax.broadcasted_iota(I32, (wwin, tq), 0)
        rq = lax.broadcasted_iota(I32, (wwin, tq), 1)
        bucket = _t5_bucket_int(rk - WIN_BACK - rq)
        far = REL_BUCKETS // 2 - 1
        for h in range(ATT_HEADS):
            a = jnp.zeros((wwin, tq), F32)
            for bk in range(REL_BUCKETS):
                a = jnp.where(bucket == bk, relb_ref[bk * ATT_HEADS + h], a)
            a = (a - relb_ref[far * ATT_HEADS + h]) * LOG2_E
            cols = slice((h % 2) * tq, (h % 2 + 1) * tq)
            bias_ref[0, h // 2, :, cols] = a
            bias_ref[1, h // 2, :, cols] = jnp.concatenate(
                [a[WIN_BACK:], jnp.zeros((WIN_BACK, tq), F32)], axis=0)

    row_q = lax.broadcasted_iota(I32, (LANES, tq), 0)
    for p in range(n_pairs):
        qpair = qbt_ref[0, p * LANES:(p + 1) * LANES, :]
        zero = jnp.zeros_like(qpair)
        qp_ref[p] = jnp.concatenate([jnp.where(row_q < ATT_DH, qpair, zero),
                                     jnp.where(row_q >= ATT_DH, qpair, zero)], axis=1)
    wi = wit_ref[0] * (IDX_HEADS ** -0.5 * IDX_DH ** -0.5)

    n_sel = (t0 + tq + S_TILE - 1) // S_TILE
    row_s = lax.broadcasted_iota(I32, (S_TILE, tq), 0)
    lane_s = lax.broadcasted_iota(I32, (S_TILE, tq), 1)
    limit = t0 + ((lane_s >> 6) + 1) * CHUNK

    stat_rows = 2 * sub

    def score_body(kt, carry):
        lo, hi, n_ge0, n_gt0 = carry
        ks = pl.multiple_of(kt * S_TILE, S_TILE)
        s_all = _dot(ki_ref[0, pl.ds(ks, S_TILE), :], qit_ref[0])
        score = jnp.zeros((S_TILE, tq), F32)
        for h in range(IDX_HEADS):
            score = score + jnp.maximum(s_all[:, h * tq:(h + 1) * tq], 0.0) * wi[h:h + 1, :]
        adm = ks + row_s < limit
        masked = jnp.where(adm, score, -jnp.inf)
        sc_ref[pl.ds(ks, S_TILE), :] = masked
        for g in range(S_TILE // stat_rows):
            rows = slice(g * stat_rows, (g + 1) * stat_rows)
            mg = masked[rows]
            hi = jnp.maximum(hi, mg)
            lo = jnp.minimum(lo, jnp.where(adm[rows], score[rows], jnp.inf))
            n_ge0 = n_ge0 + jnp.where(mg >= 0.0, 1.0, 0.0)
            n_gt0 = n_gt0 + jnp.where(mg > 0.0, 1.0, 0.0)
        return lo, hi, n_ge0, n_gt0
    zeros_s = jnp.zeros((stat_rows, tq), F32)
    lo_g, hi_g, ge0_g, gt0_g = lax.fori_loop(
        0, n_sel, score_body,
        (jnp.full((stat_rows, tq), jnp.inf, F32), jnp.full((stat_rows, tq), -jnp.inf, F32),
         zeros_s, zeros_s))

    n_cnt = (n_sel + 1) // 2

    @pl.when(n_sel % 2 == 1)
    def _():
        sc_ref[pl.ds(pl.multiple_of(n_sel * S_TILE, S_TILE), S_TILE), :] = jnp.full((S_TILE, tq), -jnp.inf, F32)

    kf = float(k_top)
    grp_rows = 8 * sub * LANES // tq
    c_groups = C_TILE // grp_rows

    def score_group(kt, g):
        return sc_ref[pl.ds(pl.multiple_of(kt * C_TILE + g * grp_rows, grp_rows), grp_rows), :]

    def key_sum(c):
        return jnp.sum(c, axis=0, keepdims=True)

    def count(cand, strict):
        def body(kt, c):
            for g in range(c_groups):
                sc = score_group(kt, g)
                c = c + jnp.where((sc > cand) if strict else (sc >= cand), 1.0, 0.0)
            return c
        return key_sum(lax.fori_loop(0, n_cnt, body, jnp.zeros((grp_rows, tq), F32)))

    row_min = jnp.min(lo_g, axis=0, keepdims=True)
    row_max = jnp.max(hi_g, axis=0, keepdims=True)
    n_adm = limit[0:1, :].astype(F32)
    c_ge0 = key_sum(ge0_g)
    c_gt0 = key_sum(gt0_g)
    has_thr = n_adm >= kf
    c_max = count(row_max, False)
    at_max = c_max >= kf
    at_zero = (c_gt0 < kf) & (c_ge0 >= kf)
    above_zero = c_gt0 >= kf
    lo0 = jnp.where(at_max, row_max, jnp.where(at_zero | above_zero, 0.0, row_min))
    cnt0 = jnp.where(at_max, c_max, jnp.where(at_zero | above_zero, c_ge0, n_adm))
    hi0 = jnp.where(at_zero | above_zero, row_max, 0.0)
    done0 = jnp.logical_not(has_thr) | at_max | at_zero | (cnt0 == kf)
    pre_done = jnp.where(done0, 1.0, 0.0)

    def bisect_step(lo, hi, cnt_lo):
        mid = 0.5 * lo + 0.5 * hi
        open_ = (mid > lo) & (mid < hi)
        c = count(mid, False)
        ge = c >= kf
        lo = jnp.where(ge, mid, lo)
        cnt_lo = jnp.where(ge, c, cnt_lo)
        hi = jnp.where(ge, hi, mid)
        return lo, hi, cnt_lo, jnp.where(open_ & (cnt_lo != kf), pre_done, 1.0)

    def bisect_cond(carry):
        return carry[3] < 0.5

    def bisect_body(carry):
        lo, hi, cnt_lo, _ = carry
        for _ in range(BISECT_STEPS):
            lo, hi, cnt_lo, conv = bisect_step(lo, hi, cnt_lo)
        return lo, hi, cnt_lo, jnp.min(conv)
    thr, _, cnt_thr, _ = lax.while_loop(
        bisect_cond, bisect_body, (lo0, hi0, cnt0, jnp.min(pre_done)))

    tie = has_thr & (cnt_thr > kf)
    any_tie = jnp.max(jnp.where(tie, 1.0, 0.0)) > 0.0
    far_end = t0 - WIN_BACK
    f32_min = float(jnp.finfo(F32).min)

    def write_mask(ks, width, sel):
        mb_ref[pl.ds(ks, width), :] = jnp.where(sel, 0.0, NEG)

    @pl.when(jnp.logical_not(any_tie))
    def _():
        lo = jnp.where(has_thr, thr, f32_min)

        def body(kt, _):
            ks = pl.multiple_of(kt * C_TILE, C_TILE)
            write_mask(ks, C_TILE, sc_ref[pl.ds(ks, C_TILE), :] >= lo)
            return 0
        lax.fori_loop(0, n_cnt, body, 0)

    @pl.when(any_tie)
    def _():
        need = kf - count(thr, True)
        lo = jnp.where(has_thr, thr, -jnp.inf)
        ri = lax.broadcasted_iota(I32, (C_TILE, C_TILE), 0)
        ci = lax.broadcasted_iota(I32, (C_TILE, C_TILE), 1)
        tri = jnp.where(ci <= ri, 1.0, 0.0).astype(BF16)

        def body(kt, seen):
            ks = pl.multiple_of(kt * C_TILE, C_TILE)
            sc = sc_ref[pl.ds(ks, C_TILE), :]
            eq = (sc == thr) & has_thr
            eq_f = jnp.where(eq, 1.0, 0.0)
            prefix = seen + _dot(tri, eq_f.astype(BF16))
            write_mask(ks, C_TILE, (sc > lo) | (eq & (prefix <= need)))
            return seen + key_sum(eq_f)
        lax.fori_loop(0, n_cnt, body, jnp.zeros((1, tq), F32))

    ws = pl.multiple_of(jnp.maximum(far_end, 0), WIN_BACK)
    n_far = (ws + K_TILE - 1) // K_TILE
    first = (j == 0).astype(I32)
    ones_rows = 16

    def pair_lanes(p):
        return slice(p * LANES, (p + 1) * LANES)

    def key_max(s):
        part = jnp.max(s.reshape(s.shape[0] // sub, sub, s.shape[1]), axis=0)
        return jnp.max(part, axis=0, keepdims=True)

    def values_t(h, ks, width):
        return jnp.concatenate([vbt_ref[0, h * ATT_DH:(h + 1) * ATT_DH, pl.ds(ks, width)],
                                jnp.ones((ones_rows, width), BF16)], axis=0)

    mrow_ref[...] = jnp.full(mrow_ref.shape, NEG, F32)
    acc_ref[...] = jnp.zeros_like(acc_ref)

    def logits_stage(slot, ks, width, bias):
        mbt = mb_ref[pl.ds(ks, width), :]
        if bias is None:
            pos = ks + lax.broadcasted_iota(I32, (width, tq), 0)
            mbt = jnp.where(pos < far_end, mbt, NEG)
        mb2 = jnp.concatenate([mbt, mbt], axis=1)
        for p in range(n_pairs):
            s = _dot(kb_ref[0, pl.ds(ks, width), pair_lanes(p)], qp_ref[p]) + mb2
            if bias is not None:
                s = s + bias(p)
            s_ref[slot, p, 0:width, :] = s
            m_old = mrow_ref[p]
            m_new = jnp.maximum(m_old, key_max(s))
            alpha_ref[p] = jnp.exp2(m_old - m_new)
            mrow_ref[p] = m_new

    def values_stage(slot, ks, width):
        for p in range(n_pairs):
            pexp = jnp.exp2(s_ref[slot, p, 0:width, :] - mrow_ref[p]).astype(BF16)
            alpha = alpha_ref[p]
            for h in (2 * p, 2 * p + 1):
                cols = slice((h % 2) * tq, (h % 2 + 1) * tq)
                acc_ref[h] = acc_ref[h] * alpha[:, cols] + _dot(values_t(h, ks, width), pexp[:, cols])

    def far_start(kt):
        return pl.multiple_of(kt * K_TILE, K_TILE)

    def window_logits():
        logits_stage(n_far % 2, ws, wwin, lambda p: bias_ref[first, p])

    @pl.when(n_far > 0)
    def _():
        logits_stage(0, far_start(0), K_TILE, None)

        def body(u, _):
            kt = 2 * u + 1
            values_stage(0, far_start(kt - 1), K_TILE)
            logits_stage(1, far_start(kt), K_TILE, None)
            values_stage(1, far_start(kt), K_TILE)
            logits_stage(0, far_start(kt + 1), K_TILE, None)
            return 0
        lax.fori_loop(0, (n_far - 1) // 2, body, 0)

        @pl.when((n_far - 1) % 2 == 1)
        def _():
            values_stage(0, far_start(n_far - 2), K_TILE)
            logits_stage(1, far_start(n_far - 1), K_TILE, None)
        values_stage((n_far - 1) % 2, far_start(n_far - 1), K_TILE)
        window_logits()

    @pl.when(n_far == 0)
    def _():
        window_logits()
    values_stage(n_far % 2, ws, wwin)

    for h in range(ATT_HEADS):
        a = acc_ref[h]
        ot_ref[h * ATT_DH:(h + 1) * ATT_DH, :] = (a[:ATT_DH, :] / a[ATT_DH:ATT_DH + 1, :]).astype(BF16)

    yb = _dot_tn(ot_ref[...], wbr_ref[...])
    mixed = jax.nn.sigmoid(gb_ref[...].astype(F32)) * yb + ya_ref[...].astype(F32)
    out_ref[...] = mixed.astype(BF16)


def _dsa_mixer(p, ya, rel_bias, w_br_att, batch, seq):
    n = batch * seq
    d = w_br_att.shape[1]
    tq = Q_TILE
    nq = seq // tq
    nb = n // tq
    k_top = min(TOPK_MAX, seq // 4)
    qit, wit, qbt, vbt = p["qi"], p["wi"], p["qb"], p["vb"]
    kb3 = p["kb"].reshape(batch, seq, ATT_W)
    ki3 = p["ki"].reshape(batch, seq, IDX_DH)
    blk = lambda b, j: (b * nq + j, 0, 0)
    row = lambda b, j: (b * nq + j, 0)
    per_b = lambda b, j: (b, 0, 0)
    const = lambda b, j: (0, 0)
    return pl.pallas_call(
        functools.partial(_dsa_kernel, k_top=k_top),
        grid=(batch, nq),
        in_specs=[pl.BlockSpec(memory_space=pltpu.SMEM),
                  pl.BlockSpec((1, IDX_DH, IDX_HEADS * tq), blk),
                  pl.BlockSpec((1, IDX_HEADS, tq), blk),
                  pl.BlockSpec((1, ATT_W, tq), blk),
                  pl.BlockSpec((tq, d), row), pl.BlockSpec((tq, d), row),
                  pl.BlockSpec((1, seq, ATT_W), per_b),
                  pl.BlockSpec((1, ATT_W, seq), per_b),
                  pl.BlockSpec((1, seq, IDX_DH), per_b),
                  pl.BlockSpec((ATT_W, d), const)],
        out_specs=pl.BlockSpec((tq, d), row),
        out_shape=jax.ShapeDtypeStruct((n, d), BF16),
        scratch_shapes=[pltpu.VMEM((seq, tq), F32),
                        pltpu.VMEM((seq, tq), F32),
                        pltpu.VMEM((2, ATT_HEADS // 2, WIN_BACK + tq, 2 * tq), F32),
                        pltpu.VMEM((ATT_HEADS // 2, LANES, 2 * tq), BF16),
                        pltpu.VMEM((2, ATT_HEADS // 2, K_TILE, 2 * tq), F32),
                        pltpu.VMEM((ATT_HEADS // 2, 1, 2 * tq), F32),
                        pltpu.VMEM((ATT_HEADS // 2, 1, 2 * tq), F32),
                        pltpu.VMEM((ATT_HEADS, ATT_DH + 16, tq), F32),
                        pltpu.VMEM((ATT_W, tq), BF16)],
        compiler_params=pltpu.CompilerParams(dimension_semantics=("arbitrary", "arbitrary"),
                                             vmem_limit_bytes=VMEM_LIMIT),
        name="dsa_mixer",
    )(rel_bias.astype(F32).reshape(-1), qit, wit, qbt, p["gb"], ya, kb3, vbt, ki3,
      w_br_att.astype(BF16))


RT_G0 = 0
RT_E0 = 8


def _moe_kernel(x_ref, mx_ref, wo_ref, gf_ref, wrh_ref, wrl_ref, br_ref, wg_ref, wu_ref,
                wd_ref, out_ref, h2_ref):
    tm = x_ref.shape[0]
    x1 = x_ref[...] + _dot(mx_ref[...], wo_ref[...])
    out_ref[...] = x1
    ms = jnp.mean(x1 * x1, axis=-1, keepdims=True)
    h = x1 * lax.rsqrt(ms + EPS) * gf_ref[...]
    h_hi, h_lo = _split_bf16(h)
    h2_ref[...] = h_hi
    logits = (_dot(h_hi, wrh_ref[...]) + _dot(h_lo, wrh_ref[...]) + _dot(h_hi, wrl_ref[...])
              + br_ref[...])
    lt = logits.T
    gl = lt[RT_G0:RT_G0 + N_GROUPS]
    row_g = lax.broadcasted_iota(I32, gl.shape, 0)
    gmax = jnp.max(gl, axis=0, keepdims=True)
    gidx = jnp.min(jnp.where(gl == gmax, row_g, N_GROUPS), axis=0, keepdims=True)
    g_w = 1.0 / jnp.sum(jnp.exp(gl - gmax), axis=0, keepdims=True)
    el = lt[RT_E0:RT_E0 + N_EXPERTS]
    row_e = lax.broadcasted_iota(I32, el.shape, 0)
    ev = jnp.where((row_e >> 3) == gidx, el, -jnp.inf)
    t1 = jnp.max(ev, axis=0, keepdims=True)
    i1 = jnp.min(jnp.where(ev == t1, row_e, N_EXPERTS), axis=0, keepdims=True)
    ev2 = jnp.where(row_e == i1, -jnp.inf, ev)
    t2 = jnp.max(ev2, axis=0, keepdims=True)
    i2 = jnp.min(jnp.where(ev2 == t2, row_e, N_EXPERTS), axis=0, keepdims=True)
    e2 = jnp.exp(t2 - t1)
    w1 = 1.0 / (1.0 + e2)
    w2 = e2 * w1
    comb_t = g_w * (jnp.where(row_e == i1, w1, 0.0) + jnp.where(row_e == i2, w2, 0.0))
    comb = jnp.concatenate([comb_t, jnp.zeros((LANES - N_EXPERTS, tm), F32)], axis=0).T

    h2 = h2_ref[...]
    per_chunk = MOE_COLS // D_EXPERT
    for grp in range(N_GROUPS):
        for c in range(EXPERTS_PER_GROUP // per_chunk):
            cols = slice(c * MOE_COLS, (c + 1) * MOE_COLS)
            hg = _dot(h2, wg_ref[grp, :, cols])
            hu = _dot(h2, wu_ref[grp, :, cols])
            e0 = grp * EXPERTS_PER_GROUP + c * per_chunk
            scale = jnp.concatenate(
                [jnp.broadcast_to(comb[:, e:e + 1], (tm, D_EXPERT)) for e in range(e0, e0 + per_chunk)],
                axis=1)
            hid = (hg * jax.nn.sigmoid(hg) * hu * scale).astype(BF16)
            out_ref[...] += _dot(hid, wd_ref[grp, cols, :])


def _out_proj_moe(x2, mixed, w_out, g_ffn, w_rg, b_rg, w_re, b_re, w_gate, w_up, w_down):
    n, d = x2.shape
    tm = ROW_TILE
    gw = EXPERTS_PER_GROUP * D_EXPERT
    w_r = jnp.zeros((d, LANES), F32).at[:, RT_G0:RT_G0 + N_GROUPS].set(w_rg)
    w_r = w_r.at[:, RT_E0:RT_E0 + N_EXPERTS].set(w_re)
    wr_hi = w_r.astype(BF16)
    wr_lo = (w_r - wr_hi.astype(F32)).astype(BF16)
    b_r = jnp.zeros((1, LANES), F32).at[0, RT_G0:RT_G0 + N_GROUPS].set(b_rg)
    b_r = b_r.at[0, RT_E0:RT_E0 + N_EXPERTS].set(b_re)

    def by_group(w):
        return (w.reshape(N_GROUPS, EXPERTS_PER_GROUP, d, D_EXPERT).transpose(0, 2, 1, 3)
                .reshape(N_GROUPS, d, gw).astype(BF16))
    wg = by_group(w_gate)
    wu = by_group(w_up)
    wd = w_down.reshape(N_GROUPS, gw, d).astype(BF16)

    row = lambda i: (i, 0)
    const = lambda i: (0, 0)
    const3 = lambda i: (0, 0, 0)
    once = pl.Buffered(1)
    return pl.pallas_call(
        _moe_kernel,
        grid=(n // tm,),
        in_specs=[pl.BlockSpec((tm, d), row), pl.BlockSpec((tm, d), row),
                  pl.BlockSpec((d, d), const, pipeline_mode=once), pl.BlockSpec((1, d), const),
                  pl.BlockSpec((d, LANES), const), pl.BlockSpec((d, LANES), const),
                  pl.BlockSpec((1, LANES), const),
                  pl.BlockSpec((N_GROUPS, d, gw), const3, pipeline_mode=once),
                  pl.BlockSpec((N_GROUPS, d, gw), const3, pipeline_mode=once),
                  pl.BlockSpec((N_GROUPS, gw, d), const3, pipeline_mode=once)],
        out_specs=pl.BlockSpec((tm, d), row),
        out_shape=jax.ShapeDtypeStruct((n, d), F32),
        scratch_shapes=[pltpu.VMEM((tm, d), BF16)],
        compiler_params=pltpu.CompilerParams(dimension_semantics=("arbitrary",),
                                             vmem_limit_bytes=VMEM_LIMIT),
        name="out_proj_moe",
    )(x2, mixed, w_out.astype(BF16), g_ffn[None, :].astype(F32), wr_hi, wr_lo, b_r,
      wg, wu, wd)


def kernel(x, g_mix, w_in, w_alpha2, b_alpha, g_gla, w_br_gla, g_q, g_k, rel_bias, w_br_att, w_out, g_ffn, w_rg, b_rg, w_re, b_re, w_gate, w_up, w_down):
    batch, seq, d = x.shape
    assert seq % ROW_TILE == 0 and seq % K_TILE == 0 and ROW_TILE % Q_TILE == 0
    x2 = x.reshape(batch * seq, d)
    for l in range(g_mix.shape[0]):
        p = _in_projection(x2, g_mix[l], w_in[l], g_q[l], g_k[l], seq)
        ya = _gla_mixer(p, w_alpha2[l], b_alpha[l], g_gla[l], w_br_gla[l], batch, seq)
        mixed = _dsa_mixer(p, ya, rel_bias, w_br_att[l], batch, seq)
        x2 = _out_proj_moe(x2, mixed, w_out[l], g_ffn[l], w_rg[l], b_rg[l], w_re[l], b_re[l],
                           w_gate[l], w_up[l], w_down[l])
    return x2.reshape(batch, seq, d)
```

```python
import functools
import math

import jax
import jax.numpy as jnp
from jax import lax
from jax.experimental import pallas as pl
from jax.experimental.pallas import tpu as pltpu

F32 = jnp.float32
BF16 = jnp.bfloat16
I32 = jnp.int32

CHUNK = 64
GLA_HEADS = 4
GLA_DK = 64
GLA_DV = 128
GLA_GATE_RANK = 16
GLA_TAU = 16.0
ATT_HEADS = 8
ATT_DH = 64
IDX_HEADS = 8
IDX_DH = 32
TOPK_MAX = 256
REL_BUCKETS = 32
REL_MAX_DIST = 128
N_GROUPS = 4
EXPERTS_PER_GROUP = 8
N_EXPERTS = N_GROUPS * EXPERTS_PER_GROUP
D_EXPERT = 128
EPS = 1e-6

GLA_QK_W = GLA_HEADS * GLA_DK
GLA_V_W = GLA_HEADS * GLA_DV
ATT_W = ATT_HEADS * ATT_DH
IDX_Q_W = IDX_HEADS * IDX_DH

LANES = 128
VMEM_LIMIT = 56 * 1024 * 1024

SM_KI = 0
SM_ALR = IDX_DH
SM_WI = IDX_DH + GLA_GATE_RANK

NEG = -1e30
LOG2_E = math.log2(math.e)

ROW_TILE = 512
Q_TILE = 256
WIN_BACK = 128
K_TILE = 512
MOE_COLS = 256
S_TILE = 512
C_TILE = S_TILE
BISECT_STEPS = 2


def _dot(a, b):
    return jnp.dot(a, b, preferred_element_type=F32)


def _dot_nt(a, b):
    return lax.dot_general(a, b, (((1,), (1,)), ((), ())), preferred_element_type=F32)


def _dot_tn(a, b):
    return lax.dot_general(a, b, (((0,), (0,)), ((), ())), preferred_element_type=F32)


def _split_bf16(a):
    hi = a.astype(BF16)
    lo = (a - hi.astype(F32)).astype(BF16)
    return hi, lo


_PROJ = (("qa", GLA_QK_W), ("ka", GLA_QK_W), ("va", GLA_V_W), ("ra", GLA_V_W),
         ("qb", ATT_W), ("kb", ATT_W), ("vb", ATT_W), ("qi", IDX_Q_W),
         ("ga", None), ("gb", None), ("sm", LANES))


def _inproj_kernel(x_ref, g_ref, w_ref, gq_ref, gk_ref, bd_ref, *out_refs, offs):
    x = x_ref[...]
    ms = jnp.mean(x * x, axis=-1, keepdims=True)
    hn = (x * lax.rsqrt(ms + EPS) * g_ref[...]).astype(BF16)

    def proj(name):
        c0, c1 = offs[name]
        return _dot(hn, w_ref[:, c0:c1])

    def head_norm(y, gain):
        ss = _dot((y * y).astype(BF16), bd_ref[...])
        return y * lax.rsqrt(ss * (1.0 / ATT_DH) + EPS) * gain

    (qa_ref, ka_ref, va_ref, ra_ref, qbt_ref, kb_ref, vbt_ref, qit_ref, ga_ref, gb_ref, sm_ref,
     ki_ref, wit_ref) = out_refs
    tq = Q_TILE
    n_blk = x.shape[0] // tq

    def blocks_t(y):
        return [y[r * tq:(r + 1) * tq, :].T for r in range(n_blk)]

    qa_ref[...] = (proj("qa") * (GLA_DK ** -0.5)).astype(BF16)
    ka_ref[...] = proj("ka").astype(BF16)
    va_ref[...] = proj("va").astype(BF16)
    ra_ref[...] = proj("ra").astype(BF16)
    qb = head_norm(proj("qb"), gq_ref[...]) * (ATT_DH ** -0.5 * LOG2_E)
    for r, yt in enumerate(blocks_t(qb)):
        qbt_ref[r] = yt.astype(BF16)
    kb_ref[...] = head_norm(proj("kb"), gk_ref[...]).astype(BF16)
    vbt_ref[0] = jnp.concatenate(blocks_t(proj("vb")), axis=1).astype(BF16)
    for r, yt in enumerate(blocks_t(proj("qi"))):
        qit_ref[r] = jnp.concatenate(
            [yt[h * IDX_DH:(h + 1) * IDX_DH, :] for h in range(IDX_HEADS)], axis=1).astype(BF16)
    ga_ref[...] = proj("ga").astype(BF16)
    gb_ref[...] = proj("gb").astype(BF16)
    small = proj("sm")
    sm_ref[...] = small
    ki_ref[...] = small[:, SM_KI:SM_KI + IDX_DH].astype(BF16)
    for r, yt in enumerate(blocks_t(small)):
        wit_ref[r] = yt[SM_WI:SM_WI + IDX_HEADS, :]


def _in_projection(x2, g_mix, w_in, g_q, g_k, seq):
    n, d = x2.shape
    widths = {name: (d if w is None else w) for name, w in _PROJ}
    ref_order = (("qa", GLA_QK_W), ("ka", GLA_QK_W), ("va", GLA_V_W), ("ra", GLA_V_W),
                 ("alr", GLA_GATE_RANK), ("qb", ATT_W), ("kb", ATT_W), ("vb", ATT_W),
                 ("qi", IDX_Q_W), ("ki", IDX_DH), ("wi", IDX_HEADS), ("ga", d), ("gb", d))
    cols, c = {}, 0
    for name, w in ref_order:
        cols[name] = w_in[:, c:c + w]
        c += w
    small = jnp.concatenate(
        [cols["ki"], cols["alr"], cols["wi"],
         jnp.zeros((d, LANES - GLA_GATE_RANK - IDX_DH - IDX_HEADS), w_in.dtype)], axis=1)
    cols["sm"] = small
    w_cat = jnp.concatenate([cols[name] for name, _ in _PROJ], axis=1).astype(BF16)
    offs, c = {}, 0
    for name, _ in _PROJ:
        offs[name] = (c, c + widths[name])
        c += widths[name]
    d_cat = c

    head = jnp.arange(ATT_W) // ATT_DH
    blockdiag = (head[:, None] == head[None, :]).astype(BF16)
    gq = jnp.tile(g_q, ATT_HEADS)[None, :].astype(F32)
    gk = jnp.tile(g_k, ATT_HEADS)[None, :].astype(F32)

    tm = ROW_TILE
    tq = Q_TILE
    nt = seq // tm
    nb = n // tq
    const = lambda i: (0, 0)
    row = lambda i: (i, 0)
    blk = lambda i: (i, 0, 0)
    shapes = {name: ((n, widths[name]), F32 if name == "sm" else BF16, (tm, widths[name]), row)
              for name, _ in _PROJ}
    shapes["qb"] = ((nb, ATT_W, tq), BF16, (tm // tq, ATT_W, tq), blk)
    shapes["vb"] = ((n // seq, ATT_W, seq), BF16, (1, ATT_W, tm), lambda i: (i // nt, 0, i % nt))
    shapes["qi"] = ((nb, IDX_DH, IDX_HEADS * tq), BF16, (tm // tq, IDX_DH, IDX_HEADS * tq), blk)
    names = [name for name, _ in _PROJ] + ["ki", "wi"]
    shapes["ki"] = ((n, IDX_DH), BF16, (tm, IDX_DH), row)
    shapes["wi"] = ((nb, IDX_HEADS, tq), F32, (tm // tq, IDX_HEADS, tq), blk)
    out_shape = [jax.ShapeDtypeStruct(shapes[k][0], shapes[k][1]) for k in names]
    out_specs = [pl.BlockSpec(shapes[k][2], shapes[k][3]) for k in names]
    outs = pl.pallas_call(
        functools.partial(_inproj_kernel, offs=offs),
        grid=(n // tm,),
        in_specs=[pl.BlockSpec((tm, d), lambda i: (i, 0)),
                  pl.BlockSpec((1, d), const),
                  pl.BlockSpec((d, d_cat), const),
                  pl.BlockSpec((1, ATT_W), const),
                  pl.BlockSpec((1, ATT_W), const),
                  pl.BlockSpec((ATT_W, ATT_W), const)],
        out_specs=out_specs,
        out_shape=out_shape,
        compiler_params=pltpu.CompilerParams(dimension_semantics=("arbitrary",),
                                             vmem_limit_bytes=VMEM_LIMIT),
        name="in_projection",
    )(x2, g_mix[None, :].astype(F32), w_cat, gq, gk, blockdiag)
    return dict(zip(names, outs))


def _gla_kernel(qa_ref, ka_ref, va_ref, ra_ref, sm_ref, ga_ref, wa2_ref, ba_ref, ltri_ref,
                lall_ref, gg_ref, wbr_ref, out_ref, st_ref, o_ref):
    @pl.when(pl.program_id(1) == 0)
    def _():
        st_ref[...] = jnp.zeros_like(st_ref)

    tb = qa_ref.shape[0]
    z = _dot(sm_ref[...].astype(BF16), wa2_ref[...]) + ba_ref[...]
    log_a = (jnp.minimum(z, 0.0) - jnp.log1p(jnp.exp(-jnp.abs(z)))) * (1.0 / GLA_TAU)
    la_hi, la_lo = _split_bf16(log_a)
    ltri = ltri_ref[...]
    lall = lall_ref[...]
    cum = _dot(ltri, la_hi) + _dot(ltri, la_lo)
    tot = _dot(lall, la_hi) + _dot(lall, la_lo)
    k_dec = (ka_ref[...].astype(F32) * jnp.exp(tot - cum)).astype(BF16)
    dec = jnp.exp(tot)

    rv = lax.broadcasted_iota(I32, (GLA_V_W, GLA_QK_W), 0)
    ck = lax.broadcasted_iota(I32, (GLA_V_W, GLA_QK_W), 1)
    same_head = (rv // GLA_DV) == (ck // GLA_DK)
    state = st_ref[...]
    for c in range(tb // CHUNK):
        rows = slice(c * CHUNK, (c + 1) * CHUNK)
        u_t = _dot_tn(va_ref[rows, :], k_dec[rows, :])
        state = state * dec[c * CHUNK:c * CHUNK + 1, :] + jnp.where(same_head, u_t, 0.0)
        o_ref[rows, :] = _dot_nt(qa_ref[rows, :], state.astype(BF16))
    st_ref[...] = state

    r = ra_ref[...].astype(F32)
    gated = []
    for h in range(GLA_HEADS):
        vl = slice(h * GLA_DV, (h + 1) * GLA_DV)
        oh = o_ref[:, vl]
        ms = jnp.mean(oh * oh, axis=-1, keepdims=True)
        oh = oh * lax.rsqrt(ms + EPS) * gg_ref[...]
        rh = r[:, vl]
        gated.append((oh * (rh * jax.nn.sigmoid(rh))).astype(BF16))
    og = jnp.concatenate(gated, axis=1)
    ya = _dot(og, wbr_ref[...])
    out_ref[...] = (jax.nn.sigmoid(ga_ref[...].astype(F32)) * ya).astype(BF16)


def _gla_mixer(p, w_alpha2, b_alpha, g_gla, w_br_gla, batch, seq):
    n = batch * seq
    d = w_br_gla.shape[1]
    tb = ROW_TILE
    nt = seq // tb
    wa2 = jnp.zeros((LANES, GLA_QK_W), F32).at[SM_ALR:SM_ALR + GLA_GATE_RANK].set(w_alpha2).astype(BF16)
    r = jnp.arange(tb)
    same = (r[:, None] // CHUNK) == (r[None, :] // CHUNK)
    ltri = (same & (r[None, :] <= r[:, None])).astype(BF16)
    lall = same.astype(BF16)
    row = lambda b, i: (b * nt + i, 0)
    const = lambda b, i: (0, 0)
    return pl.pallas_call(
        _gla_kernel,
        grid=(batch, nt),
        in_specs=[pl.BlockSpec((tb, GLA_QK_W), row), pl.BlockSpec((tb, GLA_QK_W), row),
                  pl.BlockSpec((tb, GLA_V_W), row), pl.BlockSpec((tb, GLA_V_W), row),
                  pl.BlockSpec((tb, LANES), row), pl.BlockSpec((tb, d), row),
                  pl.BlockSpec((LANES, GLA_QK_W), const), pl.BlockSpec((1, GLA_QK_W), const),
                  pl.BlockSpec((tb, tb), const), pl.BlockSpec((tb, tb), const),
                  pl.BlockSpec((1, GLA_DV), const), pl.BlockSpec((GLA_V_W, d), const)],
        out_specs=pl.BlockSpec((tb, d), row),
        out_shape=jax.ShapeDtypeStruct((n, d), BF16),
        scratch_shapes=[pltpu.VMEM((GLA_V_W, GLA_QK_W), F32),
                        pltpu.VMEM((tb, GLA_V_W), F32)],
        compiler_params=pltpu.CompilerParams(dimension_semantics=("arbitrary", "arbitrary"),
                                             vmem_limit_bytes=VMEM_LIMIT),
        name="gla_mixer",
    )(p["qa"], p["ka"], p["va"], p["ra"], p["sm"], p["ga"], wa2, b_alpha[None, :].astype(F32),
      ltri, lall, g_gla[None, :].astype(F32), w_br_gla.astype(BF16))


def _t5_bucket_int(rel):
    half = REL_BUCKETS // 2
    exact = half // 2
    n = jnp.abs(rel)
    large = jnp.full(rel.shape, exact, I32)
    for j in range(1, half - exact):
        thr = math.ceil(exact * (REL_MAX_DIST / exact) ** (j / (half - exact)) - 1e-9)
        large = large + jnp.where(n >= thr, 1, 0)
    return jnp.where(rel > 0, half, 0) + jnp.where(n < exact, n, large)


def _dsa_kernel(relb_ref, qit_ref, wit_ref, qbt_ref, gb_ref, ya_ref, kb_ref, vbt_ref, ki_ref,
                wbr_ref, out_ref, sc_ref, mb_ref, bias_ref, qp_ref, s_ref, mrow_ref, alpha_ref,
                acc_ref, ot_ref, *, k_top):
    b = pl.program_id(0)
    j = pl.program_id(1)
    tq = Q_TILE
    t0 = j * tq
    n_pairs = ATT_HEADS // 2
    wwin = WIN_BACK + tq
    sub = 8

    @pl.when((b == 0) & (j == 0))
    def _():
        rk = lax.broadcasted_iota(I32, (wwin, tq), 0)
        rq = lax.broadcasted_iota(I32, (wwin, tq), 1)
        bucket = _t5_bucket_int(rk - WIN_BACK - rq)
        far = REL_BUCKETS // 2 - 1
        for h in range(ATT_HEADS):
            a = jnp.zeros((wwin, tq), F32)
            for bk in range(REL_BUCKETS):
                a = jnp.where(bucket == bk, relb_ref[bk * ATT_HEADS + h], a)
            a = (a - relb_ref[far * ATT_HEADS + h]) * LOG2_E
            cols = slice((h % 2) * tq, (h % 2 + 1) * tq)
            bias_ref[0, h // 2, :, cols] = a
            bias_ref[1, h // 2, :, cols] = jnp.concatenate(
                [a[WIN_BACK:], jnp.zeros((WIN_BACK, tq), F32)], axis=0)

    row_q = lax.broadcasted_iota(I32, (LANES, tq), 0)
    for p in range(n_pairs):
        qpair = qbt_ref[0, p * LANES:(p + 1) * LANES, :]
        zero = jnp.zeros_like(qpair)
        qp_ref[p] = jnp.concatenate([jnp.where(row_q < ATT_DH, qpair, zero),
                                     jnp.where(row_q >= ATT_DH, qpair, zero)], axis=1)
    wi = wit_ref[0] * (IDX_HEADS ** -0.5 * IDX_DH ** -0.5)

    n_sel = (t0 + tq + S_TILE - 1) // S_TILE
    row_s = lax.broadcasted_iota(I32, (S_TILE, tq), 0)
    lane_s = lax.broadcasted_iota(I32, (S_TILE, tq), 1)
    limit = t0 + ((lane_s >> 6) + 1) * CHUNK

    stat_rows = 2 * sub

    def score_body(kt, carry):
        lo, hi, n_ge0, n_gt0 = carry
        ks = pl.multiple_of(kt * S_TILE, S_TILE)
        s_all = _dot(ki_ref[0, pl.ds(ks, S_TILE), :], qit_ref[0])
        score = jnp.zeros((S_TILE, tq), F32)
        for h in range(IDX_HEADS):
            score = score + jnp.maximum(s_all[:, h * tq:(h + 1) * tq], 0.0) * wi[h:h + 1, :]
        adm = ks + row_s < limit
        masked = jnp.where(adm, score, -jnp.inf)
        sc_ref[pl.ds(ks, S_TILE), :] = masked
        for g in range(S_TILE // stat_rows):
            rows = slice(g * stat_rows, (g + 1) * stat_rows)
            mg = masked[rows]
            hi = jnp.maximum(hi, mg)
            lo = jnp.minimum(lo, jnp.where(adm[rows], score[rows], jnp.inf))
            n_ge0 = n_ge0 + jnp.where(mg >= 0.0, 1.0, 0.0)
            n_gt0 = n_gt0 + jnp.where(mg > 0.0, 1.0, 0.0)
        return lo, hi, n_ge0, n_gt0
    zeros_s = jnp.zeros((stat_rows, tq), F32)
    lo_g, hi_g, ge0_g, gt0_g = lax.fori_loop(
        0, n_sel, score_body,
        (jnp.full((stat_rows, tq), jnp.inf, F32), jnp.full((stat_rows, tq), -jnp.inf, F32),
         zeros_s, zeros_s))

    assert C_TILE == S_TILE
    n_cnt = n_sel

    kf = float(k_top)
    grp_rows = 8 * sub * LANES // tq
    c_groups = C_TILE // grp_rows

    def score_group(kt, g):
        return sc_ref[pl.ds(pl.multiple_of(kt * C_TILE + g * grp_rows, grp_rows), grp_rows), :]

    def key_sum(c):
        return jnp.sum(c, axis=0, keepdims=True)

    def count(cand, strict):
        def body(kt, c):
            for g in range(c_groups):
                sc = score_group(kt, g)
                c = c + jnp.where((sc > cand) if strict else (sc >= cand), 1.0, 0.0)
            return c
        return key_sum(lax.fori_loop(0, n_cnt, body, jnp.zeros((grp_rows, tq), F32)))

    row_min = jnp.min(lo_g, axis=0, keepdims=True)
    row_max = jnp.max(hi_g, axis=0, keepdims=True)
    n_adm = limit[0:1, :].astype(F32)
    c_ge0 = key_sum(ge0_g)
    c_gt0 = key_sum(gt0_g)
    has_thr = n_adm >= kf
    at_zero = (c_gt0 < kf) & (c_ge0 >= kf)
    above_zero = c_gt0 >= kf
    lo0 = jnp.where(at_zero | above_zero, 0.0, row_min)
    cnt0 = jnp.where(at_zero | above_zero, c_ge0, n_adm)
    hi0 = jnp.where(at_zero | above_zero, 2.0 * row_max, 0.0)
    done0 = jnp.logical_not(has_thr) | at_zero | (cnt0 == kf)
    pre_done = jnp.where(done0, 1.0, 0.0)

    def bisect_step(lo, hi, cnt_lo):
        mid = 0.5 * lo + 0.5 * hi
        open_ = (mid > lo) & (mid < hi)
        c = count(mid, False)
        ge = c >= kf
        lo = jnp.where(ge, mid, lo)
        cnt_lo = jnp.where(ge, c, cnt_lo)
        hi = jnp.where(ge, hi, mid)
        return lo, hi, cnt_lo, jnp.where(open_ & (cnt_lo != kf), pre_done, 1.0)

    def bisect_cond(carry):
        return carry[3] < 0.5

    def bisect_body(carry):
        lo, hi, cnt_lo, _ = carry
        for _ in range(BISECT_STEPS):
            lo, hi, cnt_lo, conv = bisect_step(lo, hi, cnt_lo)
        return lo, hi, cnt_lo, jnp.min(conv)
    thr, _, cnt_thr, _ = lax.while_loop(
        bisect_cond, bisect_body, (lo0, hi0, cnt0, jnp.min(pre_done)))

    tie = has_thr & (cnt_thr > kf)
    any_tie = jnp.max(jnp.where(tie, 1.0, 0.0)) > 0.0
    far_end = t0 - WIN_BACK
    f32_min = float(jnp.finfo(F32).min)

    def write_mask(ks, width, sel):
        mb_ref[pl.ds(ks, width), :] = jnp.where(sel, 0.0, NEG)

    @pl.when(jnp.logical_not(any_tie))
    def _():
        lo = jnp.where(has_thr, thr, f32_min)

        def body(kt, _):
            ks = pl.multiple_of(kt * C_TILE, C_TILE)
            write_mask(ks, C_TILE, sc_ref[pl.ds(ks, C_TILE), :] >= lo)
            return 0
        lax.fori_loop(0, n_cnt, body, 0)

    @pl.when(any_tie)
    def _():
        need = kf - count(thr, True)
        lo = jnp.where(has_thr, thr, -jnp.inf)
        ri = lax.broadcasted_iota(I32, (C_TILE, C_TILE), 0)
        ci = lax.broadcasted_iota(I32, (C_TILE, C_TILE), 1)
        tri = jnp.where(ci <= ri, 1.0, 0.0).astype(BF16)

        def body(kt, seen):
            ks = pl.multiple_of(kt * C_TILE, C_TILE)
            sc = sc_ref[pl.ds(ks, C_TILE), :]
            eq = (sc == thr) & has_thr
            eq_f = jnp.where(eq, 1.0, 0.0)
            prefix = seen + _dot(tri, eq_f.astype(BF16))
            write_mask(ks, C_TILE, (sc > lo) | (eq & (prefix <= need)))
            return seen + key_sum(eq_f)
        lax.fori_loop(0, n_cnt, body, jnp.zeros((1, tq), F32))

    ws = pl.multiple_of(jnp.maximum(far_end, 0), WIN_BACK)
    n_far = (ws + K_TILE - 1) // K_TILE
    first = (j == 0).astype(I32)
    ones_rows = 16

    def pair_lanes(p):
        return slice(p * LANES, (p + 1) * LANES)

    def key_max(s):
        part = jnp.max(s.reshape(s.shape[0] // sub, sub, s.shape[1]), axis=0)
        return jnp.max(part, axis=0, keepdims=True)

    def values_t(h, ks, width):
        return jnp.concatenate([vbt_ref[0, h * ATT_DH:(h + 1) * ATT_DH, pl.ds(ks, width)],
                                jnp.ones((ones_rows, width), BF16)], axis=0)

    mrow_ref[...] = jnp.full(mrow_ref.shape, NEG, F32)
    acc_ref[...] = jnp.zeros_like(acc_ref)

    def logits_stage(slot, ks, width, bias):
        mbt = mb_ref[pl.ds(ks, width), :]
        if bias is None:
            pos = ks + lax.broadcasted_iota(I32, (width, tq), 0)
            mbt = jnp.where(pos < far_end, mbt, NEG)
        mb2 = jnp.concatenate([mbt, mbt], axis=1)
        for p in range(n_pairs):
            s = _dot(kb_ref[0, pl.ds(ks, width), pair_lanes(p)], qp_ref[p]) + mb2
            if bias is not None:
                s = s + bias(p)
            s_ref[slot, p, 0:width, :] = s
            m_old = mrow_ref[p]
            m_new = jnp.maximum(m_old, key_max(s))
            alpha_ref[p] = jnp.exp2(m_old - m_new)
            mrow_ref[p] = m_new

    def values_stage(slot, ks, width):
        for p in range(n_pairs):
            pexp = jnp.exp2(s_ref[slot, p, 0:width, :] - mrow_ref[p]).astype(BF16)
            alpha = alpha_ref[p]
            for h in (2 * p, 2 * p + 1):
                cols = slice((h % 2) * tq, (h % 2 + 1) * tq)
                acc_ref[h] = acc_ref[h] * alpha[:, cols] + _dot(values_t(h, ks, width), pexp[:, cols])

    def far_start(kt):
        return pl.multiple_of(kt * K_TILE, K_TILE)

    def window_logits():
        logits_stage(n_far % 2, ws, wwin, lambda p: bias_ref[first, p])

    @pl.when(n_far > 0)
    def _():
        logits_stage(0, far_start(0), K_TILE, None)

        def body(u, _):
            kt = 2 * u + 1
            values_stage(0, far_start(kt - 1), K_TILE)
            logits_stage(1, far_start(kt), K_TILE, None)
            values_stage(1, far_start(kt), K_TILE)
            logits_stage(0, far_start(kt + 1), K_TILE, None)
            return 0
        lax.fori_loop(0, (n_far - 1) // 2, body, 0)

        @pl.when((n_far - 1) % 2 == 1)
        def _():
            values_stage(0, far_start(n_far - 2), K_TILE)
            logits_stage(1, far_start(n_far - 1), K_TILE, None)
        values_stage((n_far - 1) % 2, far_start(n_far - 1), K_TILE)
        window_logits()

    @pl.when(n_far == 0)
    def _():
        window_logits()
    values_stage(n_far % 2, ws, wwin)

    for h in range(ATT_HEADS):
        a = acc_ref[h]
        ot_ref[h * ATT_DH:(h + 1) * ATT_DH, :] = (a[:ATT_DH, :] / a[ATT_DH:ATT_DH + 1, :]).astype(BF16)

    yb = _dot_tn(ot_ref[...], wbr_ref[...])
    mixed = jax.nn.sigmoid(gb_ref[...].astype(F32)) * yb + ya_ref[...].astype(F32)
    out_ref[...] = mixed.astype(BF16)


def _dsa_mixer(p, ya, rel_bias, w_br_att, batch, seq):
    n = batch * seq
    d = w_br_att.shape[1]
    tq = Q_TILE
    nq = seq // tq
    nb = n // tq
    k_top = min(TOPK_MAX, seq // 4)
    qit, wit, qbt, vbt = p["qi"], p["wi"], p["qb"], p["vb"]
    kb3 = p["kb"].reshape(batch, seq, ATT_W)
    ki3 = p["ki"].reshape(batch, seq, IDX_DH)
    blk = lambda b, j: (b * nq + j, 0, 0)
    row = lambda b, j: (b * nq + j, 0)
    per_b = lambda b, j: (b, 0, 0)
    const = lambda b, j: (0, 0)
    return pl.pallas_call(
        functools.partial(_dsa_kernel, k_top=k_top),
        grid=(batch, nq),
        in_specs=[pl.BlockSpec(memory_space=pltpu.SMEM),
                  pl.BlockSpec((1, IDX_DH, IDX_HEADS * tq), blk),
                  pl.BlockSpec((1, IDX_HEADS, tq), blk),
                  pl.BlockSpec((1, ATT_W, tq), blk),
                  pl.BlockSpec((tq, d), row), pl.BlockSpec((tq, d), row),
                  pl.BlockSpec((1, seq, ATT_W), per_b),
                  pl.BlockSpec((1, ATT_W, seq), per_b),
                  pl.BlockSpec((1, seq, IDX_DH), per_b),
                  pl.BlockSpec((ATT_W, d), const)],
        out_specs=pl.BlockSpec((tq, d), row),
        out_shape=jax.ShapeDtypeStruct((n, d), BF16),
        scratch_shapes=[pltpu.VMEM((seq, tq), F32),
                        pltpu.VMEM((seq, tq), F32),
                        pltpu.VMEM((2, ATT_HEADS // 2, WIN_BACK + tq, 2 * tq), F32),
                        pltpu.VMEM((ATT_HEADS // 2, LANES, 2 * tq), BF16),
                        pltpu.VMEM((2, ATT_HEADS // 2, K_TILE, 2 * tq), F32),
                        pltpu.VMEM((ATT_HEADS // 2, 1, 2 * tq), F32),
                        pltpu.VMEM((ATT_HEADS // 2, 1, 2 * tq), F32),
                        pltpu.VMEM((ATT_HEADS, ATT_DH + 16, tq), F32),
                        pltpu.VMEM((ATT_W, tq), BF16)],
        compiler_params=pltpu.CompilerParams(dimension_semantics=("arbitrary", "arbitrary"),
                                             vmem_limit_bytes=VMEM_LIMIT),
        name="dsa_mixer",
    )(rel_bias.astype(F32).reshape(-1), qit, wit, qbt, p["gb"], ya, kb3, vbt, ki3,
      w_br_att.astype(BF16))


RT_G0 = 0
RT_E0 = 8


def _moe_kernel(x_ref, mx_ref, wo_ref, gf_ref, wrh_ref, wrl_ref, br_ref, wg_ref, wu_ref,
                wd_ref, out_ref, h2_ref):
    tm = x_ref.shape[0]
    x1 = x_ref[...] + _dot(mx_ref[...], wo_ref[...])
    out_ref[...] = x1
    ms = jnp.mean(x1 * x1, axis=-1, keepdims=True)
    h = x1 * lax.rsqrt(ms + EPS) * gf_ref[...]
    h_hi, h_lo = _split_bf16(h)
    h2_ref[...] = h_hi
    logits = (_dot(h_hi, wrh_ref[...]) + _dot(h_lo, wrh_ref[...]) + _dot(h_hi, wrl_ref[...])
              + br_ref[...])
    lt = logits.T
    gl = lt[RT_G0:RT_G0 + N_GROUPS]
    row_g = lax.broadcasted_iota(I32, gl.shape, 0)
    gmax = jnp.max(gl, axis=0, keepdims=True)
    gidx = jnp.min(jnp.where(gl == gmax, row_g, N_GROUPS), axis=0, keepdims=True)
    g_w = 1.0 / jnp.sum(jnp.exp(gl - gmax), axis=0, keepdims=True)
    el = lt[RT_E0:RT_E0 + N_EXPERTS]
    row_e = lax.broadcasted_iota(I32, el.shape, 0)
    ev = jnp.where((row_e >> 3) == gidx, el, -jnp.inf)
    t1 = jnp.max(ev, axis=0, keepdims=True)
    i1 = jnp.min(jnp.where(ev == t1, row_e, N_EXPERTS), axis=0, keepdims=True)
    ev2 = jnp.where(row_e == i1, -jnp.inf, ev)
    t2 = jnp.max(ev2, axis=0, keepdims=True)
    i2 = jnp.min(jnp.where(ev2 == t2, row_e, N_EXPERTS), axis=0, keepdims=True)
    e2 = jnp.exp(t2 - t1)
    w1 = 1.0 / (1.0 + e2)
    w2 = e2 * w1
    comb_t = g_w * (jnp.where(row_e == i1, w1, 0.0) + jnp.where(row_e == i2, w2, 0.0))
    comb = jnp.concatenate([comb_t, jnp.zeros((LANES - N_EXPERTS, tm), F32)], axis=0).T

    h2 = h2_ref[...]
    per_chunk = MOE_COLS // D_EXPERT
    for grp in range(N_GROUPS):
        for c in range(EXPERTS_PER_GROUP // per_chunk):
            cols = slice(c * MOE_COLS, (c + 1) * MOE_COLS)
            hg = _dot(h2, wg_ref[grp, :, cols])
            hu = _dot(h2, wu_ref[grp, :, cols])
            e0 = grp * EXPERTS_PER_GROUP + c * per_chunk
            scale = jnp.concatenate(
                [jnp.broadcast_to(comb[:, e:e + 1], (tm, D_EXPERT)) for e in range(e0, e0 + per_chunk)],
                axis=1)
            hid = (hg * jax.nn.sigmoid(hg) * hu * scale).astype(BF16)
            out_ref[...] += _dot(hid, wd_ref[grp, cols, :])


def _out_proj_moe(x2, mixed, w_out, g_ffn, w_rg, b_rg, w_re, b_re, w_gate, w_up, w_down):
    n, d = x2.shape
    tm = ROW_TILE
    gw = EXPERTS_PER_GROUP * D_EXPERT
    w_r = jnp.zeros((d, LANES), F32).at[:, RT_G0:RT_G0 + N_GROUPS].set(w_rg)
    w_r = w_r.at[:, RT_E0:RT_E0 + N_EXPERTS].set(w_re)
    wr_hi = w_r.astype(BF16)
    wr_lo = (w_r - wr_hi.astype(F32)).astype(BF16)
    b_r = jnp.zeros((1, LANES), F32).at[0, RT_G0:RT_G0 + N_GROUPS].set(b_rg)
    b_r = b_r.at[0, RT_E0:RT_E0 + N_EXPERTS].set(b_re)

    def by_group(w):
        return (w.reshape(N_GROUPS, EXPERTS_PER_GROUP, d, D_EXPERT).transpose(0, 2, 1, 3)
                .reshape(N_GROUPS, d, gw).astype(BF16))
    wg = by_group(w_gate)
    wu = by_group(w_up)
    wd = w_down.reshape(N_GROUPS, gw, d).astype(BF16)

    row = lambda i: (i, 0)
    const = lambda i: (0, 0)
    const3 = lambda i: (0, 0, 0)
    once = pl.Buffered(1)
    return pl.pallas_call(
        _moe_kernel,
        grid=(n // tm,),
        in_specs=[pl.BlockSpec((tm, d), row), pl.BlockSpec((tm, d), row),
                  pl.BlockSpec((d, d), const, pipeline_mode=once), pl.BlockSpec((1, d), const),
                  pl.BlockSpec((d, LANES), const), pl.BlockSpec((d, LANES), const),
                  pl.BlockSpec((1, LANES), const),
                  pl.BlockSpec((N_GROUPS, d, gw), const3, pipeline_mode=once),
                  pl.BlockSpec((N_GROUPS, d, gw), const3, pipeline_mode=once),
                  pl.BlockSpec((N_GROUPS, gw, d), const3, pipeline_mode=once)],
        out_specs=pl.BlockSpec((tm, d), row),
        out_shape=jax.ShapeDtypeStruct((n, d), F32),
        scratch_shapes=[pltpu.VMEM((tm, d), BF16)],
        compiler_params=pltpu.CompilerParams(dimension_semantics=("arbitrary",),
                                             vmem_limit_bytes=VMEM_LIMIT),
        name="out_proj_moe",
    )(x2, mixed, w_out.astype(BF16), g_ffn[None, :].astype(F32), wr_hi, wr_lo, b_r,
      wg, wu, wd)


def kernel(x, g_mix, w_in, w_alpha2, b_alpha, g_gla, w_br_gla, g_q, g_k, rel_bias, w_br_att, w_out, g_ffn, w_rg, b_rg, w_re, b_re, w_gate, w_up, w_down):
    batch, seq, d = x.shape
    assert seq % ROW_TILE == 0 and seq % K_TILE == 0 and ROW_TILE % Q_TILE == 0
    x2 = x.reshape(batch * seq, d)
    for l in range(g_mix.shape[0]):
        p = _in_projection(x2, g_mix[l], w_in[l], g_q[l], g_k[l], seq)
        ya = _gla_mixer(p, w_alpha2[l], b_alpha[l], g_gla[l], w_br_gla[l], batch, seq)
        mixed = _dsa_mixer(p, ya, rel_bias, w_br_att[l], batch, seq)
        x2 = _out_proj_moe(x2, mixed, w_out[l], g_ffn[l], w_rg[l], b_rg[l], w_re[l], b_re[l],
                           w_gate[l], w_up[l], w_down[l])
    return x2.reshape(batch, seq, d)
```
